```python
import math
import jax, jax.numpy as jnp
from jax import lax
import numpy as np

D_MODEL = 1024
BATCH = 2
SEQ = 8192
DEPTH = 1

N_META = 16
RWKV_HEADS = 8
RWKV_HEAD_DIM = 64
RWKV_DIM = RWKV_HEADS * RWKV_HEAD_DIM
DECAY_LORA = 64
AAA_LORA = 64
GATE_LORA = 128
RWKV_GN_EPS = RWKV_HEAD_DIM * 1e-5
ATTN_Q_HEADS = 8
ATTN_KV_HEADS = 2
ATTN_HEAD_DIM = 64
ATTN_GROUP = ATTN_Q_HEADS // ATTN_KV_HEADS
ATTN_DIM = ATTN_Q_HEADS * ATTN_HEAD_DIM
KV_DIM = ATTN_KV_HEADS * ATTN_HEAD_DIM
WINDOW = 128
BLOCK = 128
REL_BUCKETS = 32
REL_MAX_EXACT = 16
REL_MAX_DIST = 128
D_FF = 2816
CONV_WIDTH = 3
NORM_EPS = 1e-6
MASK_VALUE = -1e30

RWKV_IN = 3 * RWKV_DIM + DECAY_LORA + AAA_LORA + GATE_LORA
ATTN_IN = ATTN_DIM + 2 * KV_DIM
IN_DIM = RWKV_IN + ATTN_IN
MIX_DIM = RWKV_DIM + ATTN_DIM

kernel_name = "hymba_rwkv7_swa_convffn_layer"


def rms_norm(x, w, eps=NORM_EPS):
    xf = x.astype(jnp.float32)
    y = xf * lax.rsqrt(jnp.mean(xf * xf, axis=-1, keepdims=True) + eps)
    return (y * w.astype(jnp.float32)).astype(x.dtype)


def shift_right(x, n):
    return jnp.pad(x, ((0, 0), (n, 0), (0, 0)))[:, : x.shape[1]]


def t5_bucket(d):
    d = jnp.maximum(d, 0)
    df = jnp.maximum(d, REL_MAX_EXACT).astype(jnp.float32)
    large = REL_MAX_EXACT + (jnp.log(df / REL_MAX_EXACT) / math.log(REL_MAX_DIST / REL_MAX_EXACT)
                             * (REL_BUCKETS - REL_MAX_EXACT)).astype(jnp.int32)
    large = jnp.minimum(large, REL_BUCKETS - 1)
    return jnp.where(d < REL_MAX_EXACT, d, large)


def rwkv7_step(S, inp):
    r, w, k, v, a, b = inp
    sa = jnp.einsum('bhij,bhj->bhi', S, a)
    S = S * w[:, :, None, :] + sa[..., None] * b[:, :, None, :] + v[..., None] * k[:, :, None, :]
    y = jnp.einsum('bhij,bhj->bhi', S, r)
    return S, y


def rwkv7_mixer(p, mu, w0, w_dec_up, a0, a_up, g_up, k_k, k_a, r_k, lnx_w, lnx_b):
    B, L, _ = p.shape
    H, N = RWKV_HEADS, RWKV_HEAD_DIM
    p = p + (shift_right(p, 1) - p) * mu
    r, k, v, wd, ad, gd = jnp.split(
        p, [RWKV_DIM, 2 * RWKV_DIM, 3 * RWKV_DIM, 3 * RWKV_DIM + DECAY_LORA,
            3 * RWKV_DIM + DECAY_LORA + AAA_LORA], axis=-1)
    w_log = -jax.nn.softplus(-(w0 + jnp.tanh(wd) @ w_dec_up)) - 0.5
    decay = jnp.exp(-jnp.exp(w_log.astype(jnp.float32)))
    a = jax.nn.sigmoid(a0 + ad @ a_up)
    g = jax.nn.sigmoid(gd) @ g_up
    heads = lambda t: t.reshape(B, L, H, N).astype(jnp.float32)
    kk = heads(k * k_k)
    kk = kk / jnp.maximum(jnp.sqrt(jnp.sum(kk * kk, axis=-1, keepdims=True)), 1e-12)
    k = k * (1.0 + (a - 1.0) * k_a)
    r_h, k_h, v_h, a_h, w_h = heads(r), heads(k), heads(v), heads(a), heads(decay)
    xs = tuple(jnp.moveaxis(t, 1, 0) for t in (r_h, w_h, k_h, v_h, -kk, kk * a_h))
    S0 = jnp.zeros((B, H, N, N), jnp.float32)
    _, y = lax.scan(rwkv7_step, S0, xs)
    y = jnp.moveaxis(y, 0, 1)
    mean = jnp.mean(y, axis=-1, keepdims=True)
    var = jnp.mean(jnp.square(y - mean), axis=-1, keepdims=True)
    y = ((y - mean) * lax.rsqrt(var + RWKV_GN_EPS)).reshape(B, L, RWKV_DIM)
    y = y * lnx_w.astype(jnp.float32) + lnx_b.astype(jnp.float32)
    bonus = jnp.sum(r_h * k_h * r_k.astype(jnp.float32), axis=-1, keepdims=True) * v_h
    y = (y + bonus.reshape(B, L, RWKV_DIM)) * g.astype(jnp.float32)
    return y.astype(p.dtype)


def swa_mixer(p, q_norm_w, k_norm_w, rel_bias, sinks):
    B, L, _ = p.shape
    q, k, v = jnp.split(p, [ATTN_DIM, ATTN_DIM + KV_DIM], axis=-1)
    q = rms_norm(q.reshape(B, L, ATTN_Q_HEADS, ATTN_HEAD_DIM), q_norm_w)
    k = rms_norm(k.reshape(B, L, ATTN_KV_HEADS, ATTN_HEAD_DIM), k_norm_w)
    v = v.reshape(B, L, ATTN_KV_HEADS, ATTN_HEAD_DIM)
    meta_k, meta_v = k[:, :N_META], v[:, :N_META]
    pad = BLOCK - N_META
    padf = lambda t: jnp.pad(t, ((0, 0), (pad, 0), (0, 0), (0, 0)))
    q, k, v = padf(q), padf(k), padf(v)
    nb = (L + pad) // BLOCK
    qb = q.reshape(B, nb, BLOCK, ATTN_KV_HEADS, ATTN_GROUP, ATTN_HEAD_DIM)

    def band(t):
        t = t.reshape(B, nb, BLOCK, ATTN_KV_HEADS, ATTN_HEAD_DIM)
        prev = jnp.pad(t, ((0, 0), (1, 0), (0, 0), (0, 0), (0, 0)))[:, :nb]
        return jnp.concatenate([prev, t], axis=2)

    kband, vband = band(k), band(v)
    scale = ATTN_HEAD_DIM ** -0.5
    s_meta = jnp.einsum('bnqhgd,bmhd->bhgnqm', qb, meta_k)
    s_band = jnp.einsum('bnqhgd,bnkhd->bhgnqk', qb, kband)
    qpos = jnp.arange(nb)[:, None] * BLOCK + jnp.arange(BLOCK)[None, :] - pad
    kpos = (jnp.arange(nb)[:, None] - 1) * BLOCK + jnp.arange(2 * BLOCK)[None, :] - pad
    d_band = qpos[:, :, None] - kpos[:, None, :]
    mask_band = (kpos[:, None, :] >= N_META) & (d_band >= 0) & (d_band < WINDOW)
    d_meta = qpos[:, :, None] - jnp.arange(N_META)[None, None, :]
    mask_meta = d_meta >= 0
    d_all = jnp.concatenate([d_meta, d_band], axis=-1)
    mask_all = jnp.concatenate([mask_meta, mask_band], axis=-1)
    bias = rel_bias[t5_bucket(d_all)]
    bias = jnp.moveaxis(bias, -1, 0).reshape(ATTN_KV_HEADS, ATTN_GROUP, nb, BLOCK, -1)
    s = jnp.concatenate([s_meta, s_band], axis=-1).astype(jnp.float32) * scale + bias.astype(jnp.float32)
    s = jnp.where(mask_all, s, MASK_VALUE)
    sink = sinks.astype(jnp.float32).reshape(ATTN_KV_HEADS, ATTN_GROUP, 1, 1, 1)
    m = jnp.maximum(jnp.max(s, axis=-1, keepdims=True), sink)
    e = jnp.exp(s - m)
    prob = (e / (jnp.sum(e, axis=-1, keepdims=True) + jnp.exp(sink - m))).astype(v.dtype)
    o = (jnp.einsum('bhgnqm,bmhd->bnqhgd', prob[..., :N_META], meta_v)
         + jnp.einsum('bhgnqk,bnkhd->bnqhgd', prob[..., N_META:], vband))
    return o.reshape(B, nb * BLOCK, ATTN_DIM)[:, pad:]


def conv_ffn(u, w_up, conv_w, conv_b, w_down):
    hid = u @ w_up
    hid = (hid * conv_w[0] + shift_right(hid, 1) * conv_w[1]
           + shift_right(hid, 2) * conv_w[2] + conv_b)
    gate, val = jnp.split(hid, 2, axis=-1)
    return (jax.nn.silu(gate) * val) @ w_down


def setup_inputs(seed: int = 0) -> dict:
    key = jax.random.key(seed)
    ks = jax.random.split(key, 32)
    nrm = lambda k, shape, s: jax.random.normal(k, shape, jnp.float32) * s
    near_one = lambda k, shape: 1.0 + 0.02 * jax.random.normal(k, shape, jnp.float32)
    return {
        "x": nrm(ks[0], (BATCH, SEQ, D_MODEL), 1.0),
        "meta_tokens": nrm(ks[1], (N_META, D_MODEL), 1.0),
        "rel_bias": nrm(ks[2], (REL_BUCKETS, ATTN_Q_HEADS), 0.5),
        "norm1_w": near_one(ks[3], (DEPTH, D_MODEL)),
        "w_in": nrm(ks[4], (DEPTH, D_MODEL, IN_DIM), D_MODEL ** -0.5),
        "shift_mu": jax.random.uniform(ks[5], (DEPTH, RWKV_IN), jnp.float32, 0.0, 1.0),
        "decay_w0": jax.random.uniform(ks[6], (DEPTH, RWKV_DIM), jnp.float32, -5.0, -1.0),
        "decay_up": nrm(ks[7], (DEPTH, DECAY_LORA, RWKV_DIM), 0.1 * DECAY_LORA ** -0.5),
        "aaa_a0": nrm(ks[8], (DEPTH, RWKV_DIM), 0.1),
        "aaa_up": nrm(ks[9], (DEPTH, AAA_LORA, RWKV_DIM), 0.5 * AAA_LORA ** -0.5),
        "gate_up": nrm(ks[10], (DEPTH, GATE_LORA, RWKV_DIM), GATE_LORA ** -0.5),
        "k_k": 0.85 + nrm(ks[11], (DEPTH, RWKV_DIM), 0.02),
        "k_a": near_one(ks[12], (DEPTH, RWKV_DIM)),
        "r_k": nrm(ks[13], (DEPTH, RWKV_HEADS, RWKV_HEAD_DIM), 0.1),
        "lnx_w": near_one(ks[14], (DEPTH, RWKV_DIM)),
        "lnx_b": nrm(ks[15], (DEPTH, RWKV_DIM), 0.02),
        "q_norm_w": near_one(ks[16], (DEPTH, ATTN_HEAD_DIM)),
        "k_norm_w": near_one(ks[17], (DEPTH, ATTN_HEAD_DIM)),
        "sinks": nrm(ks[18], (DEPTH, ATTN_Q_HEADS), 1.0),
        "w_out": nrm(ks[19], (DEPTH, MIX_DIM, D_MODEL), MIX_DIM ** -0.5),
        "norm2_w": near_one(ks[20], (DEPTH, D_MODEL)),
        "w_up": nrm(ks[21], (DEPTH, D_MODEL, 2 * D_FF), D_MODEL ** -0.5),
        "conv_w": nrm(ks[22], (DEPTH, CONV_WIDTH, 2 * D_FF), CONV_WIDTH ** -0.5),
        "conv_b": nrm(ks[23], (DEPTH, 2 * D_FF), 0.02),
        "w_down": nrm(ks[24], (DEPTH, D_FF, D_MODEL), D_FF ** -0.5),
    }


def reference(x, meta_tokens, rel_bias, norm1_w, w_in, shift_mu, decay_w0, decay_up, aaa_a0,
              aaa_up, gate_up, k_k, k_a, r_k, lnx_w, lnx_b, q_norm_w, k_norm_w, sinks, w_out,
              norm2_w, w_up, conv_w, conv_b, w_down):
    B = x.shape[0]
    meta = jnp.broadcast_to(meta_tokens[None].astype(x.dtype), (B, N_META, D_MODEL))
    h = jnp.concatenate([meta, x], axis=1)
    for layer in range(DEPTH):
        u = rms_norm(h, norm1_w[layer])
        p = u @ w_in[layer]
        y_rwkv = rwkv7_mixer(p[..., :RWKV_IN], shift_mu[layer], decay_w0[layer], decay_up[layer],
                             aaa_a0[layer], aaa_up[layer], gate_up[layer], k_k[layer], k_a[layer],
                             r_k[layer], lnx_w[layer], lnx_b[layer])
        y_attn = swa_mixer(p[..., RWKV_IN:], q_norm_w[layer], k_norm_w[layer], rel_bias, sinks[layer])
        h = h + jnp.concatenate([y_rwkv, y_attn], axis=-1) @ w_out[layer]
        u = rms_norm(h, norm2_w[layer])
        h = h + conv_ffn(u, w_up[layer], conv_w[layer], conv_b[layer], w_down[layer])
    return h[:, N_META:]
```

```python
import functools
import math

import jax
import jax.numpy as jnp
from jax import lax
from jax.experimental import pallas as pl
from jax.experimental.pallas import tpu as pltpu

F32 = jnp.float32
BF16 = jnp.bfloat16
HI = lax.Precision.HIGHEST

D_MODEL = 1024
N_META = 16
BLOCK = 128
PAD = BLOCK - N_META
HEADS = 8
HEAD_DIM = 64
RWKV_DIM = 512
RWKV_IN = 1792
ATTN_DIM = 512
KV_DIM = 128
IN_DIM = 2560
D_FF = 2816
CHUNK = 64
NORM_EPS = 1e-6
GN_EPS = HEAD_DIM * 1e-5
MASK_VALUE = -1e30
REL_BUCKETS = 32
REL_MAX_EXACT = 16
REL_MAX_DIST = 128
VMEM_LIMIT = 56 * 1024 * 1024


def _dot(a, b, precision=None):
    return lax.dot_general(a, b, (((1,), (0,)), ((), ())), precision=precision,
                           preferred_element_type=F32)


def _dot_nt(a, b, precision=None):
    return lax.dot_general(a, b, (((1,), (1,)), ((), ())), precision=precision,
                           preferred_element_type=F32)


def _head_ones(n):
    r = lax.broadcasted_iota(jnp.int32, (n, n), 0) // HEAD_DIM
    c = lax.broadcasted_iota(jnp.int32, (n, n), 1) // HEAD_DIM
    return (r == c).astype(F32)


def _inproj_kernel(h_ref, nw_ref, w_ref, pr_ref, q_ref, kv_ref):
    x = h_ref[...]
    ms = jnp.mean(x * x, axis=-1, keepdims=True)
    u = (x * lax.rsqrt(ms + NORM_EPS) * nw_ref[...]).astype(BF16)
    for j in range(IN_DIM // 256):
        c = _dot(u, w_ref[:, 256 * j:256 * (j + 1)])
        lo = 256 * j
        if lo < RWKV_IN:
            pr_ref[:, lo:lo + 256] = c
        elif lo < RWKV_IN + ATTN_DIM:
            q_ref[:, lo - RWKV_IN:lo - RWKV_IN + 256] = c
        else:
            kv_ref[...] = c


def _inproj(h, norm_w, w_in_bf16, tm):
    tp = h.shape[0]
    return pl.pallas_call(
        _inproj_kernel,
        grid=(tp // tm,),
        in_specs=[
            pl.BlockSpec((tm, D_MODEL), lambda i: (i, 0)),
            pl.BlockSpec((1, D_MODEL), lambda i: (0, 0)),
            pl.BlockSpec((D_MODEL, IN_DIM), lambda i: (0, 0)),
        ],
        out_specs=[
            pl.BlockSpec((tm, RWKV_IN), lambda i: (i, 0)),
            pl.BlockSpec((tm, ATTN_DIM), lambda i: (i, 0)),
            pl.BlockSpec((tm, 2 * KV_DIM), lambda i: (i, 0)),
        ],
        out_shape=[
            jax.ShapeDtypeStruct((tp, RWKV_IN), F32),
            jax.ShapeDtypeStruct((tp, ATTN_DIM), F32),
            jax.ShapeDtypeStruct((tp, 2 * KV_DIM), F32),
        ],
        compiler_params=pltpu.CompilerParams(
            dimension_semantics=("arbitrary",), vmem_limit_bytes=VMEM_LIMIT),
    )(h, norm_w, w_in_bf16)


def _rwkv_prep_kernel(lp, p_ref, halo_ref, mu_ref, w0_ref, decup_ref, a0_ref, aup_ref, gup_ref,
                      kk_ref, ka_ref, r_out, lw_out, k_out, v_out, kn_out, b_out, g_out):
    tm = p_ref.shape[0]
    p = p_ref[...]
    row = lax.broadcasted_iota(jnp.int32, (tm, 1), 0)
    grow = row + pl.program_id(0) * tm
    prev = pltpu.roll(p, shift=1, axis=0)
    prev = jnp.where(row == 0, halo_ref[7:8, :], prev)
    prev = jnp.where(grow % lp == 0, 0.0, prev)
    ps = p + (prev - p) * mu_ref[...]
    r = ps[:, 0:512]
    k = ps[:, 512:1024]
    v = ps[:, 1024:1536]
    wa = ps[:, 1536:1664]
    gd = ps[:, 1664:1792]
    z = w0_ref[...] + _dot(jnp.tanh(wa), decup_ref[...], HI)
    nz = -z
    softplus = jnp.maximum(nz, 0.0) + jnp.log1p(jnp.exp(-jnp.abs(nz)))
    w_log = -softplus - 0.5
    lw_out[...] = -jnp.exp(w_log)
    a = jax.nn.sigmoid(a0_ref[...] + _dot(wa, aup_ref[...], HI))
    g_out[...] = _dot(jax.nn.sigmoid(gd), gup_ref[...], HI)
    kk = k * kk_ref[...]
    ss = _dot(kk * kk, _head_ones(RWKV_DIM), HI)
    kn = kk / jnp.maximum(jnp.sqrt(ss), 1e-12)
    r_out[...] = r
    k_out[...] = k * (1.0 + (a - 1.0) * ka_ref[...])
    v_out[...] = v
    kn_out[...] = kn
    b_out[...] = kn * a


def _rwkv_prep(pr, lp, mu, w0, decup_pad, a0, aup_pad, gup, k_k, k_a, tm):
    tp = pr.shape[0]
    row = lambda i: (i, 0)
    const = lambda i: (0, 0)
    vec = pl.BlockSpec((1, RWKV_DIM), const)
    lora = pl.BlockSpec((128, RWKV_DIM), const)
    out = pl.BlockSpec((tm, RWKV_DIM), row)
    return pl.pallas_call(
        functools.partial(_rwkv_prep_kernel, lp),
        grid=(tp // tm,),
        in_specs=[
            pl.BlockSpec((tm, RWKV_IN), row),
            pl.BlockSpec((8, RWKV_IN), lambda i: (jnp.maximum(i * (tm // 8) - 1, 0), 0)),
            pl.BlockSpec((1, RWKV_IN), const),
            vec, lora, vec, lora, lora, vec, vec,
        ],
        out_specs=[out] * 7,
        out_shape=[jax.ShapeDtypeStruct((tp, RWKV_DIM), F32)] * 7,
        compiler_params=pltpu.CompilerParams(
            dimension_semantics=("arbitrary",), vmem_limit_bytes=VMEM_LIMIT),
    )(pr, pr, mu, w0, decup_pad, a0, aup_pad, gup, k_k, k_a)


def _unit_lower_inverse(a, blk_id, lower):
    n = a.shape[0]
    eye = (lax.broadcasted_iota(jnp.int32, (n, n), 0) ==
           lax.broadcasted_iota(jnp.int32, (n, n), 1)).astype(F32)
    r8, c8 = blk_id(8)
    d = jnp.where(r8 == c8, a, 0.0)
    d2 = _dot(d, d, HI)
    d4 = _dot(d2, d2, HI)
    t = eye + d
    t = t + _dot(t, d2, HI)
    t = t + _dot(t, d4, HI)
    size = 8
    while size < CHUNK:
        rs, cs = blk_id(size)
        r2, c2 = blk_id(2 * size)
        off = jnp.where((r2 == c2) & (rs != cs) & lower, a, 0.0)
        t = t + _dot(t, _dot(off, t, HI), HI)
        size *= 2
    return t


def _rwkv_scan_kernel(r_ref, lw_ref, k_ref, v_ref, kn_ref, b_ref, g_ref, lnw_ref, lnb_ref, rk_ref,
                      o_ref, s_ref):
    @pl.when(pl.program_id(1) == 0)
    def _():
        s_ref[...] = jnp.zeros_like(s_ref)

    c = CHUNK
    lw = lw_ref[...]
    tri = (lax.broadcasted_iota(jnp.int32, (c, c), 0) >=
           lax.broadcasted_iota(jnp.int32, (c, c), 1)).astype(F32)
    cum = _dot(tri, lw, HI)
    cum_last = cum[c - 1:c, :]
    e_neg = jnp.exp(-cum)
    e_end = jnp.exp(cum_last - cum)
    rt = r_ref[...] * jnp.exp(cum)
    at = -kn_ref[...] * jnp.exp(cum - lw)
    kt = k_ref[...] * e_neg
    bt = b_ref[...] * e_neg
    kh = k_ref[...] * e_end
    bh = b_ref[...] * e_end
    wc = jnp.exp(cum_last)
    v = v_ref[...]

    n2 = 2 * c
    ri = lax.broadcasted_iota(jnp.int32, (n2, n2), 0)
    ci = lax.broadcasted_iota(jnp.int32, (n2, n2), 1)
    same_head = (ri // c) == (ci // c)
    strict = same_head & (ci < ri)
    incl = same_head & (ci <= ri)
    eye = ri == ci
    blk_id = lambda s: (ri // s, ci // s)
    lane_lo = lax.broadcasted_iota(jnp.int32, (c, 128), 1) < HEAD_DIM
    ones = _head_ones(128)

    def stack(z):
        return jnp.concatenate([jnp.where(lane_lo, z, 0.0), jnp.where(lane_lo, 0.0, z)], axis=0)

    for j in range(HEADS // 2):
        sl = slice(128 * j, 128 * (j + 1))
        xa, xr, vb = stack(at[:, sl]), stack(rt[:, sl]), stack(v[:, sl])
        yb = jnp.concatenate([bt[:, sl], bt[:, sl]], axis=0)
        yk = jnp.concatenate([kt[:, sl], kt[:, sl]], axis=0)
        pm = _dot_nt(jnp.concatenate([xa, xr], axis=0), jnp.concatenate([yb, yk], axis=0), HI)
        a_ab = jnp.where(strict, pm[0:n2, 0:n2], 0.0)
        a_ak = jnp.where(strict, pm[0:n2, n2:2 * n2], 0.0)
        p_rb = jnp.where(incl, pm[n2:2 * n2, 0:n2], 0.0)
        p_rk = jnp.where(incl, pm[n2:2 * n2, n2:2 * n2], 0.0)
        t = _unit_lower_inverse(a_ab, blk_id, ci < ri)
        st = s_ref[j]
        u = _dot(t, _dot(xa, st, HI) + _dot(a_ak, vb, HI), HI)
        yo = _dot(xr, st, HI) + _dot(p_rb, u, HI) + _dot(p_rk, vb, HI)
        y = yo[0:c, :] + yo[c:n2, :]
        wc_col = jnp.sum(jnp.where(eye, jnp.broadcast_to(wc[:, sl], (n2, n2)), 0.0),
                         axis=1, keepdims=True)
        s_ref[j] = (wc_col * st + _dot(stack(bh[:, sl]).T, u, HI)
                    + _dot(stack(kh[:, sl]).T, vb, HI))
        mean = _dot(y, ones, HI) * (1.0 / HEAD_DIM)
        dy = y - mean
        var = _dot(dy * dy, ones, HI) * (1.0 / HEAD_DIM)
        yn = dy * lax.rsqrt(var + GN_EPS) * lnw_ref[:, sl] + lnb_ref[:, sl]
        bonus = _dot(r_ref[:, sl] * k_ref[:, sl] * rk_ref[:, sl], ones, HI) * v[:, sl]
        o_ref[:, sl] = (yn + bonus) * g_ref[:, sl]


def _rwkv_scan(r, lw, k, v, kn, b, g, lnx_w, lnx_b, r_k, batch, lp):
    nc = lp // CHUNK
    blk = pl.BlockSpec((CHUNK, RWKV_DIM), lambda bi, ci: (bi * nc + ci, 0))
    vec = pl.BlockSpec((1, RWKV_DIM), lambda bi, ci: (0, 0))
    return pl.pallas_call(
        _rwkv_scan_kernel,
        grid=(batch, nc),
        in_specs=[blk] * 7 + [vec] * 3,
        out_specs=blk,
        out_shape=jax.ShapeDtypeStruct(r.shape, F32),
        scratch_shapes=[pltpu.VMEM((HEADS // 2, 128, 128), F32)],
        compiler_params=pltpu.CompilerParams(
            dimension_semantics=("arbitrary", "arbitrary"), vmem_limit_bytes=VMEM_LIMIT),
    )(r, lw, k, v, kn, b, g, lnx_w, lnx_b, r_k)


def _t5_bucket(d):
    d = jnp.maximum(d, 0)
    df = jnp.maximum(d, REL_MAX_EXACT).astype(F32)
    large = REL_MAX_EXACT + (jnp.log(df / REL_MAX_EXACT) / math.log(REL_MAX_DIST / REL_MAX_EXACT)
                             * (REL_BUCKETS - REL_MAX_EXACT)).astype(jnp.int32)
    large = jnp.minimum(large, REL_BUCKETS - 1)
    return jnp.where(d < REL_MAX_EXACT, d, large)


def _attn_kernel(rb_ref, sink_ref, q_ref, kv0_ref, kvp_ref, kvc_ref, qw_ref, kw_ref, o_ref, tbl_ref):
    n = pl.program_id(1)

    @pl.when((pl.program_id(0) == 0) & (n == 0))
    def _():
        q = lax.broadcasted_iota(jnp.int32, (BLOCK, 3 * BLOCK), 0)
        col = lax.broadcasted_iota(jnp.int32, (BLOCK, 3 * BLOCK), 1)
        is_meta = col < BLOCK
        for nn in range(3):
            d_meta = nn * BLOCK + q - col
            d = jnp.where(is_meta, d_meta, q + 2 * BLOCK - col)
            ok = is_meta & (col >= PAD) & (d_meta >= 0)
            if nn >= 2:
                ok = ok | ((col >= BLOCK) & (col < 2 * BLOCK) & (col - BLOCK > q))
            if nn >= 1:
                ok = ok | ((col >= 2 * BLOCK) & (col - 2 * BLOCK <= q))
            bucket = _t5_bucket(d)

            def per_head(h, carry):
                acc = jnp.zeros((BLOCK, 3 * BLOCK), F32)
                for bk in range(REL_BUCKETS):
                    acc = jnp.where(bucket == bk, rb_ref[bk, h], acc)
                tbl_ref[nn, h] = jnp.where(ok, acc, MASK_VALUE)
                return carry

            lax.fori_loop(0, HEADS, per_head, 0)

    ones = _head_ones(128)
    lane_lo = lax.broadcasted_iota(jnp.int32, (BLOCK, 128), 1) < HEAD_DIM

    def qk_norm(x, w):
        ms = _dot(x * x, ones, HI) * (1.0 / HEAD_DIM)
        return x * lax.rsqrt(ms + NORM_EPS) * w

    kv = jnp.concatenate([kv0_ref[...], kvp_ref[...], kvc_ref[...]], axis=0)
    kn = qk_norm(kv[:, 0:128], kw_ref[...])
    vv = kv[:, 128:256]
    lane_lo3 = lax.broadcasted_iota(jnp.int32, (3 * BLOCK, 128), 1) < HEAD_DIM
    tsel = jnp.minimum(n, 2)
    scale = HEAD_DIM ** -0.5

    for g in range(2):
        keep = lane_lo3 if g == 0 else jnp.logical_not(lane_lo3)
        km = jnp.where(keep, kn, 0.0)
        vm = jnp.where(keep, vv, 0.0)
        kd = (km + pltpu.roll(km, shift=HEAD_DIM, axis=1)).astype(BF16)
        vd = (vm + pltpu.roll(vm, shift=HEAD_DIM, axis=1)).astype(BF16)
        for jj in range(2):
            col = 2 * g + jj
            sl = slice(128 * col, 128 * (col + 1))
            qn = qk_norm(q_ref[:, sl], qw_ref[...])
            outs = []
            for e in range(2):
                h = 2 * col + e
                qm = jnp.where(lane_lo if e == 0 else jnp.logical_not(lane_lo), qn, 0.0)
                s = _dot_nt(qm.astype(BF16), kd) * scale
                tb = tbl_ref[tsel, h]
                s = jnp.where(tb > 0.5 * MASK_VALUE, s + tb, MASK_VALUE)
                sink = sink_ref[0, h]
                m = jnp.maximum(jnp.max(s, axis=-1, keepdims=True), sink)
                ex = jnp.exp(s - m)
                den = jnp.sum(ex, axis=-1, keepdims=True) + jnp.exp(sink - m)
                prob = ex / den
                outs.append(_dot(prob.astype(BF16), vd))
            o_ref[:, sl] = jnp.where(lane_lo, outs[0], outs[1])


def _attention(q, kv, rel_bias, sinks, q_norm_w2, k_norm_w2, batch, lp):
    nb = lp // BLOCK
    return pl.pallas_call(
        _attn_kernel,
        grid=(batch, nb),
        in_specs=[
            pl.BlockSpec(memory_space=pltpu.SMEM),
            pl.BlockSpec(memory_space=pltpu.SMEM),
            pl.BlockSpec((BLOCK, ATTN_DIM), lambda b, n: (b * nb + n, 0)),
            pl.BlockSpec((BLOCK, 2 * KV_DIM), lambda b, n: (b * nb, 0)),
            pl.BlockSpec((BLOCK, 2 * KV_DIM), lambda b, n: (b * nb + jnp.maximum(n - 1, 0), 0)),
            pl.BlockSpec((BLOCK, 2 * KV_DIM), lambda b, n: (b * nb + n, 0)),
            pl.BlockSpec((1, 128), lambda b, n: (0, 0)),
            pl.BlockSpec((1, 128), lambda b, n: (0, 0)),
        ],
        out_specs=pl.BlockSpec((BLOCK, ATTN_DIM), lambda b, n: (b * nb + n, 0)),
        out_shape=jax.ShapeDtypeStruct(q.shape, F32),
        scratch_shapes=[pltpu.VMEM((3, HEADS, BLOCK, 3 * BLOCK), F32)],
        compiler_params=pltpu.CompilerParams(
            dimension_semantics=("arbitrary", "arbitrary"), vmem_limit_bytes=VMEM_LIMIT),
    )(rel_bias, sinks, q, kv, kv, kv, q_norm_w2, k_norm_w2)


def _ffn_kernel(lp, h_ref, ya_ref, yb_ref, wo_ref, nw_ref, wup_ref, cw_ref, cb_ref, wdn_ref,
                o_ref, carry_ref, hbuf_ref):
    tm = h_ref.shape[0]
    grow = lax.broadcasted_iota(jnp.int32, (tm, 1), 0) + pl.program_id(0) * tm
    pos = grow % lp

    @pl.when(pl.program_id(0) * tm % lp == 0)
    def _():
        carry_ref[...] = jnp.zeros_like(carry_ref)

    h = (h_ref[...] + _dot(ya_ref[...].astype(BF16), wo_ref[0:RWKV_DIM, :])
         + _dot(yb_ref[...].astype(BF16), wo_ref[RWKV_DIM:, :]))
    h = jnp.where(pos >= PAD, h, 0.0)
    ms = jnp.mean(h * h, axis=-1, keepdims=True)
    u = (h * lax.rsqrt(ms + NORM_EPS) * nw_ref[...]).astype(BF16)
    acc = h
    cw = 256
    for j in range(D_FF // cw):
        halves = []
        for base in (0, D_FF):
            lo = base + cw * j
            hid = _dot(u, wup_ref[:, lo:lo + cw])
            hbuf_ref[0:8, :] = carry_ref[:, lo:lo + cw]
            hbuf_ref[8:8 + tm, :] = hid
            carry_ref[:, lo:lo + cw] = hid[tm - 8:tm, :]
            halves.append(hid * cw_ref[0:1, lo:lo + cw]
                          + hbuf_ref[7:7 + tm, :] * cw_ref[1:2, lo:lo + cw]
                          + hbuf_ref[6:6 + tm, :] * cw_ref[2:3, lo:lo + cw]
                          + cb_ref[:, lo:lo + cw])
        gate, val = halves
        act = (gate * jax.nn.sigmoid(gate) * val).astype(BF16)
        acc = acc + _dot(act, wdn_ref[cw * j:cw * (j + 1), :])
    o_ref[...] = acc


def _ffn(h, ya, yb, w_out, norm_w, w_up, conv_w, conv_b, w_down, lp, tm):
    tp = h.shape[0]
    row = lambda i: (i, 0)
    const = lambda i: (0, 0)
    return pl.pallas_call(
        functools.partial(_ffn_kernel, lp),
        grid=(tp // tm,),
        in_specs=[
            pl.BlockSpec((tm, D_MODEL), row),
            pl.BlockSpec((tm, RWKV_DIM), row),
            pl.BlockSpec((tm, ATTN_DIM), row),
            pl.BlockSpec((D_MODEL, D_MODEL), const),
            pl.BlockSpec((1, D_MODEL), const),
            pl.BlockSpec((D_MODEL, 2 * D_FF), const),
            pl.BlockSpec((3, 2 * D_FF), const),
            pl.BlockSpec((1, 2 * D_FF), const),
            pl.BlockSpec((D_FF, D_MODEL), const),
        ],
        out_specs=pl.BlockSpec((tm, D_MODEL), row),
        out_shape=jax.ShapeDtypeStruct(h.shape, F32),
        scratch_shapes=[pltpu.VMEM((8, 2 * D_FF), F32), pltpu.VMEM((tm + 8, 256), F32)],
        compiler_params=pltpu.CompilerParams(
            dimension_semantics=("arbitrary",), vmem_limit_bytes=VMEM_LIMIT),
    )(h, ya, yb, w_out, norm_w, w_up, conv_w, conv_b, w_down)


def _row_tile(lp, cap):
    nb = lp // BLOCK
    best = 1
    for f in range(1, nb + 1):
        if nb % f == 0 and f * BLOCK <= cap:
            best = f
    return best * BLOCK


def kernel(x, meta_tokens, rel_bias, norm1_w, w_in, shift_mu, decay_w0, decay_up, aaa_a0, aaa_up, gate_up, k_k, k_a, r_k, lnx_w, lnx_b, q_norm_w, k_norm_w, sinks, w_out, norm2_w, w_up, conv_w, conv_b, w_down):
    batch, seq, _ = x.shape
    lp = PAD + N_META + seq
    assert lp % BLOCK == 0
    tm = _row_tile(lp, 640)
    meta = jnp.broadcast_to(meta_tokens[None].astype(x.dtype), (batch, N_META, D_MODEL))
    h = jnp.concatenate([jnp.zeros((batch, PAD, D_MODEL), x.dtype), meta, x], axis=1)
    h = h.reshape(batch * lp, D_MODEL)
    row2 = lambda t: t.reshape(1, -1)
    zeros64 = jnp.zeros((64, RWKV_DIM), F32)
    for layer in range(norm1_w.shape[0]):
        pr, q, kv = _inproj(h, row2(norm1_w[layer]), w_in[layer].astype(BF16), tm)
        r, lw, k, v, kn, b, g = _rwkv_prep(
            pr, lp, row2(shift_mu[layer]), row2(decay_w0[layer]),
            jnp.concatenate([decay_up[layer], zeros64], axis=0), row2(aaa_a0[layer]),
            jnp.concatenate([zeros64, aaa_up[layer]], axis=0), gate_up[layer],
            row2(k_k[layer]), row2(k_a[layer]), tm)
        y_rwkv = _rwkv_scan(r, lw, k, v, kn, b, g, row2(lnx_w[layer]), row2(lnx_b[layer]),
                            row2(r_k[layer]), batch, lp)
        y_attn = _attention(q, kv, rel_bias, sinks[layer].reshape(1, HEADS),
                            jnp.tile(q_norm_w[layer], 2).reshape(1, 128),
                            jnp.tile(k_norm_w[layer], 2).reshape(1, 128), batch, lp)
        h = _ffn(h, y_rwkv, y_attn, w_out[layer].astype(BF16), row2(norm2_w[layer]),
                 w_up[layer].astype(BF16), conv_w[layer], row2(conv_b[layer]),
                 w_down[layer].astype(BF16), lp, tm)
    return h.reshape(batch, lp, D_MODEL)[:, PAD + N_META:]
```

```python
import functools
import math

import jax
import jax.numpy as jnp
from jax import lax
from jax.experimental import pallas as pl
from jax.experimental.pallas import tpu as pltpu

F32 = jnp.float32
BF16 = jnp.bfloat16
HI = lax.Precision.HIGHEST

D_MODEL = 1024
N_META = 16
BLOCK = 128
PAD = BLOCK - N_META
HEADS = 8
HEAD_DIM = 64
RWKV_DIM = 512
RWKV_IN = 1792
ATTN_DIM = 512
KV_DIM = 128
IN_DIM = 2560
D_FF = 2816
CHUNK = 64
CHUNKS_PER_STEP = 2
NORM_EPS = 1e-6
GN_EPS = HEAD_DIM * 1e-5
MASK_VALUE = -1e30
REL_BUCKETS = 32
REL_MAX_EXACT = 16
REL_MAX_DIST = 128
VMEM_LIMIT = 56 * 1024 * 1024


def _dot(a, b, precision=None):
    return lax.dot_general(a, b, (((1,), (0,)), ((), ())), precision=precision,
                           preferred_element_type=F32)


def _dot_nt(a, b, precision=None):
    return lax.dot_general(a, b, (((1,), (1,)), ((), ())), precision=precision,
                           preferred_element_type=F32)


def _bf(x):
    return x.astype(BF16)


def _dot_split(x, w_bf16):
    hi = _bf(x)
    lo = _bf(x - hi.astype(F32))
    return _dot(hi, w_bf16) + _dot(lo, w_bf16)


def _head_ones(n, dtype=F32):
    r = lax.broadcasted_iota(jnp.int32, (n, n), 0) // HEAD_DIM
    c = lax.broadcasted_iota(jnp.int32, (n, n), 1) // HEAD_DIM
    return (r == c).astype(dtype)


def _inproj_kernel(h_ref, nw_ref, w_ref, pr_ref, q_ref, kv_ref):
    x = h_ref[...]
    ms = jnp.mean(x * x, axis=-1, keepdims=True)
    u = (x * lax.rsqrt(ms + NORM_EPS) * nw_ref[...]).astype(BF16)
    for j in range(IN_DIM // 256):
        c = _dot(u, w_ref[:, 256 * j:256 * (j + 1)])
        lo = 256 * j
        if lo < RWKV_IN:
            pr_ref[:, lo:lo + 256] = c
        elif lo < RWKV_IN + ATTN_DIM:
            q_ref[:, lo - RWKV_IN:lo - RWKV_IN + 256] = c
        else:
            kv_ref[...] = c


def _inproj(h, norm_w, w_in_bf16, tm):
    tp = h.shape[0]
    return pl.pallas_call(
        _inproj_kernel,
        grid=(tp // tm,),
        in_specs=[
            pl.BlockSpec((tm, D_MODEL), lambda i: (i, 0)),
            pl.BlockSpec((1, D_MODEL), lambda i: (0, 0)),
            pl.BlockSpec((D_MODEL, IN_DIM), lambda i: (0, 0)),
        ],
        out_specs=[
            pl.BlockSpec((tm, RWKV_IN), lambda i: (i, 0)),
            pl.BlockSpec((tm, ATTN_DIM), lambda i: (i, 0)),
            pl.BlockSpec((tm, 2 * KV_DIM), lambda i: (i, 0)),
        ],
        out_shape=[
            jax.ShapeDtypeStruct((tp, RWKV_IN), F32),
            jax.ShapeDtypeStruct((tp, ATTN_DIM), F32),
            jax.ShapeDtypeStruct((tp, 2 * KV_DIM), F32),
        ],
        compiler_params=pltpu.CompilerParams(
            dimension_semantics=("arbitrary",), vmem_limit_bytes=VMEM_LIMIT),
    )(h, norm_w, w_in_bf16)


def _unit_lower_inverse(mats, eye, blk_masks):
    d = [jnp.where(blk_masks[0], a, 0.0) for a in mats]
    db = [_bf(x) for x in d]
    d2 = [_bf(_dot(x, x)) for x in db]
    d4 = [_bf(_dot(x, x)) for x in d2]
    t = [eye + x for x in d]
    t = [x + _dot(_bf(x), y) for x, y in zip(t, d2)]
    t = [x + _dot(_bf(x), y) for x, y in zip(t, d4)]
    for m in blk_masks[1:]:
        tb = [_bf(x) for x in t]
        off = [_bf(jnp.where(m, a, 0.0)) for a in mats]
        inner = [_bf(_dot(o, x)) for o, x in zip(off, tb)]
        t = [x + _dot(xb, i) for x, xb, i in zip(t, tb, inner)]
    return t


def _rwkv_kernel(p_ref, halo_ref, mu_ref, w0_ref, decup_ref, a0_ref, aup_ref, gup_ref,
                 kk_ref, ka_ref, lnw_ref, lnb_ref, rk_ref, o_ref, s_ref):
    step = pl.program_id(0)
    nb, rows, _ = p_ref.shape
    c = CHUNK

    @pl.when(step == 0)
    def _():
        s_ref[...] = jnp.zeros_like(s_ref)

    n2 = 2 * c
    ri = lax.broadcasted_iota(jnp.int32, (n2, n2), 0)
    ci = lax.broadcasted_iota(jnp.int32, (n2, n2), 1)
    same_head = (ri // c) == (ci // c)
    strict = same_head & (ci < ri)
    incl = same_head & (ci <= ri)
    eye = ri == ci
    eye_f = eye.astype(F32)
    blk_masks = [(ri // 8) == (ci // 8)]
    size = 8
    while size < c:
        blk_masks.append(((ri // (2 * size)) == (ci // (2 * size)))
                         & ((ri // size) != (ci // size)) & (ci < ri))
        size *= 2
    lane_lo = lax.broadcasted_iota(jnp.int32, (c, 128), 1) < HEAD_DIM
    ones = _head_ones(128, BF16)
    tri = (lax.broadcasted_iota(jnp.int32, (c, c), 0) >=
           lax.broadcasted_iota(jnp.int32, (c, c), 1)).astype(F32)
    row = lax.broadcasted_iota(jnp.int32, (rows, 1), 0)

    def stack(z):
        return jnp.concatenate([jnp.where(lane_lo, z, 0.0), jnp.where(lane_lo, 0.0, z)], axis=0)

    chains = []
    for b in range(nb):
        p = p_ref[b]
        prev = pltpu.roll(p, shift=1, axis=0)
        prev = jnp.where(row == 0, halo_ref[b, 7:8, :], prev)
        prev = jnp.where((row == 0) & (step == 0), 0.0, prev)
        ps = p + (prev - p) * mu_ref[...]
        r = ps[:, 0:512]
        k = ps[:, 512:1024]
        v = ps[:, 1024:1536]
        wa = ps[:, 1536:1664]
        gd = ps[:, 1664:1792]
        nz = -(w0_ref[...] + _dot(jnp.tanh(wa), decup_ref[...], HI))
        softplus = jnp.maximum(nz, 0.0) + jnp.log1p(jnp.exp(-jnp.abs(nz)))
        lw = -jnp.exp(-softplus - 0.5)
        a = jax.nn.sigmoid(a0_ref[...] + _dot(_bf(wa), aup_ref[...]))
        g = _dot(_bf(jax.nn.sigmoid(gd)), gup_ref[...])
        kk = k * kk_ref[...]
        k2 = kk * kk
        ss = jnp.concatenate([_dot_split(k2[:, 128 * j:128 * (j + 1)], ones)
                              for j in range(HEADS // 2)], axis=1)
        kn = kk / jnp.maximum(jnp.sqrt(ss), 1e-12)
        bb = kn * a
        k = k * (1.0 + (a - 1.0) * ka_ref[...])
        rkr = r * k * rk_ref[...]
        for ch in range(rows // c):
            rs = slice(c * ch, c * (ch + 1))
            lwc = lw[rs]
            cum = _dot(tri, lwc, HI)
            cum_last = cum[c - 1:c, :]
            e_neg = jnp.exp(-cum)
            e_end = jnp.exp(cum_last - cum)
            rt = r[rs] * jnp.exp(cum)
            at = -kn[rs] * jnp.exp(cum - lwc)
            kt, bt = k[rs] * e_neg, bb[rs] * e_neg
            kh, bh = k[rs] * e_end, bb[rs] * e_end
            wc = jnp.exp(cum_last)
            for j in range(HEADS // 2):
                sl = slice(128 * j, 128 * (j + 1))
                xr_f = stack(rt[:, sl])
                chains.append(dict(
                    b=b, ch=ch, j=j, rs=rs, sl=sl, xr_f=xr_f, xr=_bf(xr_f),
                    xa=_bf(stack(at[:, sl])), vb=_bf(stack(v[rs, sl])),
                    bt=_bf(bt[:, sl]), kt=_bf(kt[:, sl]),
                    bht=_bf(stack(bh[:, sl]).T), kht=_bf(stack(kh[:, sl]).T),
                    wc=wc[:, sl], v=v[rs, sl], rkr=rkr[rs, sl], g=g[rs, sl]))

    for cd in chains:
        cd["pm"] = _dot_nt(jnp.concatenate([cd["xa"], cd["xr"]], axis=0),
                           jnp.concatenate([cd["bt"], cd["bt"], cd["kt"], cd["kt"]], axis=0))
    for cd in chains:
        pm = cd.pop("pm")
        cd["a_ab"] = jnp.where(strict, pm[0:n2, 0:n2], 0.0)
        cd["a_ak"] = _bf(jnp.where(strict, pm[0:n2, n2:2 * n2], 0.0))
        cd["p_rb"] = _bf(jnp.where(incl, pm[n2:2 * n2, 0:n2], 0.0))
        cd["p_rk"] = _bf(jnp.where(incl, pm[n2:2 * n2, n2:2 * n2], 0.0))
    tinv = _unit_lower_inverse([cd.pop("a_ab") for cd in chains], eye_f, blk_masks)
    for cd in chains:
        cd["w1"] = _bf(_dot(cd["a_ak"], cd["vb"]))
    for cd, t in zip(chains, tinv):
        cd["gu"] = _bf(_dot(_bf(t), jnp.concatenate([cd["xa"], cd["w1"]], axis=1)))
    for cd in chains:
        cd["nq"] = _dot(cd["p_rb"], cd["gu"])
        cd["bm"] = _dot(cd["bht"], cd["gu"])
    for cd in chains:
        cd["y0"] = cd["nq"][:, n2:2 * n2] + _dot(cd["p_rk"], cd["vb"])
        cd["n_add"] = cd["bm"][:, n2:2 * n2] + _dot(cd["kht"], cd["vb"])
        cd["mq_lhs"] = jnp.concatenate(
            [_bf(cd["bm"][:, 0:n2]), _bf(cd["xr_f"] + cd["nq"][:, 0:n2])], axis=0)
        cd["wc_col"] = jnp.sum(jnp.where(eye, jnp.broadcast_to(cd["wc"], (n2, n2)), 0.0),
                               axis=1, keepdims=True)

    states = {(b, j): s_ref[b, j] for b in range(nb) for j in range(HEADS // 2)}
    for ch in range(rows // c):
        for cd in chains:
            if cd["ch"] != ch:
                continue
            st = states[(cd["b"], cd["j"])]
            mq = _dot(cd["mq_lhs"], _bf(st))
            states[(cd["b"], cd["j"])] = cd["wc_col"] * st + mq[0:n2] + cd["n_add"]
            yo = mq[n2:2 * n2] + cd["y0"]
            cd["y"] = yo[0:c, :] + yo[c:n2, :]
    for (b, j), st in states.items():
        s_ref[b, j] = st

    for cd in chains:
        y, sl = cd["y"], cd["sl"]
        mean = _dot_split(y, ones) * (1.0 / HEAD_DIM)
        dy = y - mean
        var = _dot_split(dy * dy, ones) * (1.0 / HEAD_DIM)
        yn = dy * lax.rsqrt(var + GN_EPS) * lnw_ref[:, sl] + lnb_ref[:, sl]
        bonus = _dot_split(cd["rkr"], ones) * cd["v"]
        o_ref[cd["b"], cd["rs"], sl] = (yn + bonus) * cd["g"]


def _rwkv(pr3, mu, w0, decup_pad, a0, aup_pad, gup, k_k, k_a, lnx_w, lnx_b, r_k):
    batch, lp, _ = pr3.shape
    rows = CHUNK * CHUNKS_PER_STEP
    const = lambda i: (0, 0)
    vec = pl.BlockSpec((1, RWKV_DIM), const)
    lora = pl.BlockSpec((128, RWKV_DIM), const)
    return pl.pallas_call(
        _rwkv_kernel,
        grid=(lp // rows,),
        in_specs=[
            pl.BlockSpec((batch, rows, RWKV_IN), lambda i: (0, i, 0)),
            pl.BlockSpec((batch, 8, RWKV_IN), lambda i: (0, jnp.maximum(i * (rows // 8) - 1, 0), 0)),
            pl.BlockSpec((1, RWKV_IN), const),
            vec, lora, vec, lora, lora, vec, vec, vec, vec, vec,
        ],
        out_specs=pl.BlockSpec((batch, rows, RWKV_DIM), lambda i: (0, i, 0)),
        out_shape=jax.ShapeDtypeStruct((batch, lp, RWKV_DIM), F32),
        scratch_shapes=[pltpu.VMEM((batch, HEADS // 2, 128, 128), F32)],
        compiler_params=pltpu.CompilerParams(
            dimension_semantics=("arbitrary",), vmem_limit_bytes=VMEM_LIMIT),
    )(pr3, pr3, mu, w0, decup_pad, a0, aup_pad, gup, k_k, k_a, lnx_w, lnx_b, r_k)


def _t5_bucket(d):
    d = jnp.maximum(d, 0)
    df = jnp.maximum(d, REL_MAX_EXACT).astype(F32)
    large = REL_MAX_EXACT + (jnp.log(df / REL_MAX_EXACT) / math.log(REL_MAX_DIST / REL_MAX_EXACT)
                             * (REL_BUCKETS - REL_MAX_EXACT)).astype(jnp.int32)
    large = jnp.minimum(large, REL_BUCKETS - 1)
    return jnp.where(d < REL_MAX_EXACT, d, large)


def _attn_kernel(rb_ref, sink_ref, q_ref, kv0_ref, kvp_ref, kvc_ref, qw_ref, kw_ref, o_ref, tbl_ref):
    n = pl.program_id(1)

    @pl.when((pl.program_id(0) == 0) & (n == 0))
    def _():
        q = lax.broadcasted_iota(jnp.int32, (BLOCK, 3 * BLOCK), 0)
        col = lax.broadcasted_iota(jnp.int32, (BLOCK, 3 * BLOCK), 1)
        is_meta = col < BLOCK
        for nn in range(3):
            d_meta = nn * BLOCK + q - col
            d = jnp.where(is_meta, d_meta, q + 2 * BLOCK - col)
            ok = is_meta & (col >= PAD) & (d_meta >= 0)
            if nn >= 2:
                ok = ok | ((col >= BLOCK) & (col < 2 * BLOCK) & (col - BLOCK > q))
            if nn >= 1:
                ok = ok | ((col >= 2 * BLOCK) & (col - 2 * BLOCK <= q))
            bucket = _t5_bucket(d)

            def per_head(h, carry):
                acc = jnp.zeros((BLOCK, 3 * BLOCK), F32)
                for bk in range(REL_BUCKETS):
                    acc = jnp.where(bucket == bk, rb_ref[bk, h], acc)
                tbl_ref[nn, h] = jnp.where(ok, acc, MASK_VALUE)
                return carry

            lax.fori_loop(0, HEADS, per_head, 0)

    ones = _head_ones(128)
    lane_lo = lax.broadcasted_iota(jnp.int32, (BLOCK, 128), 1) < HEAD_DIM

    def qk_norm(x, w):
        ms = _dot(x * x, ones, HI) * (1.0 / HEAD_DIM)
        return x * lax.rsqrt(ms + NORM_EPS) * w

    kv = jnp.concatenate([kv0_ref[...], kvp_ref[...], kvc_ref[...]], axis=0)
    kn = qk_norm(kv[:, 0:128], kw_ref[...])
    vv = kv[:, 128:256]
    lane_lo3 = lax.broadcasted_iota(jnp.int32, (3 * BLOCK, 128), 1) < HEAD_DIM
    tsel = jnp.minimum(n, 2)
    scale = HEAD_DIM ** -0.5

    for g in range(2):
        keep = lane_lo3 if g == 0 else jnp.logical_not(lane_lo3)
        km = jnp.where(keep, kn, 0.0)
        vm = jnp.where(keep, vv, 0.0)
        kd = (km + pltpu.roll(km, shift=HEAD_DIM, axis=1)).astype(BF16)
        vd = (vm + pltpu.roll(vm, shift=HEAD_DIM, axis=1)).astype(BF16)
        for jj in range(2):
            col = 2 * g + jj
            sl = slice(128 * col, 128 * (col + 1))
            qn = qk_norm(q_ref[:, sl], qw_ref[...])
            outs = []
            for e in range(2):
                h = 2 * col + e
                qm = jnp.where(lane_lo if e == 0 else jnp.logical_not(lane_lo), qn, 0.0)
                s = _dot_nt(qm.astype(BF16), kd) * scale
                tb = tbl_ref[tsel, h]
                s = jnp.where(tb > 0.5 * MASK_VALUE, s + tb, MASK_VALUE)
                sink = sink_ref[0, h]
                m = jnp.maximum(jnp.max(s, axis=-1, keepdims=True), sink)
                ex = jnp.exp(s - m)
                den = jnp.sum(ex, axis=-1, keepdims=True) + jnp.exp(sink - m)
                prob = ex / den
                outs.append(_dot(prob.astype(BF16), vd))
            o_ref[:, sl] = jnp.where(lane_lo, outs[0], outs[1])


def _attention(q, kv, rel_bias, sinks, q_norm_w2, k_norm_w2, batch, lp):
    nb = lp // BLOCK
    return pl.pallas_call(
        _attn_kernel,
        grid=(batch, nb),
        in_specs=[
            pl.BlockSpec(memory_space=pltpu.SMEM),
            pl.BlockSpec(memory_space=pltpu.SMEM),
            pl.BlockSpec((BLOCK, ATTN_DIM), lambda b, n: (b * nb + n, 0)),
            pl.BlockSpec((BLOCK, 2 * KV_DIM), lambda b, n: (b * nb, 0)),
            pl.BlockSpec((BLOCK, 2 * KV_DIM), lambda b, n: (b * nb + jnp.maximum(n - 1, 0), 0)),
            pl.BlockSpec((BLOCK, 2 * KV_DIM), lambda b, n: (b * nb + n, 0)),
            pl.BlockSpec((1, 128), lambda b, n: (0, 0)),
            pl.BlockSpec((1, 128), lambda b, n: (0, 0)),
        ],
        out_specs=pl.BlockSpec((BLOCK, ATTN_DIM), lambda b, n: (b * nb + n, 0)),
        out_shape=jax.ShapeDtypeStruct(q.shape, F32),
        scratch_shapes=[pltpu.VMEM((3, HEADS, BLOCK, 3 * BLOCK), F32)],
        compiler_params=pltpu.CompilerParams(
            dimension_semantics=("arbitrary", "arbitrary"), vmem_limit_bytes=VMEM_LIMIT),
    )(rel_bias, sinks, q, kv, kv, kv, q_norm_w2, k_norm_w2)


def _ffn_kernel(lp, h_ref, ya_ref, yb_ref, wo_ref, nw_ref, wup_ref, cw_ref, cb_ref, wdn_ref,
                o_ref, carry_ref, hbuf_ref):
    tm = h_ref.shape[0]
    grow = lax.broadcasted_iota(jnp.int32, (tm, 1), 0) + pl.program_id(0) * tm
    pos = grow % lp

    @pl.when(pl.program_id(0) * tm % lp == 0)
    def _():
        carry_ref[...] = jnp.zeros_like(carry_ref)

    h = (h_ref[...] + _dot(ya_ref[...].astype(BF16), wo_ref[0:RWKV_DIM, :])
         + _dot(yb_ref[...].astype(BF16), wo_ref[RWKV_DIM:, :]))
    h = jnp.where(pos >= PAD, h, 0.0)
    ms = jnp.mean(h * h, axis=-1, keepdims=True)
    u = (h * lax.rsqrt(ms + NORM_EPS) * nw_ref[...]).astype(BF16)
    acc = h
    cw = 256
    for j in range(D_FF // cw):
        halves = []
        for base in (0, D_FF):
            lo = base + cw * j
            hid = _dot(u, wup_ref[:, lo:lo + cw])
            hbuf_ref[0:8, :] = carry_ref[:, lo:lo + cw]
            hbuf_ref[8:8 + tm, :] = hid
            carry_ref[:, lo:lo + cw] = hid[tm - 8:tm, :]
            halves.append(hid * cw_ref[0:1, lo:lo + cw]
                          + hbuf_ref[7:7 + tm, :] * cw_ref[1:2, lo:lo + cw]
                          + hbuf_ref[6:6 + tm, :] * cw_ref[2:3, lo:lo + cw]
                          + cb_ref[:, lo:lo + cw])
        gate, val = halves
        act = (gate * jax.nn.sigmoid(gate) * val).astype(BF16)
        acc = acc + _dot(act, wdn_ref[cw * j:cw * (j + 1), :])
    o_ref[...] = acc


def _ffn(h, ya, yb, w_out, norm_w, w_up, conv_w, conv_b, w_down, lp, tm):
    tp = h.shape[0]
    row = lambda i: (i, 0)
    const = lambda i: (0, 0)
    return pl.pallas_call(
        functools.partial(_ffn_kernel, lp),
        grid=(tp // tm,),
        in_specs=[
            pl.BlockSpec((tm, D_MODEL), row),
            pl.BlockSpec((tm, RWKV_DIM), row),
            pl.BlockSpec((tm, ATTN_DIM), row),
            pl.BlockSpec((D_MODEL, D_MODEL), const),
            pl.BlockSpec((1, D_MODEL), const),
            pl.BlockSpec((D_MODEL, 2 * D_FF), const),
            pl.BlockSpec((3, 2 * D_FF), const),
            pl.BlockSpec((1, 2 * D_FF), const),
            pl.BlockSpec((D_FF, D_MODEL), const),
        ],
        out_specs=pl.BlockSpec((tm, D_MODEL), row),
        out_shape=jax.ShapeDtypeStruct(h.shape, F32),
        scratch_shapes=[pltpu.VMEM((8, 2 * D_FF), F32), pltpu.VMEM((tm + 8, 256), F32)],
        compiler_params=pltpu.CompilerParams(
            dimension_semantics=("arbitrary",), vmem_limit_bytes=VMEM_LIMIT),
    )(h, ya, yb, w_out, norm_w, w_up, conv_w, conv_b, w_down)


def _row_tile(lp, cap):
    nb = lp // BLOCK
    best = 1
    for f in range(1, nb + 1):
        if nb % f == 0 and f * BLOCK <= cap:
            best = f
    return best * BLOCK


def kernel(x, meta_tokens, rel_bias, norm1_w, w_in, shift_mu, decay_w0, decay_up, aaa_a0, aaa_up, gate_up, k_k, k_a, r_k, lnx_w, lnx_b, q_norm_w, k_norm_w, sinks, w_out, norm2_w, w_up, conv_w, conv_b, w_down):
    batch, seq, _ = x.shape
    lp = PAD + N_META + seq
    assert lp % BLOCK == 0
    tm = _row_tile(lp, 640)
    meta = jnp.broadcast_to(meta_tokens[None].astype(x.dtype), (batch, N_META, D_MODEL))
    h = jnp.concatenate([jnp.zeros((batch, PAD, D_MODEL), x.dtype), meta, x], axis=1)
    h = h.reshape(batch * lp, D_MODEL)
    row2 = lambda t: t.reshape(1, -1)
    zeros64 = jnp.zeros((64, RWKV_DIM), F32)
    for layer in range(norm1_w.shape[0]):
        pr, q, kv = _inproj(h, row2(norm1_w[layer]), w_in[layer].astype(BF16), tm)
        y_rwkv = _rwkv(
            pr.reshape(batch, lp, RWKV_IN), row2(shift_mu[layer]), row2(decay_w0[layer]),
            jnp.concatenate([decay_up[layer], zeros64], axis=0), row2(aaa_a0[layer]),
            jnp.concatenate([zeros64, aaa_up[layer]], axis=0).astype(BF16),
            gate_up[layer].astype(BF16), row2(k_k[layer]), row2(k_a[layer]),
            row2(lnx_w[layer]), row2(lnx_b[layer]), row2(r_k[layer]))
        y_attn = _attention(q, kv, rel_bias, sinks[layer].reshape(1, HEADS),
                            jnp.tile(q_norm_w[layer], 2).reshape(1, 128),
                            jnp.tile(k_norm_w[layer], 2).reshape(1, 128), batch, lp)
        h = _ffn(h, y_rwkv.reshape(batch * lp, RWKV_DIM), y_attn, w_out[layer].astype(BF16),
                 row2(norm2_w[layer]), w_up[layer].astype(BF16), conv_w[layer],
                 row2(conv_b[layer]), w_down[layer].astype(BF16), lp, tm)
    return h.reshape(batch, lp, D_MODEL)[:, PAD + N_META:]
```

```python
import functools
import math

import jax
import jax.numpy as jnp
from jax import lax
from jax.experimental import pallas as pl
from jax.experimental.pallas import tpu as pltpu

F32 = jnp.float32
BF16 = jnp.bfloat16
HI = lax.Precision.HIGHEST

D_MODEL = 1024
N_META = 16
BLOCK = 128
PAD = BLOCK - N_META
HEADS = 8
HEAD_DIM = 64
RWKV_DIM = 512
RWKV_IN = 1792
ATTN_DIM = 512
KV_DIM = 128
IN_DIM = 2560
D_FF = 2816
CHUNK = 64
CHUNKS_PER_STEP = 2
NORM_EPS = 1e-6
GN_EPS = HEAD_DIM * 1e-5
MASK_VALUE = -1e30
REL_BUCKETS = 32
REL_MAX_EXACT = 16
REL_MAX_DIST = 128
VMEM_LIMIT = 56 * 1024 * 1024


def _dot(a, b, precision=None):
    return lax.dot_general(a, b, (((1,), (0,)), ((), ())), precision=precision,
                           preferred_element_type=F32)


def _dot_nt(a, b, precision=None):
    return lax.dot_general(a, b, (((1,), (1,)), ((), ())), precision=precision,
                           preferred_element_type=F32)


def _bf(x):
    return x.astype(BF16)


def _dot_split(x, w_bf16):
    hi = _bf(x)
    lo = _bf(x - hi.astype(F32))
    return _dot(hi, w_bf16) + _dot(lo, w_bf16)


def _head_ones(n, dtype=F32):
    r = lax.broadcasted_iota(jnp.int32, (n, n), 0) // HEAD_DIM
    c = lax.broadcasted_iota(jnp.int32, (n, n), 1) // HEAD_DIM
    return (r == c).astype(dtype)


def _inproj_kernel(h_ref, nw_ref, w_ref, pr_ref, q_ref, kv_ref):
    x = h_ref[...]
    ms = jnp.mean(x * x, axis=-1, keepdims=True)
    u = (x * lax.rsqrt(ms + NORM_EPS) * nw_ref[...]).astype(BF16)
    for j in range(IN_DIM // 256):
        c = _dot(u, w_ref[:, 256 * j:256 * (j + 1)])
        lo = 256 * j
        if lo < RWKV_IN:
            pr_ref[:, lo:lo + 256] = c
        elif lo < RWKV_IN + ATTN_DIM:
            q_ref[:, lo - RWKV_IN:lo - RWKV_IN + 256] = c
        else:
            kv_ref[...] = c


def _inproj(h, norm_w, w_in_bf16, tm):
    tp = h.shape[0]
    return pl.pallas_call(
        _inproj_kernel,
        grid=(tp // tm,),
        in_specs=[
            pl.BlockSpec((tm, D_MODEL), lambda i: (i, 0)),
            pl.BlockSpec((1, D_MODEL), lambda i: (0, 0)),
            pl.BlockSpec((D_MODEL, IN_DIM), lambda i: (0, 0)),
        ],
        out_specs=[
            pl.BlockSpec((tm, RWKV_IN), lambda i: (i, 0)),
            pl.BlockSpec((tm, ATTN_DIM), lambda i: (i, 0)),
            pl.BlockSpec((tm, 2 * KV_DIM), lambda i: (i, 0)),
        ],
        out_shape=[
            jax.ShapeDtypeStruct((tp, RWKV_IN), F32),
            jax.ShapeDtypeStruct((tp, ATTN_DIM), F32),
            jax.ShapeDtypeStruct((tp, 2 * KV_DIM), F32),
        ],
        compiler_params=pltpu.CompilerParams(
            dimension_semantics=("arbitrary",), vmem_limit_bytes=VMEM_LIMIT),
    )(h, norm_w, w_in_bf16)


def _unit_lower_inverse(mats, eye, blk_masks):
    d = [jnp.where(blk_masks[0], a, 0.0) for a in mats]
    db = [_bf(x) for x in d]
    d2 = [_bf(_dot(x, x)) for x in db]
    d4 = [_bf(_dot(x, x)) for x in d2]
    t = [eye + x for x in d]
    t = [x + _dot(_bf(x), y) for x, y in zip(t, d2)]
    t = [x + _dot(_bf(x), y) for x, y in zip(t, d4)]
    for m in blk_masks[1:]:
        tb = [_bf(x) for x in t]
        off = [_bf(jnp.where(m, a, 0.0)) for a in mats]
        inner = [_bf(_dot(o, x)) for o, x in zip(off, tb)]
        t = [x + _dot(xb, i) for x, xb, i in zip(t, tb, inner)]
    return t


def _rwkv_kernel(p_ref, halo_ref, mu_ref, w0_ref, decup_ref, a0_ref, aup_ref, gup_ref,
                 kk_ref, ka_ref, lnw_ref, lnb_ref, rk_ref, o_ref, s_ref):
    step = pl.program_id(0)
    nb, rows, _ = p_ref.shape
    c = CHUNK

    @pl.when(step == 0)
    def _():
        s_ref[...] = jnp.zeros_like(s_ref)

    n2 = 2 * c
    ri = lax.broadcasted_iota(jnp.int32, (n2, n2), 0)
    ci = lax.broadcasted_iota(jnp.int32, (n2, n2), 1)
    same_head = (ri // c) == (ci // c)
    strict = same_head & (ci < ri)
    incl = same_head & (ci <= ri)
    eye = ri == ci
    eye_f = eye.astype(F32)
    blk_masks = [(ri // 8) == (ci // 8)]
    size = 8
    while size < c:
        blk_masks.append(((ri // (2 * size)) == (ci // (2 * size)))
                         & ((ri // size) != (ci // size)) & (ci < ri))
        size *= 2
    lane_lo = lax.broadcasted_iota(jnp.int32, (c, 128), 1) < HEAD_DIM
    ones = _head_ones(128, BF16)
    tri = (lax.broadcasted_iota(jnp.int32, (c, c), 0) >=
           lax.broadcasted_iota(jnp.int32, (c, c), 1)).astype(F32)
    row = lax.broadcasted_iota(jnp.int32, (rows, 1), 0)

    def stack(z):
        return jnp.concatenate([jnp.where(lane_lo, z, 0.0), jnp.where(lane_lo, 0.0, z)], axis=0)

    chains = []
    for b in range(nb):
        p = p_ref[b]
        prev = pltpu.roll(p, shift=1, axis=0)
        prev = jnp.where(row == 0, halo_ref[b, 7:8, :], prev)
        prev = jnp.where((row == 0) & (step == 0), 0.0, prev)
        ps = p + (prev - p) * mu_ref[...]
        r = ps[:, 0:512]
        k = ps[:, 512:1024]
        v = ps[:, 1024:1536]
        wa = ps[:, 1536:1664]
        gd = ps[:, 1664:1792]
        nz = -(w0_ref[...] + _dot(jnp.tanh(wa), decup_ref[...], HI))
        softplus = jnp.maximum(nz, 0.0) + jnp.log1p(jnp.exp(-jnp.abs(nz)))
        lw = -jnp.exp(-softplus - 0.5)
        a = jax.nn.sigmoid(a0_ref[...] + _dot(_bf(wa), aup_ref[...]))
        g = _dot(_bf(jax.nn.sigmoid(gd)), gup_ref[...])
        kk = k * kk_ref[...]
        k2 = kk * kk
        ss = jnp.concatenate([_dot_split(k2[:, 128 * j:128 * (j + 1)], ones)
                              for j in range(HEADS // 2)], axis=1)
        kn = kk / jnp.maximum(jnp.sqrt(ss), 1e-12)
        bb = kn * a
        k = k * (1.0 + (a - 1.0) * ka_ref[...])
        rkr = r * k * rk_ref[...]
        for ch in range(rows // c):
            rs = slice(c * ch, c * (ch + 1))
            lwc = lw[rs]
            cum = _dot(tri, lwc, HI)
            cum_last = cum[c - 1:c, :]
            e_neg = jnp.exp(-cum)
            e_end = jnp.exp(cum_last - cum)
            rt = r[rs] * jnp.exp(cum)
            at = -kn[rs] * jnp.exp(cum - lwc)
            kt, bt = k[rs] * e_neg, bb[rs] * e_neg
            kh, bh = k[rs] * e_end, bb[rs] * e_end
            wc = jnp.exp(cum_last)
            for j in range(HEADS // 2):
                sl = slice(128 * j, 128 * (j + 1))
                xr_f = stack(rt[:, sl])
                chains.append(dict(
                    b=b, ch=ch, j=j, rs=rs, sl=sl, xr_f=xr_f, xr=_bf(xr_f),
                    xa=_bf(stack(at[:, sl])), vb=_bf(stack(v[rs, sl])),
                    bt=_bf(bt[:, sl]), kt=_bf(kt[:, sl]),
                    bht=_bf(stack(bh[:, sl]).T), kht=_bf(stack(kh[:, sl]).T),
                    wc=wc[:, sl], v=v[rs, sl], rkr=rkr[rs, sl], g=g[rs, sl]))

    for cd in chains:
        cd["pm"] = _dot_nt(jnp.concatenate([cd["xa"], cd["xr"]], axis=0),
                           jnp.concatenate([cd["bt"], cd["bt"], cd["kt"], cd["kt"]], axis=0))
    for cd in chains:
        pm = cd.pop("pm")
        cd["a_ab"] = jnp.where(strict, pm[0:n2, 0:n2], 0.0)
        cd["a_ak"] = _bf(jnp.where(strict, pm[0:n2, n2:2 * n2], 0.0))
        cd["p_rb"] = _bf(jnp.where(incl, pm[n2:2 * n2, 0:n2], 0.0))
        cd["p_rk"] = _bf(jnp.where(incl, pm[n2:2 * n2, n2:2 * n2], 0.0))
    tinv = _unit_lower_inverse([cd.pop("a_ab") for cd in chains], eye_f, blk_masks)
    for cd in chains:
        cd["w1"] = _bf(_dot(cd["a_ak"], cd["vb"]))
    for cd, t in zip(chains, tinv):
        cd["gu"] = _bf(_dot(_bf(t), jnp.concatenate([cd["xa"], cd["w1"]], axis=1)))
    for cd in chains:
        cd["nq"] = _dot(cd["p_rb"], cd["gu"])
        cd["bm"] = _dot(cd["bht"], cd["gu"])
    for cd in chains:
        cd["y0"] = cd["nq"][:, n2:2 * n2] + _dot(cd["p_rk"], cd["vb"])
        cd["n_add"] = cd["bm"][:, n2:2 * n2] + _dot(cd["kht"], cd["vb"])
        cd["mq_lhs"] = jnp.concatenate(
            [_bf(cd["bm"][:, 0:n2]), _bf(cd["xr_f"] + cd["nq"][:, 0:n2])], axis=0)
        cd["wc_col"] = jnp.sum(jnp.where(eye, jnp.broadcast_to(cd["wc"], (n2, n2)), 0.0),
                               axis=1, keepdims=True)

    states = {(b, j): s_ref[b, j] for b in range(nb) for j in range(HEADS // 2)}
    for ch in range(rows // c):
        for cd in chains:
            if cd["ch"] != ch:
                continue
            st = states[(cd["b"], cd["j"])]
            mq = _dot(cd["mq_lhs"], _bf(st))
            states[(cd["b"], cd["j"])] = cd["wc_col"] * st + mq[0:n2] + cd["n_add"]
            yo = mq[n2:2 * n2] + cd["y0"]
            cd["y"] = yo[0:c, :] + yo[c:n2, :]
    for (b, j), st in states.items():
        s_ref[b, j] = st

    for cd in chains:
        y, sl = cd["y"], cd["sl"]
        mean = _dot_split(y, ones) * (1.0 / HEAD_DIM)
        dy = y - mean
        var = _dot_split(dy * dy, ones) * (1.0 / HEAD_DIM)
        yn = dy * lax.rsqrt(var + GN_EPS) * lnw_ref[:, sl] + lnb_ref[:, sl]
        bonus = _dot_split(cd["rkr"], ones) * cd["v"]
        o_ref[cd["b"], cd["rs"], sl] = (yn + bonus) * cd["g"]


def _rwkv(pr3, mu, w0, decup_pad, a0, aup_pad, gup, k_k, k_a, lnx_w, lnx_b, r_k):
    batch, lp, _ = pr3.shape
    rows = CHUNK * CHUNKS_PER_STEP
    const = lambda i: (0, 0)
    vec = pl.BlockSpec((1, RWKV_DIM), const)
    lora = pl.BlockSpec((128, RWKV_DIM), const)
    return pl.pallas_call(
        _rwkv_kernel,
        grid=(lp // rows,),
        in_specs=[
            pl.BlockSpec((batch, rows, RWKV_IN), lambda i: (0, i, 0)),
            pl.BlockSpec((batch, 8, RWKV_IN), lambda i: (0, jnp.maximum(i * (rows // 8) - 1, 0), 0)),
            pl.BlockSpec((1, RWKV_IN), const),
            vec, lora, vec, lora, lora, vec, vec, vec, vec, vec,
        ],
        out_specs=pl.BlockSpec((batch, rows, RWKV_DIM), lambda i: (0, i, 0)),
        out_shape=jax.ShapeDtypeStruct((batch, lp, RWKV_DIM), F32),
        scratch_shapes=[pltpu.VMEM((batch, HEADS // 2, 128, 128), F32)],
        compiler_params=pltpu.CompilerParams(
            dimension_semantics=("arbitrary",), vmem_limit_bytes=VMEM_LIMIT),
    )(pr3, pr3, mu, w0, decup_pad, a0, aup_pad, gup, k_k, k_a, lnx_w, lnx_b, r_k)


def _t5_bucket(d):
    d = jnp.maximum(d, 0)
    df = jnp.maximum(d, REL_MAX_EXACT).astype(F32)
    large = REL_MAX_EXACT + (jnp.log(df / REL_MAX_EXACT) / math.log(REL_MAX_DIST / REL_MAX_EXACT)
                             * (REL_BUCKETS - REL_MAX_EXACT)).astype(jnp.int32)
    large = jnp.minimum(large, REL_BUCKETS - 1)
    return jnp.where(d < REL_MAX_EXACT, d, large)


def _attn_kernel(rb_ref, sink_ref, q_ref, kv0_ref, kvp_ref, kvc_ref, qw_ref, kw_ref, o_ref, tbl_ref):
    n = pl.program_id(1)

    @pl.when((pl.program_id(0) == 0) & (n == 0))
    def _():
        q = lax.broadcasted_iota(jnp.int32, (BLOCK, 3 * BLOCK), 0)
        col = lax.broadcasted_iota(jnp.int32, (BLOCK, 3 * BLOCK), 1)
        is_meta = col < BLOCK
        for nn in range(3):
            d_meta = nn * BLOCK + q - col
            d = jnp.where(is_meta, d_meta, q + 2 * BLOCK - col)
            ok = is_meta & (col >= PAD) & (d_meta >= 0)
            if nn >= 2:
                ok = ok | ((col >= BLOCK) & (col < 2 * BLOCK) & (col - BLOCK > q))
            if nn >= 1:
                ok = ok | ((col >= 2 * BLOCK) & (col - 2 * BLOCK <= q))
            bucket = _t5_bucket(d)

            def per_head(h, carry):
                acc = jnp.zeros((BLOCK, 3 * BLOCK), F32)
                for bk in range(REL_BUCKETS):
                    acc = jnp.where(bucket == bk, rb_ref[bk, h], acc)
                tbl_ref[nn, h] = jnp.where(ok, acc, MASK_VALUE)
                return carry

            lax.fori_loop(0, HEADS, per_head, 0)

    ones = _head_ones(128, BF16)
    lane_lo = lax.broadcasted_iota(jnp.int32, (BLOCK, 128), 1) < HEAD_DIM
    lane_hi = jnp.logical_not(lane_lo)
    rr = lax.broadcasted_iota(jnp.int32, (128, 128), 0)
    cc = lax.broadcasted_iota(jnp.int32, (128, 128), 1)
    dup = [((rr // HEAD_DIM == g) & (rr % HEAD_DIM == cc % HEAD_DIM)).astype(BF16) for g in range(2)]

    def qk_norm(x, w):
        ms = _dot(_bf(x * x), ones) * (1.0 / HEAD_DIM)
        return x * lax.rsqrt(ms + NORM_EPS) * w

    nbk = q_ref.shape[0] // BLOCK
    grp = HEADS // 2
    qw = qw_ref[...] * HEAD_DIM ** -0.5

    kvb = [kv0_ref[...], kvp_ref[...]] + [kvc_ref[BLOCK * jb:BLOCK * (jb + 1), :] for jb in range(nbk)]
    kn = [_bf(qk_norm(x[:, 0:128], kw_ref[...])) for x in kvb]
    kd = [[_bf(_dot(x, dup[g])) for x in kn] for g in range(2)]
    vd = [[_bf(_dot(_bf(x[:, 128:256]), dup[g])) for x in kvb] for g in range(2)]
    qn = [[qk_norm(q_ref[BLOCK * jb:BLOCK * (jb + 1), 128 * c4:128 * (c4 + 1)], qw)
           for c4 in range(4)] for jb in range(nbk)]
    sinks = [jnp.concatenate([jnp.full((BLOCK, 128), sink_ref[0, grp * g + i], F32)
                              for i in range(grp)], axis=0) for g in range(2)]
    ones_cols = jnp.ones((3 * BLOCK, 128), BF16)

    chains = [(jb, g) for jb in range(nbk) for g in range(2)]
    s = []
    for jb, g in chains:
        lhs = jnp.concatenate(
            [jnp.where(lane_lo if e == 0 else lane_hi, qn[jb][2 * g + jj], 0.0)
             for jj in range(2) for e in range(2)], axis=0).astype(BF16)
        keys = jnp.concatenate([kd[g][0], kd[g][1 + jb], kd[g][2 + jb]], axis=0)
        tsel = jnp.minimum(n * nbk + jb, 2)
        tb = tbl_ref[tsel, pl.ds(grp * g, grp)].reshape(grp * BLOCK, 3 * BLOCK)
        s.append(_dot_nt(lhs, keys) + tb)
    mb = []
    for x, (_, g) in zip(s, chains):
        m3 = jnp.maximum(jnp.maximum(x[:, 0:128], x[:, 128:256]), x[:, 256:384])
        m = jnp.maximum(jnp.max(m3, axis=-1, keepdims=True), sinks[g][:, 0:1])
        mb.append(jnp.broadcast_to(m, (grp * BLOCK, 128)))
    ex = [jnp.concatenate([_bf(jnp.exp(x[:, 128 * i:128 * (i + 1)] - mm)) for i in range(3)], axis=1)
          for x, mm in zip(s, mb)]
    out = []
    for x, mm, (jb, g) in zip(ex, mb, chains):
        vals = jnp.concatenate([vd[g][0], vd[g][1 + jb], vd[g][2 + jb]], axis=0)
        od = _dot(x, jnp.concatenate([vals, ones_cols], axis=1))
        den = od[:, 128:256] + jnp.exp(sinks[g] - mm)
        out.append(od[:, 0:128] * (1.0 / den))
    for o, (jb, g) in zip(out, chains):
        for jj in range(2):
            col = 2 * g + jj
            o_ref[BLOCK * jb:BLOCK * (jb + 1), 128 * col:128 * (col + 1)] = jnp.where(
                lane_lo, o[256 * jj:256 * jj + 128], o[256 * jj + 128:256 * jj + 256])


def _attention(q, kv, rel_bias, sinks, q_norm_w2, k_norm_w2, batch, lp, tm):
    nb = lp // BLOCK
    nt = lp // tm
    nbk = tm // BLOCK
    return pl.pallas_call(
        _attn_kernel,
        grid=(batch, nt),
        in_specs=[
            pl.BlockSpec(memory_space=pltpu.SMEM),
            pl.BlockSpec(memory_space=pltpu.SMEM),
            pl.BlockSpec((tm, ATTN_DIM), lambda b, n: (b * nt + n, 0)),
            pl.BlockSpec((BLOCK, 2 * KV_DIM), lambda b, n: (b * nb, 0)),
            pl.BlockSpec((BLOCK, 2 * KV_DIM),
                         lambda b, n: (b * nb + jnp.maximum(n * nbk - 1, 0), 0)),
            pl.BlockSpec((tm, 2 * KV_DIM), lambda b, n: (b * nt + n, 0)),
            pl.BlockSpec((1, 128), lambda b, n: (0, 0)),
            pl.BlockSpec((1, 128), lambda b, n: (0, 0)),
        ],
        out_specs=pl.BlockSpec((tm, ATTN_DIM), lambda b, n: (b * nt + n, 0)),
        out_shape=jax.ShapeDtypeStruct(q.shape, F32),
        scratch_shapes=[pltpu.VMEM((3, HEADS, BLOCK, 3 * BLOCK), F32)],
        compiler_params=pltpu.CompilerParams(
            dimension_semantics=("arbitrary", "arbitrary"), vmem_limit_bytes=VMEM_LIMIT),
    )(rel_bias, sinks, q, kv, kv, kv, q_norm_w2, k_norm_w2)


def _ffn_kernel(lp, h_ref, ya_ref, yb_ref, wo_ref, nw_ref, wup_ref, cw_ref, cb_ref, wdn_ref,
                o_ref, carry_ref, hbuf_ref):
    tm = h_ref.shape[0]
    grow = lax.broadcasted_iota(jnp.int32, (tm, 1), 0) + pl.program_id(0) * tm
    pos = grow % lp

    @pl.when(pl.program_id(0) * tm % lp == 0)
    def _():
        carry_ref[...] = jnp.zeros_like(carry_ref)

    h = (h_ref[...] + _dot(ya_ref[...].astype(BF16), wo_ref[0:RWKV_DIM, :])
         + _dot(yb_ref[...].astype(BF16), wo_ref[RWKV_DIM:, :]))
    h = jnp.where(pos >= PAD, h, 0.0)
    ms = jnp.mean(h * h, axis=-1, keepdims=True)
    u = (h * lax.rsqrt(ms + NORM_EPS) * nw_ref[...]).astype(BF16)
    acc = h
    cw = 256
    for j in range(D_FF // cw):
        halves = []
        for base in (0, D_FF):
            lo = base + cw * j
            hid = _dot(u, wup_ref[:, lo:lo + cw])
            hbuf_ref[0:8, :] = carry_ref[:, lo:lo + cw]
            hbuf_ref[8:8 + tm, :] = hid
            carry_ref[:, lo:lo + cw] = hid[tm - 8:tm, :]
            halves.append(hid * cw_ref[0:1, lo:lo + cw]
                          + hbuf_ref[7:7 + tm, :] * cw_ref[1:2, lo:lo + cw]
                          + hbuf_ref[6:6 + tm, :] * cw_ref[2:3, lo:lo + cw]
                          + cb_ref[:, lo:lo + cw])
        gate, val = halves
        act = (gate * jax.nn.sigmoid(gate) * val).astype(BF16)
        acc = acc + _dot(act, wdn_ref[cw * j:cw * (j + 1), :])
    o_ref[...] = acc


def _ffn(h, ya, yb, w_out, norm_w, w_up, conv_w, conv_b, w_down, lp, tm):
    tp = h.shape[0]
    row = lambda i: (i, 0)
    const = lambda i: (0, 0)
    return pl.pallas_call(
        functools.partial(_ffn_kernel, lp),
        grid=(tp // tm,),
        in_specs=[
            pl.BlockSpec((tm, D_MODEL), row),
            pl.BlockSpec((tm, RWKV_DIM), row),
            pl.BlockSpec((tm, ATTN_DIM), row),
            pl.BlockSpec((D_MODEL, D_MODEL), const),
            pl.BlockSpec((1, D_MODEL), const),
            pl.BlockSpec((D_MODEL, 2 * D_FF), const),
            pl.BlockSpec((3, 2 * D_FF), const),
            pl.BlockSpec((1, 2 * D_FF), const),
            pl.BlockSpec((D_FF, D_MODEL), const),
        ],
        out_specs=pl.BlockSpec((tm, D_MODEL), row),
        out_shape=jax.ShapeDtypeStruct(h.shape, F32),
        scratch_shapes=[pltpu.VMEM((8, 2 * D_FF), F32), pltpu.VMEM((tm + 8, 256), F32)],
        compiler_params=pltpu.CompilerParams(
            dimension_semantics=("arbitrary",), vmem_limit_bytes=VMEM_LIMIT),
    )(h, ya, yb, w_out, norm_w, w_up, conv_w, conv_b, w_down)


def _row_tile(lp, cap):
    nb = lp // BLOCK
    best = 1
    for f in range(1, nb + 1):
        if nb % f == 0 and f * BLOCK <= cap:
            best = f
    return best * BLOCK


def kernel(x, meta_tokens, rel_bias, norm1_w, w_in, shift_mu, decay_w0, decay_up, aaa_a0, aaa_up, gate_up, k_k, k_a, r_k, lnx_w, lnx_b, q_norm_w, k_norm_w, sinks, w_out, norm2_w, w_up, conv_w, conv_b, w_down):
    batch, seq, _ = x.shape
    lp = PAD + N_META + seq
    assert lp % BLOCK == 0
    tm = _row_tile(lp, 640)
    meta = jnp.broadcast_to(meta_tokens[None].astype(x.dtype), (batch, N_META, D_MODEL))
    h = jnp.concatenate([jnp.zeros((batch, PAD, D_MODEL), x.dtype), meta, x], axis=1)
    h = h.reshape(batch * lp, D_MODEL)
    row2 = lambda t: t.reshape(1, -1)
    zeros64 = jnp.zeros((64, RWKV_DIM), F32)
    for layer in range(norm1_w.shape[0]):
        pr, q, kv = _inproj(h, row2(norm1_w[layer]), w_in[layer].astype(BF16), tm)
        y_rwkv = _rwkv(
            pr.reshape(batch, lp, RWKV_IN), row2(shift_mu[layer]), row2(decay_w0[layer]),
            jnp.concatenate([decay_up[layer], zeros64], axis=0), row2(aaa_a0[layer]),
            jnp.concatenate([zeros64, aaa_up[layer]], axis=0).astype(BF16),
            gate_up[layer].astype(BF16), row2(k_k[layer]), row2(k_a[layer]),
            row2(lnx_w[layer]), row2(lnx_b[layer]), row2(r_k[layer]))
        y_attn = _attention(q, kv, rel_bias, sinks[layer].reshape(1, HEADS),
                            jnp.tile(q_norm_w[layer], 2).reshape(1, 128),
                            jnp.tile(k_norm_w[layer], 2).reshape(1, 128), batch, lp, tm)
        h = _ffn(h, y_rwkv.reshape(batch * lp, RWKV_DIM), y_attn, w_out[layer].astype(BF16),
                 row2(norm2_w[layer]), w_up[layer].astype(BF16), conv_w[layer],
                 row2(conv_b[layer]), w_down[layer].astype(BF16), lp, tm)
    return h.reshape(batch, lp, D_MODEL)[:, PAD + N_META:]
```

```python
import functools
import math

import jax
import jax.numpy as jnp
from jax import lax
from jax.experimental import pallas as pl
from jax.experimental.pallas import tpu as pltpu

F32 = jnp.float32
BF16 = jnp.bfloat16
HI = lax.Precision.HIGHEST

D_MODEL = 1024
N_META = 16
BLOCK = 128
PAD = BLOCK - N_META
HEADS = 8
HEAD_DIM = 64
RWKV_DIM = 512
RWKV_IN = 1792
ATTN_DIM = 512
KV_DIM = 128
IN_DIM = 2560
D_FF = 2816
CHUNK = 64
CHUNKS_PER_STEP = 2
NORM_EPS = 1e-6
GN_EPS = HEAD_DIM * 1e-5
MASK_VALUE = -1e30
REL_BUCKETS = 32
REL_MAX_EXACT = 16
REL_MAX_DIST = 128
VMEM_LIMIT = 56 * 1024 * 1024


def _dot(a, b, precision=None):
    return lax.dot_general(a, b, (((1,), (0,)), ((), ())), precision=precision,
                           preferred_element_type=F32)


def _dot_nt(a, b, precision=None):
    return lax.dot_general(a, b, (((1,), (1,)), ((), ())), precision=precision,
                           preferred_element_type=F32)


def _bf(x):
    return x.astype(BF16)


def _dot_split(x, w_bf16):
    hi = _bf(x)
    lo = _bf(x - hi.astype(F32))
    return _dot(hi, w_bf16) + _dot(lo, w_bf16)


def _head_ones(n, dtype=F32):
    r = lax.broadcasted_iota(jnp.int32, (n, n), 0) // HEAD_DIM
    c = lax.broadcasted_iota(jnp.int32, (n, n), 1) // HEAD_DIM
    return (r == c).astype(dtype)


def _inproj_kernel(h_ref, nw_ref, w_ref, pr_ref, q_ref, kv_ref):
    x = h_ref[...]
    ms = jnp.mean(x * x, axis=-1, keepdims=True)
    u = (x * lax.rsqrt(ms + NORM_EPS) * nw_ref[...]).astype(BF16)
    for j in range(IN_DIM // 256):
        c = _dot(u, w_ref[:, 256 * j:256 * (j + 1)])
        lo = 256 * j
        if lo < RWKV_IN:
            pr_ref[:, lo:lo + 256] = c
        elif lo < RWKV_IN + ATTN_DIM:
            q_ref[:, lo - RWKV_IN:lo - RWKV_IN + 256] = c
        else:
            kv_ref[...] = c


def _inproj(h, norm_w, w_in_bf16, tm):
    tp = h.shape[0]
    return pl.pallas_call(
        _inproj_kernel,
        grid=(tp // tm,),
        in_specs=[
            pl.BlockSpec((tm, D_MODEL), lambda i: (i, 0)),
            pl.BlockSpec((1, D_MODEL), lambda i: (0, 0)),
            pl.BlockSpec((D_MODEL, IN_DIM), lambda i: (0, 0)),
        ],
        out_specs=[
            pl.BlockSpec((tm, RWKV_IN), lambda i: (i, 0)),
            pl.BlockSpec((tm, ATTN_DIM), lambda i: (i, 0)),
            pl.BlockSpec((tm, 2 * KV_DIM), lambda i: (i, 0)),
        ],
        out_shape=[
            jax.ShapeDtypeStruct((tp, RWKV_IN), F32),
            jax.ShapeDtypeStruct((tp, ATTN_DIM), F32),
            jax.ShapeDtypeStruct((tp, 2 * KV_DIM), F32),
        ],
        compiler_params=pltpu.CompilerParams(
            dimension_semantics=("arbitrary",), vmem_limit_bytes=VMEM_LIMIT),
    )(h, norm_w, w_in_bf16)


def _unit_lower_inverse(mats, eye, blk_masks):
    d = [jnp.where(blk_masks[0], a, 0.0) for a in mats]
    db = [_bf(x) for x in d]
    d2 = [_bf(_dot(x, x)) for x in db]
    d4 = [_bf(_dot(x, x)) for x in d2]
    t = [eye + x for x in d]
    t = [x + _dot(_bf(x), y) for x, y in zip(t, d2)]
    t = [x + _dot(_bf(x), y) for x, y in zip(t, d4)]
    for m in blk_masks[1:]:
        tb = [_bf(x) for x in t]
        off = [_bf(jnp.where(m, a, 0.0)) for a in mats]
        inner = [_bf(_dot(o, x)) for o, x in zip(off, tb)]
        t = [x + _dot(xb, i) for x, xb, i in zip(t, tb, inner)]
    return t


def _rwkv_kernel(p_ref, halo_ref, mu_ref, w0_ref, decup_ref, a0_ref, aup_ref, gup_ref,
                 kk_ref, ka_ref, lnw_ref, lnb_ref, rk_ref, o_ref, s_ref):
    step = pl.program_id(0)
    nb, rows, _ = p_ref.shape
    c = CHUNK

    @pl.when(step == 0)
    def _():
        s_ref[...] = jnp.zeros_like(s_ref)

    n2 = 2 * c
    ri = lax.broadcasted_iota(jnp.int32, (n2, n2), 0)
    ci = lax.broadcasted_iota(jnp.int32, (n2, n2), 1)
    same_head = (ri // c) == (ci // c)
    strict = same_head & (ci < ri)
    incl = same_head & (ci <= ri)
    eye = ri == ci
    eye_f = eye.astype(F32)
    blk_masks = [(ri // 8) == (ci // 8)]
    size = 8
    while size < c:
        blk_masks.append(((ri // (2 * size)) == (ci // (2 * size)))
                         & ((ri // size) != (ci // size)) & (ci < ri))
        size *= 2
    lane_lo = lax.broadcasted_iota(jnp.int32, (c, 128), 1) < HEAD_DIM
    ones = _head_ones(128, BF16)
    tri = (lax.broadcasted_iota(jnp.int32, (c, c), 0) >=
           lax.broadcasted_iota(jnp.int32, (c, c), 1)).astype(F32)
    row = lax.broadcasted_iota(jnp.int32, (rows, 1), 0)

    def stack(z):
        return jnp.concatenate([jnp.where(lane_lo, z, 0.0), jnp.where(lane_lo, 0.0, z)], axis=0)

    chains = []
    for b in range(nb):
        p = p_ref[b]
        prev = pltpu.roll(p, shift=1, axis=0)
        prev = jnp.where(row == 0, halo_ref[b, 7:8, :], prev)
        prev = jnp.where((row == 0) & (step == 0), 0.0, prev)
        ps = p + (prev - p) * mu_ref[...]
        r = ps[:, 0:512]
        k = ps[:, 512:1024]
        v = ps[:, 1024:1536]
        wa = ps[:, 1536:1664]
        gd = ps[:, 1664:1792]
        nz = -(w0_ref[...] + _dot(jnp.tanh(wa), decup_ref[...], HI))
        softplus = jnp.maximum(nz, 0.0) + jnp.log1p(jnp.exp(-jnp.abs(nz)))
        lw = -jnp.exp(-softplus - 0.5)
        a = jax.nn.sigmoid(a0_ref[...] + _dot(_bf(wa), aup_ref[...]))
        g = _dot(_bf(jax.nn.sigmoid(gd)), gup_ref[...])
        kk = k * kk_ref[...]
        k2 = kk * kk
        ss = jnp.concatenate([_dot_split(k2[:, 128 * j:128 * (j + 1)], ones)
                              for j in range(HEADS // 2)], axis=1)
        kn = kk / jnp.maximum(jnp.sqrt(ss), 1e-12)
        bb = kn * a
        k = k * (1.0 + (a - 1.0) * ka_ref[...])
        rkr = r * k * rk_ref[...]
        for ch in range(rows // c):
            rs = slice(c * ch, c * (ch + 1))
            lwc = lw[rs]
            cum = _dot(tri, lwc, HI)
            cum_last = cum[c - 1:c, :]
            e_neg = jnp.exp(-cum)
            e_end = jnp.exp(cum_last - cum)
            rt = r[rs] * jnp.exp(cum)
            at = -kn[rs] * jnp.exp(cum - lwc)
            kt, bt = k[rs] * e_neg, bb[rs] * e_neg
            kh, bh = k[rs] * e_end, bb[rs] * e_end
            wc = jnp.exp(cum_last)
            for j in range(HEADS // 2):
                sl = slice(128 * j, 128 * (j + 1))
                xr_f = stack(rt[:, sl])
                chains.append(dict(
                    b=b, ch=ch, j=j, rs=rs, sl=sl, xr_f=xr_f, xr=_bf(xr_f),
                    xa=_bf(stack(at[:, sl])), vb=_bf(stack(v[rs, sl])),
                    bt=_bf(bt[:, sl]), kt=_bf(kt[:, sl]),
                    bht=_bf(stack(bh[:, sl]).T), kht=_bf(stack(kh[:, sl]).T),
                    wc=wc[:, sl], v=v[rs, sl], rkr=rkr[rs, sl], g=g[rs, sl]))

    for cd in chains:
        cd["pm"] = _dot_nt(jnp.concatenate([cd["xa"], cd["xr"]], axis=0),
                           jnp.concatenate([cd["bt"], cd["bt"], cd["kt"], cd["kt"]], axis=0))
    for cd in chains:
        pm = cd.pop("pm")
        cd["a_ab"] = jnp.where(strict, pm[0:n2, 0:n2], 0.0)
        cd["a_ak"] = _bf(jnp.where(strict, pm[0:n2, n2:2 * n2], 0.0))
        cd["p_rb"] = _bf(jnp.where(incl, pm[n2:2 * n2, 0:n2], 0.0))
        cd["p_rk"] = _bf(jnp.where(incl, pm[n2:2 * n2, n2:2 * n2], 0.0))
    tinv = _unit_lower_inverse([cd.pop("a_ab") for cd in chains], eye_f, blk_masks)
    for cd in chains:
        cd["w1"] = _bf(_dot(cd["a_ak"], cd["vb"]))
    for cd, t in zip(chains, tinv):
        cd["gu"] = _bf(_dot(_bf(t), jnp.concatenate([cd["xa"], cd["w1"]], axis=1)))
    for cd in chains:
        cd["nq"] = _dot(cd["p_rb"], cd["gu"])
        cd["bm"] = _dot(cd["bht"], cd["gu"])
    for cd in chains:
        cd["y0"] = cd["nq"][:, n2:2 * n2] + _dot(cd["p_rk"], cd["vb"])
        cd["n_add"] = cd["bm"][:, n2:2 * n2] + _dot(cd["kht"], cd["vb"])
        cd["mq_lhs"] = jnp.concatenate(
            [_bf(cd["bm"][:, 0:n2]), _bf(cd["xr_f"] + cd["nq"][:, 0:n2])], axis=0)
        cd["wc_col"] = jnp.sum(jnp.where(eye, jnp.broadcast_to(cd["wc"], (n2, n2)), 0.0),
                               axis=1, keepdims=True)

    states = {(b, j): s_ref[b, j] for b in range(nb) for j in range(HEADS // 2)}
    for ch in range(rows // c):
        for cd in chains:
            if cd["ch"] != ch:
                continue
            st = states[(cd["b"], cd["j"])]
            mq = _dot(cd["mq_lhs"], _bf(st))
            states[(cd["b"], cd["j"])] = cd["wc_col"] * st + mq[0:n2] + cd["n_add"]
            yo = mq[n2:2 * n2] + cd["y0"]
            cd["y"] = yo[0:c, :] + yo[c:n2, :]
    for (b, j), st in states.items():
        s_ref[b, j] = st

    for cd in chains:
        y, sl = cd["y"], cd["sl"]
        mean = _dot_split(y, ones) * (1.0 / HEAD_DIM)
        dy = y - mean
        var = _dot_split(dy * dy, ones) * (1.0 / HEAD_DIM)
        yn = dy * lax.rsqrt(var + GN_EPS) * lnw_ref[:, sl] + lnb_ref[:, sl]
        bonus = _dot_split(cd["rkr"], ones) * cd["v"]
        o_ref[cd["b"], cd["rs"], sl] = (yn + bonus) * cd["g"]


def _rwkv(pr3, mu, w0, decup_pad, a0, aup_pad, gup, k_k, k_a, lnx_w, lnx_b, r_k):
    batch, lp, _ = pr3.shape
    rows = CHUNK * CHUNKS_PER_STEP
    const = lambda i: (0, 0)
    vec = pl.BlockSpec((1, RWKV_DIM), const)
    lora = pl.BlockSpec((128, RWKV_DIM), const)
    return pl.pallas_call(
        _rwkv_kernel,
        grid=(lp // rows,),
        in_specs=[
            pl.BlockSpec((batch, rows, RWKV_IN), lambda i: (0, i, 0)),
            pl.BlockSpec((batch, 8, RWKV_IN), lambda i: (0, jnp.maximum(i * (rows // 8) - 1, 0), 0)),
            pl.BlockSpec((1, RWKV_IN), const),
            vec, lora, vec, lora, lora, vec, vec, vec, vec, vec,
        ],
        out_specs=pl.BlockSpec((batch, rows, RWKV_DIM), lambda i: (0, i, 0)),
        out_shape=jax.ShapeDtypeStruct((batch, lp, RWKV_DIM), F32),
        scratch_shapes=[pltpu.VMEM((batch, HEADS // 2, 128, 128), F32)],
        compiler_params=pltpu.CompilerParams(
            dimension_semantics=("arbitrary",), vmem_limit_bytes=VMEM_LIMIT),
    )(pr3, pr3, mu, w0, decup_pad, a0, aup_pad, gup, k_k, k_a, lnx_w, lnx_b, r_k)


def _t5_bucket(d):
    d = jnp.maximum(d, 0)
    df = jnp.maximum(d, REL_MAX_EXACT).astype(F32)
    large = REL_MAX_EXACT + (jnp.log(df / REL_MAX_EXACT) / math.log(REL_MAX_DIST / REL_MAX_EXACT)
                             * (REL_BUCKETS - REL_MAX_EXACT)).astype(jnp.int32)
    large = jnp.minimum(large, REL_BUCKETS - 1)
    return jnp.where(d < REL_MAX_EXACT, d, large)


def _attn_kernel(rb_ref, sink_ref, q_ref, kv0_ref, kvp_ref, kvc_ref, qw_ref, kw_ref, o_ref, tbl_ref):
    n = pl.program_id(1)

    @pl.when((pl.program_id(0) == 0) & (n == 0))
    def _():
        q = lax.broadcasted_iota(jnp.int32, (BLOCK, 3 * BLOCK), 0)
        col = lax.broadcasted_iota(jnp.int32, (BLOCK, 3 * BLOCK), 1)
        is_meta = col < BLOCK
        for nn in range(3):
            d_meta = nn * BLOCK + q - col
            d = jnp.where(is_meta, d_meta, q + 2 * BLOCK - col)
            ok = is_meta & (col >= PAD) & (d_meta >= 0)
            if nn >= 2:
                ok = ok | ((col >= BLOCK) & (col < 2 * BLOCK) & (col - BLOCK > q))
            if nn >= 1:
                ok = ok | ((col >= 2 * BLOCK) & (col - 2 * BLOCK <= q))
            bucket = _t5_bucket(d)

            def per_head(h, carry):
                acc = jnp.zeros((BLOCK, 3 * BLOCK), F32)
                for bk in range(REL_BUCKETS):
                    acc = jnp.where(bucket == bk, rb_ref[bk, h], acc)
                tbl_ref[nn, h] = jnp.where(ok, acc, MASK_VALUE)
                return carry

            lax.fori_loop(0, HEADS, per_head, 0)

    ones = _head_ones(128, BF16)
    lane_lo = lax.broadcasted_iota(jnp.int32, (BLOCK, 128), 1) < HEAD_DIM
    lane_hi = jnp.logical_not(lane_lo)
    rr = lax.broadcasted_iota(jnp.int32, (128, 128), 0)
    cc = lax.broadcasted_iota(jnp.int32, (128, 128), 1)
    dup = [((rr // HEAD_DIM == g) & (rr % HEAD_DIM == cc % HEAD_DIM)).astype(BF16) for g in range(2)]

    def qk_norm(x, w):
        ms = _dot(_bf(x * x), ones) * (1.0 / HEAD_DIM)
        return x * lax.rsqrt(ms + NORM_EPS) * w

    nbk = q_ref.shape[0] // BLOCK
    grp = HEADS // 2
    qw = qw_ref[...] * HEAD_DIM ** -0.5

    kvb = [kv0_ref[...], kvp_ref[...]] + [kvc_ref[BLOCK * jb:BLOCK * (jb + 1), :] for jb in range(nbk)]
    kn = [_bf(qk_norm(x[:, 0:128], kw_ref[...])) for x in kvb]
    kd = [[_bf(_dot(x, dup[g])) for x in kn] for g in range(2)]
    vd = [[_bf(_dot(_bf(x[:, 128:256]), dup[g])) for x in kvb] for g in range(2)]
    qn = [[qk_norm(q_ref[BLOCK * jb:BLOCK * (jb + 1), 128 * c4:128 * (c4 + 1)], qw)
           for c4 in range(4)] for jb in range(nbk)]
    sinks = [jnp.concatenate([jnp.full((BLOCK, 128), sink_ref[0, grp * g + i], F32)
                              for i in range(grp)], axis=0) for g in range(2)]
    ones_cols = jnp.ones((3 * BLOCK, 128), BF16)

    chains = [(jb, g) for jb in range(nbk) for g in range(2)]
    s = []
    for jb, g in chains:
        lhs = jnp.concatenate(
            [jnp.where(lane_lo if e == 0 else lane_hi, qn[jb][2 * g + jj], 0.0)
             for jj in range(2) for e in range(2)], axis=0).astype(BF16)
        keys = jnp.concatenate([kd[g][0], kd[g][1 + jb], kd[g][2 + jb]], axis=0)
        tsel = jnp.minimum(n * nbk + jb, 2)
        tb = tbl_ref[tsel, pl.ds(grp * g, grp)].reshape(grp * BLOCK, 3 * BLOCK)
        s.append(_dot_nt(lhs, keys) + tb)
    mb = []
    for x, (_, g) in zip(s, chains):
        m3 = jnp.maximum(jnp.maximum(x[:, 0:128], x[:, 128:256]), x[:, 256:384])
        m = jnp.maximum(jnp.max(m3, axis=-1, keepdims=True), sinks[g][:, 0:1])
        mb.append(jnp.broadcast_to(m, (grp * BLOCK, 128)))
    ex = [jnp.concatenate([_bf(jnp.exp(x[:, 128 * i:128 * (i + 1)] - mm)) for i in range(3)], axis=1)
          for x, mm in zip(s, mb)]
    out = []
    for x, mm, (jb, g) in zip(ex, mb, chains):
        vals = jnp.concatenate([vd[g][0], vd[g][1 + jb], vd[g][2 + jb]], axis=0)
        od = _dot(x, jnp.concatenate([vals, ones_cols], axis=1))
        den = od[:, 128:256] + jnp.exp(sinks[g] - mm)
        out.append(od[:, 0:128] * (1.0 / den))
    for o, (jb, g) in zip(out, chains):
        for jj in range(2):
            col = 2 * g + jj
            o_ref[BLOCK * jb:BLOCK * (jb + 1), 128 * col:128 * (col + 1)] = jnp.where(
                lane_lo, o[256 * jj:256 * jj + 128], o[256 * jj + 128:256 * jj + 256])


def _attention(q, kv, rel_bias, sinks, q_norm_w2, k_norm_w2, batch, lp, tm):
    nb = lp // BLOCK
    nt = lp // tm
    nbk = tm // BLOCK
    return pl.pallas_call(
        _attn_kernel,
        grid=(batch, nt),
        in_specs=[
            pl.BlockSpec(memory_space=pltpu.SMEM),
            pl.BlockSpec(memory_space=pltpu.SMEM),
            pl.BlockSpec((tm, ATTN_DIM), lambda b, n: (b * nt + n, 0)),
            pl.BlockSpec((BLOCK, 2 * KV_DIM), lambda b, n: (b * nb, 0)),
            pl.BlockSpec((BLOCK, 2 * KV_DIM),
                         lambda b, n: (b * nb + jnp.maximum(n * nbk - 1, 0), 0)),
            pl.BlockSpec((tm, 2 * KV_DIM), lambda b, n: (b * nt + n, 0)),
            pl.BlockSpec((1, 128), lambda b, n: (0, 0)),
            pl.BlockSpec((1, 128), lambda b, n: (0, 0)),
        ],
        out_specs=pl.BlockSpec((tm, ATTN_DIM), lambda b, n: (b * nt + n, 0)),
        out_shape=jax.ShapeDtypeStruct(q.shape, F32),
        scratch_shapes=[pltpu.VMEM((3, HEADS, BLOCK, 3 * BLOCK), F32)],
        compiler_params=pltpu.CompilerParams(
            dimension_semantics=("arbitrary", "arbitrary"), vmem_limit_bytes=VMEM_LIMIT),
    )(rel_bias, sinks, q, kv, kv, kv, q_norm_w2, k_norm_w2)


def _ffn_kernel(lp, h_ref, ya_ref, yb_ref, wo_ref, nw_ref, wup_ref, cw_ref, cb_ref, wdn_ref,
                o_ref, carry_ref, hbuf_ref):
    tm = h_ref.shape[0]
    grow = lax.broadcasted_iota(jnp.int32, (tm, 1), 0) + pl.program_id(0) * tm
    pos = grow % lp

    @pl.when(pl.program_id(0) * tm % lp == 0)
    def _():
        carry_ref[...] = jnp.zeros_like(carry_ref)

    h = (h_ref[...] + _dot(ya_ref[...].astype(BF16), wo_ref[0:RWKV_DIM, :])
         + _dot(yb_ref[...].astype(BF16), wo_ref[RWKV_DIM:, :]))
    h = jnp.where(pos >= PAD, h, 0.0)
    ms = jnp.mean(h * h, axis=-1, keepdims=True)
    u = (h * lax.rsqrt(ms + NORM_EPS) * nw_ref[...]).astype(BF16)
    acc = h
    cw = 256
    nj = D_FF // cw
    nslot = hbuf_ref.shape[0]

    def up(j):
        return [_dot(u, wup_ref[:, base + cw * j:base + cw * (j + 1)]) for base in (0, D_FF)]

    hids = up(0)
    for j in range(nj):
        nxt = up(j + 1) if j + 1 < nj else None
        halves = []
        for half, (base, hid) in enumerate(zip((0, D_FF), hids)):
            lo = base + cw * j
            buf = hbuf_ref.at[(2 * j + half) % nslot]
            buf[0:8, :] = carry_ref[:, lo:lo + cw]
            buf[8:8 + tm, :] = hid
            carry_ref[:, lo:lo + cw] = hid[tm - 8:tm, :]
            halves.append(hid * cw_ref[0:1, lo:lo + cw]
                          + buf[7:7 + tm, :] * cw_ref[1:2, lo:lo + cw]
                          + buf[6:6 + tm, :] * cw_ref[2:3, lo:lo + cw]
                          + cb_ref[:, lo:lo + cw])
        gate, val = halves
        act = (gate * jax.nn.sigmoid(gate) * val).astype(BF16)
        acc = acc + _dot(act, wdn_ref[cw * j:cw * (j + 1), :])
        hids = nxt
    o_ref[...] = acc


def _ffn(h, ya, yb, w_out, norm_w, w_up, conv_w, conv_b, w_down, lp, tm):
    tp = h.shape[0]
    row = lambda i: (i, 0)
    const = lambda i: (0, 0)
    return pl.pallas_call(
        functools.partial(_ffn_kernel, lp),
        grid=(tp // tm,),
        in_specs=[
            pl.BlockSpec((tm, D_MODEL), row),
            pl.BlockSpec((tm, RWKV_DIM), row),
            pl.BlockSpec((tm, ATTN_DIM), row),
            pl.BlockSpec((D_MODEL, D_MODEL), const),
            pl.BlockSpec((1, D_MODEL), const),
            pl.BlockSpec((D_MODEL, 2 * D_FF), const),
            pl.BlockSpec((3, 2 * D_FF), const),
            pl.BlockSpec((1, 2 * D_FF), const),
            pl.BlockSpec((D_FF, D_MODEL), const),
        ],
        out_specs=pl.BlockSpec((tm, D_MODEL), row),
        out_shape=jax.ShapeDtypeStruct(h.shape, F32),
        scratch_shapes=[pltpu.VMEM((8, 2 * D_FF), F32), pltpu.VMEM((4, tm + 8, 256), F32)],
        compiler_params=pltpu.CompilerParams(
            dimension_semantics=("arbitrary",), vmem_limit_bytes=VMEM_LIMIT),
    )(h, ya, yb, w_out, norm_w, w_up, conv_w, conv_b, w_down)


def _row_tile(lp, cap, unit=BLOCK):
    n = lp // unit
    best = 1
    for f in range(1, n + 1):
        if n % f == 0 and f * unit <= cap:
            best = f
    return best * unit


MXU_OVERLAP_ROWS = 336


def kernel(x, meta_tokens, rel_bias, norm1_w, w_in, shift_mu, decay_w0, decay_up, aaa_a0, aaa_up, gate_up, k_k, k_a, r_k, lnx_w, lnx_b, q_norm_w, k_norm_w, sinks, w_out, norm2_w, w_up, conv_w, conv_b, w_down):
    batch, seq, _ = x.shape
    lp = PAD + N_META + seq
    assert lp % BLOCK == 0
    tm = _row_tile(lp, 640)
    tm_mm = _row_tile(lp, MXU_OVERLAP_ROWS, 16)
    meta = jnp.broadcast_to(meta_tokens[None].astype(x.dtype), (batch, N_META, D_MODEL))
    h = jnp.concatenate([jnp.zeros((batch, PAD, D_MODEL), x.dtype), meta, x], axis=1)
    h = h.reshape(batch * lp, D_MODEL)
    row2 = lambda t: t.reshape(1, -1)
    zeros64 = jnp.zeros((64, RWKV_DIM), F32)
    for layer in range(norm1_w.shape[0]):
        pr, q, kv = _inproj(h, row2(norm1_w[layer]), w_in[layer].astype(BF16), tm)
        y_rwkv = _rwkv(
            pr.reshape(batch, lp, RWKV_IN), row2(shift_mu[layer]), row2(decay_w0[layer]),
            jnp.concatenate([decay_up[layer], zeros64], axis=0), row2(aaa_a0[layer]),
            jnp.concatenate([zeros64, aaa_up[layer]], axis=0).astype(BF16),
            gate_up[layer].astype(BF16), row2(k_k[layer]), row2(k_a[layer]),
            row2(lnx_w[layer]), row2(lnx_b[layer]), row2(r_k[layer]))
        y_attn = _attention(q, kv, rel_bias, sinks[layer].reshape(1, HEADS),
                            jnp.tile(q_norm_w[layer], 2).reshape(1, 128),
                            jnp.tile(k_norm_w[layer], 2).reshape(1, 128), batch, lp, tm)
        h = _ffn(h, y_rwkv.reshape(batch * lp, RWKV_DIM), y_attn, w_out[layer].astype(BF16),
                 row2(norm2_w[layer]), w_up[layer].astype(BF16), conv_w[layer],
                 row2(conv_b[layer]), w_down[layer].astype(BF16), lp, tm_mm)
    return h.reshape(batch, lp, D_MODEL)[:, PAD + N_META:]
```

```python
import functools
import math

import jax
import jax.numpy as jnp
from jax import lax
from jax.experimental import pallas as pl
from jax.experimental.pallas import tpu as pltpu

F32 = jnp.float32
BF16 = jnp.bfloat16
HI = lax.Precision.HIGHEST

D_MODEL = 1024
N_META = 16
BLOCK = 128
PAD = BLOCK - N_META
HEADS = 8
HEAD_DIM = 64
RWKV_DIM = 512
RWKV_IN = 1792
ATTN_DIM = 512
KV_DIM = 128
IN_DIM = 2560
D_FF = 2816
CHUNK = 64
CHUNKS_PER_STEP = 2
NORM_EPS = 1e-6
GN_EPS = HEAD_DIM * 1e-5
MASK_VALUE = -1e30
REL_BUCKETS = 32
REL_MAX_EXACT = 16
REL_MAX_DIST = 128
VMEM_LIMIT = 56 * 1024 * 1024


def _dot(a, b, precision=None):
    return lax.dot_general(a, b, (((1,), (0,)), ((), ())), precision=precision,
                           preferred_element_type=F32)


def _dot_nt(a, b, precision=None):
    return lax.dot_general(a, b, (((1,), (1,)), ((), ())), precision=precision,
                           preferred_element_type=F32)


def _bf(x):
    return x.astype(BF16)


def _dot_split(x, w_bf16):
    hi = _bf(x)
    lo = _bf(x - hi.astype(F32))
    return _dot(hi, w_bf16) + _dot(lo, w_bf16)


def _head_ones(n, dtype=F32):
    r = lax.broadcasted_iota(jnp.int32, (n, n), 0) // HEAD_DIM
    c = lax.broadcasted_iota(jnp.int32, (n, n), 1) // HEAD_DIM
    return (r == c).astype(dtype)


def _inproj_kernel(h_ref, nw_ref, w_ref, pr_ref, q_ref, kv_ref):
    x = h_ref[...]
    ms = jnp.mean(x * x, axis=-1, keepdims=True)
    u = (x * lax.rsqrt(ms + NORM_EPS) * nw_ref[...]).astype(BF16)
    for j in range(IN_DIM // 256):
        c = _dot(u, w_ref[:, 256 * j:256 * (j + 1)])
        lo = 256 * j
        if lo < RWKV_IN:
            pr_ref[:, lo:lo + 256] = c
        elif lo < RWKV_IN + ATTN_DIM:
            q_ref[:, lo - RWKV_IN:lo - RWKV_IN + 256] = c
        else:
            kv_ref[...] = c


def _inproj(h, norm_w, w_in_bf16, tm):
    tp = h.shape[0]
    return pl.pallas_call(
        _inproj_kernel,
        grid=(tp // tm,),
        in_specs=[
            pl.BlockSpec((tm, D_MODEL), lambda i: (i, 0)),
            pl.BlockSpec((1, D_MODEL), lambda i: (0, 0)),
            pl.BlockSpec((D_MODEL, IN_DIM), lambda i: (0, 0)),
        ],
        out_specs=[
            pl.BlockSpec((tm, RWKV_IN), lambda i: (i, 0)),
            pl.BlockSpec((tm, ATTN_DIM), lambda i: (i, 0)),
            pl.BlockSpec((tm, 2 * KV_DIM), lambda i: (i, 0)),
        ],
        out_shape=[
            jax.ShapeDtypeStruct((tp, RWKV_IN), F32),
            jax.ShapeDtypeStruct((tp, ATTN_DIM), F32),
            jax.ShapeDtypeStruct((tp, 2 * KV_DIM), F32),
        ],
        compiler_params=pltpu.CompilerParams(
            dimension_semantics=("arbitrary",), vmem_limit_bytes=VMEM_LIMIT),
    )(h, norm_w, w_in_bf16)


def _blockdiag(z, lane_lo):
    zero = jnp.zeros_like(z)
    return jnp.concatenate([jnp.where(lane_lo, z, zero), jnp.where(lane_lo, zero, z)], axis=0)


def _unit_lower_inverse(mats, eye, blk_masks, lane_lo):
    bd = lambda z: _blockdiag(z, lane_lo)
    d = [jnp.where(blk_masks[0], a, 0.0) for a in mats]
    db = [_bf(x) for x in d]
    bdd = [bd(x) for x in db]
    d2 = [_dot(x, y) for x, y in zip(db, bdd)]
    d2b = [_bf(x) for x in d2]
    d34 = [_dot(x, jnp.concatenate([y, bd(x)], axis=1)) for x, y in zip(d2b, bdd)]
    s3 = [eye + x + y + z[:, 0:128] for x, y, z in zip(d, d2, d34)]
    t = [x + _dot(_bf(z[:, 128:256]), bd(_bf(x))) for x, z in zip(s3, d34)]
    for m in blk_masks[1:]:
        tb = [_bf(x) for x in t]
        inner = [_bf(_dot(_bf(jnp.where(m, a, 0.0)), bd(x))) for a, x in zip(mats, tb)]
        t = [x + _dot(xb, bd(i)) for x, xb, i in zip(t, tb, inner)]
    return t


def _rwkv_kernel(p_ref, halo_ref, mu_ref, w0_ref, decup_ref, a0_ref, aup_ref, gup_ref,
                 kk_ref, ka_ref, lnw_ref, lnb_ref, rk_ref, o_ref, s_ref):
    step = pl.program_id(0)
    nb, rows, _ = p_ref.shape
    c = CHUNK

    @pl.when(step == 0)
    def _():
        s_ref[...] = jnp.zeros_like(s_ref)

    ti = lax.broadcasted_iota(jnp.int32, (c, 128), 0)
    lane = lax.broadcasted_iota(jnp.int32, (c, 128), 1)
    si = lane % HEAD_DIM
    lane_lo = lane < HEAD_DIM
    strict = si < ti
    incl = si <= ti
    eye = si == ti
    eye_f = eye.astype(F32)
    blk_masks = [(ti // 8) == (si // 8)]
    size = 8
    while size < c:
        blk_masks.append(((ti // (2 * size)) == (si // (2 * size))) & ((ti // size) != (si // size)))
        size *= 2
    ones = _head_ones(128, BF16)
    row = lax.broadcasted_iota(jnp.int32, (rows, 1), 0)
    bd = lambda z: _blockdiag(z, lane_lo)

    def pair_t(z):
        zt = bd(z).T
        return zt[0:c, :] + zt[c:2 * c, :]

    npair = HEADS // 2

    def head_sums(x, split=False):
        n = x.shape[0]
        xs = jnp.concatenate([x[:, 128 * j:128 * (j + 1)] for j in range(npair)], axis=0)
        ys = _dot_split(xs, ones) if split else _dot(_bf(xs), ones)
        return jnp.concatenate([ys[n * j:n * (j + 1)] for j in range(npair)], axis=1)

    prevs = []
    for b in range(nb):
        prev = pltpu.roll(p_ref[b], shift=1, axis=0)
        prev = jnp.where(row == 0, halo_ref[b, 7:8, :], prev)
        prevs.append(jnp.where((row == 0) & (step == 0), 0.0, prev))
    p = jnp.concatenate([p_ref[b] for b in range(nb)], axis=0)
    ps = p + (jnp.concatenate(prevs, axis=0) - p) * mu_ref[...]
    r = ps[:, 0:512]
    k = ps[:, 512:1024]
    v = ps[:, 1024:1536]
    wa = ps[:, 1536:1664]
    gd = ps[:, 1664:1792]
    nz = -(w0_ref[...] + _dot(jnp.tanh(wa), decup_ref[...], HI))
    softplus = jnp.maximum(nz, 0.0) + jnp.log1p(jnp.exp(-jnp.abs(nz)))
    lw = -jnp.exp(-softplus - 0.5)
    a = jax.nn.sigmoid(a0_ref[...] + _dot(_bf(wa), aup_ref[...]))
    g = _dot(_bf(jax.nn.sigmoid(gd)), gup_ref[...])
    kk = k * kk_ref[...]
    kn = kk / jnp.maximum(jnp.sqrt(head_sums(kk * kk)), 1e-12)
    bb = kn * a
    k = k * (1.0 + (a - 1.0) * ka_ref[...])
    bonus = head_sums(r * k * rk_ref[...]) * v

    rowc = lax.broadcasted_iota(jnp.int32, (c, 1), 0)
    chains = []
    for b in range(nb):
        for ch in range(rows // c):
            rs = slice(c * ch, c * (ch + 1))
            ra = slice(rows * b + c * ch, rows * b + c * (ch + 1))
            lwc = lw[ra]
            cum = lwc
            for sh in (1, 2, 4, 8, 16, 32):
                cum = cum + jnp.where(rowc >= sh, pltpu.roll(cum, shift=sh, axis=0), 0.0)
            cum_last = cum[c - 1:c, :]
            e_neg = jnp.exp(-cum)
            e_end = jnp.exp(cum_last - cum)
            rt = r[ra] * jnp.exp(cum)
            at = -kn[ra] * jnp.exp(cum - lwc)
            kt, bt = k[ra] * e_neg, bb[ra] * e_neg
            kh, bh = k[ra] * e_end, bb[ra] * e_end
            wc = jnp.exp(cum_last)
            for j in range(npair):
                sl = slice(128 * j, 128 * (j + 1))
                chains.append(dict(
                    b=b, ch=ch, j=j, rs=rs, sl=sl, rt=rt[:, sl], at=_bf(at[:, sl]),
                    rhs=jnp.concatenate([bd(_bf(bt[:, sl])), bd(_bf(kt[:, sl]))], axis=0),
                    vbd=bd(_bf(v[ra, sl])),
                    bht=_bf(pair_t(bh[:, sl])), kht=_bf(pair_t(kh[:, sl])),
                    wc=wc[:, sl], bonus=bonus[ra, sl], g=g[ra, sl]))

    for cd in chains:
        cd["pm"] = _dot_nt(jnp.concatenate([cd["at"], _bf(cd["rt"])], axis=0), cd.pop("rhs"))
    for cd in chains:
        pm = cd.pop("pm")
        cd["a_ab"] = jnp.where(strict, pm[0:c, 0:128], 0.0)
        cd["a_ak"] = _bf(jnp.where(strict, pm[0:c, 128:256], 0.0))
        cd["p_rb"] = _bf(jnp.where(incl, pm[c:2 * c, 0:128], 0.0))
        cd["p_rk"] = _bf(jnp.where(incl, pm[c:2 * c, 128:256], 0.0))
    tinv = _unit_lower_inverse([cd.pop("a_ab") for cd in chains], eye_f, blk_masks, lane_lo)
    for cd in chains:
        cd["xv"] = _dot(jnp.concatenate([cd["a_ak"], cd["p_rk"], cd["kht"]], axis=0), cd["vbd"])
    for cd, t in zip(chains, tinv):
        gu = _bf(_dot(_bf(t), jnp.concatenate([bd(cd["at"]), bd(_bf(cd["xv"][0:c]))], axis=1)))
        cd["gu"] = jnp.concatenate([bd(gu[:, 0:128]), bd(gu[:, 128:256])], axis=1)
    for cd in chains:
        r = _dot(jnp.concatenate([cd["p_rb"], cd["bht"]], axis=0), cd.pop("gu"))
        xv = cd.pop("xv")
        cd["y0"] = r[0:c, 128:256] + xv[c:2 * c]
        cd["n_add"] = r[c:2 * c, 128:256] + xv[2 * c:3 * c]
        cd["mq_lhs"] = _bf(jnp.concatenate([r[c:2 * c, 0:128], cd["rt"] + r[0:c, 0:128]], axis=0))
    wcc = _dot_split(jnp.concatenate(
        [jnp.where(eye, jnp.broadcast_to(cd["wc"], (c, 128)), 0.0) for cd in chains], axis=0), ones)
    for i, cd in enumerate(chains):
        cd["wc_col"] = wcc[c * i:c * (i + 1)]

    states = {(b, j): s_ref[b, j] for b in range(nb) for j in range(HEADS // 2)}
    for ch in range(rows // c):
        for cd in chains:
            if cd["ch"] != ch:
                continue
            st = states[(cd["b"], cd["j"])]
            mq = _dot(cd["mq_lhs"], bd(_bf(st)))
            states[(cd["b"], cd["j"])] = cd["wc_col"] * st + mq[0:c] + cd["n_add"]
            cd["y"] = mq[c:2 * c] + cd["y0"]
    for (b, j), st in states.items():
        s_ref[b, j] = st

    ys = jnp.concatenate([cd["y"] for cd in chains], axis=0)
    dy = ys - _dot_split(ys, ones) * (1.0 / HEAD_DIM)
    var = _dot(_bf(dy * dy), ones) * (1.0 / HEAD_DIM)
    zn = dy * lax.rsqrt(var + GN_EPS)
    for i, cd in enumerate(chains):
        sl = cd["sl"]
        yn = zn[c * i:c * (i + 1)] * lnw_ref[:, sl] + lnb_ref[:, sl]
        o_ref[cd["b"], cd["rs"], sl] = (yn + cd["bonus"]) * cd["g"]


def _rwkv(pr3, mu, w0, decup_pad, a0, aup_pad, gup, k_k, k_a, lnx_w, lnx_b, r_k):
    batch, lp, _ = pr3.shape
    rows = CHUNK * CHUNKS_PER_STEP
    const = lambda i: (0, 0)
    vec = pl.BlockSpec((1, RWKV_DIM), const)
    lora = pl.BlockSpec((128, RWKV_DIM), const)
    return pl.pallas_call(
        _rwkv_kernel,
        grid=(lp // rows,),
        in_specs=[
            pl.BlockSpec((batch, rows, RWKV_IN), lambda i: (0, i, 0)),
            pl.BlockSpec((batch, 8, RWKV_IN), lambda i: (0, jnp.maximum(i * (rows // 8) - 1, 0), 0)),
            pl.BlockSpec((1, RWKV_IN), const),
            vec, lora, vec, lora, lora, vec, vec, vec, vec, vec,
        ],
        out_specs=pl.BlockSpec((batch, rows, RWKV_DIM), lambda i: (0, i, 0)),
        out_shape=jax.ShapeDtypeStruct((batch, lp, RWKV_DIM), F32),
        scratch_shapes=[pltpu.VMEM((batch, HEADS // 2, CHUNK, 128), F32)],
        compiler_params=pltpu.CompilerParams(
            dimension_semantics=("arbitrary",), vmem_limit_bytes=VMEM_LIMIT),
    )(pr3, pr3, mu, w0, decup_pad, a0, aup_pad, gup, k_k, k_a, lnx_w, lnx_b, r_k)


def _t5_thresholds():
    n_log = REL_BUCKETS - REL_MAX_EXACT
    out = []
    for k in range(1, n_log):
        x = REL_MAX_EXACT * (REL_MAX_DIST / REL_MAX_EXACT) ** (k / n_log)
        assert min(x - math.floor(x), math.ceil(x) - x) > 1e-3
        out.append(math.ceil(x))
    return out


def _t5_bucket(d):
    d = jnp.maximum(d, 0)
    large = jnp.full(d.shape, REL_MAX_EXACT, jnp.int32)
    for t in _t5_thresholds():
        large = large + (d >= t).astype(jnp.int32)
    return jnp.where(d < REL_MAX_EXACT, d, large)


def _attn_kernel(rb_ref, sink_ref, q_ref, kv0_ref, kvp_ref, kvc_ref, qw_ref, kw_ref, o_ref, tbl_ref):
    n = pl.program_id(1)

    @pl.when((pl.program_id(0) == 0) & (n == 0))
    def _():
        q = lax.broadcasted_iota(jnp.int32, (BLOCK, 3 * BLOCK), 0)
        col = lax.broadcasted_iota(jnp.int32, (BLOCK, 3 * BLOCK), 1)
        is_meta = col < BLOCK
        for nn in range(3):
            d_meta = nn * BLOCK + q - col
            d = jnp.where(is_meta, d_meta, q + 2 * BLOCK - col)
            ok = is_meta & (col >= PAD) & (d_meta >= 0)
            if nn >= 2:
                ok = ok | ((col >= BLOCK) & (col < 2 * BLOCK) & (col - BLOCK > q))
            if nn >= 1:
                ok = ok | ((col >= 2 * BLOCK) & (col - 2 * BLOCK <= q))
            bucket = _t5_bucket(d)

            def per_head(h, carry):
                acc = jnp.zeros((BLOCK, 3 * BLOCK), F32)
                for bk in range(REL_BUCKETS):
                    acc = jnp.where(bucket == bk, rb_ref[bk, h], acc)
                tbl_ref[nn, h] = jnp.where(ok, acc, MASK_VALUE)
                return carry

            lax.fori_loop(0, HEADS, per_head, 0)

    ones = _head_ones(128, BF16)
    lane_lo = lax.broadcasted_iota(jnp.int32, (BLOCK, 128), 1) < HEAD_DIM
    lane_hi = jnp.logical_not(lane_lo)
    rr = lax.broadcasted_iota(jnp.int32, (128, 128), 0)
    cc = lax.broadcasted_iota(jnp.int32, (128, 128), 1)
    dup = [((rr // HEAD_DIM == g) & (rr % HEAD_DIM == cc % HEAD_DIM)).astype(BF16) for g in range(2)]

    def qk_norm(x, w):
        ms = _dot(_bf(x * x), ones) * (1.0 / HEAD_DIM)
        return x * lax.rsqrt(ms + NORM_EPS) * w

    nbk = q_ref.shape[0] // BLOCK
    grp = HEADS // 2
    qw = qw_ref[...] * HEAD_DIM ** -0.5

    kvb = [kv0_ref[...], kvp_ref[...]] + [kvc_ref[BLOCK * jb:BLOCK * (jb + 1), :] for jb in range(nbk)]
    kn = [_bf(qk_norm(x[:, 0:128], kw_ref[...])) for x in kvb]
    kd = [[_bf(_dot(x, dup[g])) for x in kn] for g in range(2)]
    vd = [[_bf(_dot(_bf(x[:, 128:256]), dup[g])) for x in kvb] for g in range(2)]
    qn = [[qk_norm(q_ref[BLOCK * jb:BLOCK * (jb + 1), 128 * c4:128 * (c4 + 1)], qw)
           for c4 in range(4)] for jb in range(nbk)]
    sinks = [jnp.concatenate([jnp.full((BLOCK, 128), sink_ref[0, grp * g + i], F32)
                              for i in range(grp)], axis=0) for g in range(2)]
    ones_cols = jnp.ones((3 * BLOCK, 128), BF16)

    chains = [(jb, g) for jb in range(nbk) for g in range(2)]
    s = []
    for jb, g in chains:
        lhs = jnp.concatenate(
            [jnp.where(lane_lo if e == 0 else lane_hi, qn[jb][2 * g + jj], 0.0)
             for jj in range(2) for e in range(2)], axis=0).astype(BF16)
        keys = jnp.concatenate([kd[g][0], kd[g][1 + jb], kd[g][2 + jb]], axis=0)
        tsel = jnp.minimum(n * nbk + jb, 2)
        tb = tbl_ref[tsel, pl.ds(grp * g, grp)].reshape(grp * BLOCK, 3 * BLOCK)
        s.append(_dot_nt(lhs, keys) + tb)
    mb = []
    for x, (_, g) in zip(s, chains):
        m3 = jnp.maximum(jnp.maximum(x[:, 0:128], x[:, 128:256]), x[:, 256:384])
        m = jnp.maximum(jnp.max(m3, axis=-1, keepdims=True), sinks[g][:, 0:1])
        mb.append(jnp.broadcast_to(m, (grp * BLOCK, 128)))
    ex = [jnp.concatenate([_bf(jnp.exp(x[:, 128 * i:128 * (i + 1)] - mm)) for i in range(3)], axis=1)
          for x, mm in zip(s, mb)]
    out = []
    for x, mm, (jb, g) in zip(ex, mb, chains):
        vals = jnp.concatenate([vd[g][0], vd[g][1 + jb], vd[g][2 + jb]], axis=0)
        od = _dot(x, jnp.concatenate([vals, ones_cols], axis=1))
        den = od[:, 128:256] + jnp.exp(sinks[g] - mm)
        out.append(od[:, 0:128] * (1.0 / den))
    for o, (jb, g) in zip(out, chains):
        for jj in range(2):
            col = 2 * g + jj
            o_ref[BLOCK * jb:BLOCK * (jb + 1), 128 * col:128 * (col + 1)] = jnp.where(
                lane_lo, o[256 * jj:256 * jj + 128], o[256 * jj + 128:256 * jj + 256])


def _attention(q, kv, rel_bias, sinks, q_norm_w2, k_norm_w2, batch, lp, tm):
    nb = lp // BLOCK
    nt = lp // tm
    nbk = tm // BLOCK
    return pl.pallas_call(
        _attn_kernel,
        grid=(batch, nt),
        in_specs=[
            pl.BlockSpec(memory_space=pltpu.SMEM),
            pl.BlockSpec(memory_space=pltpu.SMEM),
            pl.BlockSpec((tm, ATTN_DIM), lambda b, n: (b * nt + n, 0)),
            pl.BlockSpec((BLOCK, 2 * KV_DIM), lambda b, n: (b * nb, 0)),
            pl.BlockSpec((BLOCK, 2 * KV_DIM),
                         lambda b, n: (b * nb + jnp.maximum(n * nbk - 1, 0), 0)),
            pl.BlockSpec((tm, 2 * KV_DIM), lambda b, n: (b * nt + n, 0)),
            pl.BlockSpec((1, 128), lambda b, n: (0, 0)),
            pl.BlockSpec((1, 128), lambda b, n: (0, 0)),
        ],
        out_specs=pl.BlockSpec((tm, ATTN_DIM), lambda b, n: (b * nt + n, 0)),
        out_shape=jax.ShapeDtypeStruct(q.shape, F32),
        scratch_shapes=[pltpu.VMEM((3, HEADS, BLOCK, 3 * BLOCK), F32)],
        compiler_params=pltpu.CompilerParams(
            dimension_semantics=("arbitrary", "arbitrary"), vmem_limit_bytes=VMEM_LIMIT),
    )(rel_bias, sinks, q, kv, kv, kv, q_norm_w2, k_norm_w2)


def _ffn_kernel(lp, h_ref, ya_ref, yb_ref, wo_ref, nw_ref, wup_ref, cw_ref, cb_ref, wdn_ref,
                o_ref, carry_ref, hbuf_ref):
    tm = h_ref.shape[0]
    grow = lax.broadcasted_iota(jnp.int32, (tm, 1), 0) + pl.program_id(0) * tm
    pos = grow % lp

    @pl.when(pl.program_id(0) * tm % lp == 0)
    def _():
        carry_ref[...] = jnp.zeros_like(carry_ref)

    h = (h_ref[...] + _dot(ya_ref[...].astype(BF16), wo_ref[0:RWKV_DIM, :])
         + _dot(yb_ref[...].astype(BF16), wo_ref[RWKV_DIM:, :]))
    h = jnp.where(pos >= PAD, h, 0.0)
    ms = jnp.mean(h * h, axis=-1, keepdims=True)
    u = (h * lax.rsqrt(ms + NORM_EPS) * nw_ref[...]).astype(BF16)
    acc = h
    cw = 256
    nj = D_FF // cw
    nslot = hbuf_ref.shape[0]

    def up(j):
        return [_dot(u, wup_ref[:, base + cw * j:base + cw * (j + 1)]) for base in (0, D_FF)]

    hids = up(0)
    for j in range(nj):
        nxt = up(j + 1) if j + 1 < nj else None
        halves = []
        for half, (base, hid) in enumerate(zip((0, D_FF), hids)):
            lo = base + cw * j
            buf = hbuf_ref.at[(2 * j + half) % nslot]
            buf[0:8, :] = carry_ref[:, lo:lo + cw]
            buf[8:8 + tm, :] = hid
            carry_ref[:, lo:lo + cw] = hid[tm - 8:tm, :]
            halves.append(hid * cw_ref[0:1, lo:lo + cw]
                          + buf[7:7 + tm, :] * cw_ref[1:2, lo:lo + cw]
                          + buf[6:6 + tm, :] * cw_ref[2:3, lo:lo + cw]
                          + cb_ref[:, lo:lo + cw])
        gate, val = halves
        act = (gate * jax.nn.sigmoid(gate) * val).astype(BF16)
        acc = acc + _dot(act, wdn_ref[cw * j:cw * (j + 1), :])
        hids = nxt
    o_ref[...] = acc


def _ffn(h, ya, yb, w_out, norm_w, w_up, conv_w, conv_b, w_down, lp, tm):
    tp = h.shape[0]
    row = lambda i: (i, 0)
    const = lambda i: (0, 0)
    return pl.pallas_call(
        functools.partial(_ffn_kernel, lp),
        grid=(tp // tm,),
        in_specs=[
            pl.BlockSpec((tm, D_MODEL), row),
            pl.BlockSpec((tm, RWKV_DIM), row),
            pl.BlockSpec((tm, ATTN_DIM), row),
            pl.BlockSpec((D_MODEL, D_MODEL), const),
            pl.BlockSpec((1, D_MODEL), const),
            pl.BlockSpec((D_MODEL, 2 * D_FF), const),
            pl.BlockSpec((3, 2 * D_FF), const),
            pl.BlockSpec((1, 2 * D_FF), const),
            pl.BlockSpec((D_FF, D_MODEL), const),
        ],
        out_specs=pl.BlockSpec((tm, D_MODEL), row),
        out_shape=jax.ShapeDtypeStruct(h.shape, F32),
        scratch_shapes=[pltpu.VMEM((8, 2 * D_FF), F32), pltpu.VMEM((4, tm + 8, 256), F32)],
        compiler_params=pltpu.CompilerParams(
            dimension_semantics=("arbitrary",), vmem_limit_bytes=VMEM_LIMIT),
    )(h, ya, yb, w_out, norm_w, w_up, conv_w, conv_b, w_down)


def _row_tile(lp, cap, unit=BLOCK):
    n = lp // unit
    best = 1
    for f in range(1, n + 1):
        if n % f == 0 and f * unit <= cap:
            best = f
    return best * unit


MXU_OVERLAP_ROWS = 336


def kernel(x, meta_tokens, rel_bias, norm1_w, w_in, shift_mu, decay_w0, decay_up, aaa_a0, aaa_up, gate_up, k_k, k_a, r_k, lnx_w, lnx_b, q_norm_w, k_norm_w, sinks, w_out, norm2_w, w_up, conv_w, conv_b, w_down):
    batch, seq, _ = x.shape
    lp = PAD + N_META + seq
    assert lp % BLOCK == 0
    tm = _row_tile(lp, 640)
    tm_mm = _row_tile(lp, MXU_OVERLAP_ROWS, 16)
    meta = jnp.broadcast_to(meta_tokens[None].astype(x.dtype), (batch, N_META, D_MODEL))
    h = jnp.concatenate([jnp.zeros((batch, PAD, D_MODEL), x.dtype), meta, x], axis=1)
    h = h.reshape(batch * lp, D_MODEL)
    row2 = lambda t: t.reshape(1, -1)
    zeros64 = jnp.zeros((64, RWKV_DIM), F32)
    for layer in range(norm1_w.shape[0]):
        pr, q, kv = _inproj(h, row2(norm1_w[layer]), w_in[layer].astype(BF16), tm)
        y_rwkv = _rwkv(
            pr.reshape(batch, lp, RWKV_IN), row2(shift_mu[layer]), row2(decay_w0[layer]),
            jnp.concatenate([decay_up[layer], zeros64], axis=0), row2(aaa_a0[layer]),
            jnp.concatenate([zeros64, aaa_up[layer]], axis=0).astype(BF16),
            gate_up[layer].astype(BF16), row2(k_k[layer]), row2(k_a[layer]),
            row2(lnx_w[layer]), row2(lnx_b[layer]), row2(r_k[layer]))
        y_attn = _attention(q, kv, rel_bias, sinks[layer].reshape(1, HEADS),
                            jnp.tile(q_norm_w[layer], 2).reshape(1, 128),
                            jnp.tile(k_norm_w[layer], 2).reshape(1, 128), batch, lp, tm)
        h = _ffn(h, y_rwkv.reshape(batch * lp, RWKV_DIM), y_attn, w_out[layer].astype(BF16),
                 row2(norm2_w[layer]), w_up[layer].astype(BF16), conv_w[layer],
                 row2(conv_b[layer]), w_down[layer].astype(BF16), lp, tm_mm)
    return h.reshape(batch, lp, D_MODEL)[:, PAD + N_META:]
```

```python
import math

import jax
import jax.numpy as jnp
from jax import lax
from jax.experimental import pallas as pl
from jax.experimental.pallas import tpu as pltpu

F32 = jnp.float32
BF16 = jnp.bfloat16
HI = lax.Precision.HIGHEST

D_MODEL = 1024
N_META = 16
BLOCK = 128
LEAD = 2 * BLOCK
PAD = LEAD - N_META
META_BLOCK = 1
HEADS = 8
HEAD_DIM = 64
RWKV_DIM = 512
RWKV_IN = 1792
ATTN_DIM = 512
KV_DIM = 128
IN_DIM = 2560
D_FF = 2816
CHUNK = 64
CHUNKS_PER_STEP = 2
NORM_EPS = 1e-6
GN_EPS = HEAD_DIM * 1e-5
MASK_VALUE = -1e30
REL_BUCKETS = 32
REL_MAX_EXACT = 16
REL_MAX_DIST = 128
VMEM_LIMIT = 56 * 1024 * 1024


def _dot(a, b, precision=None):
    return lax.dot_general(a, b, (((1,), (0,)), ((), ())), precision=precision,
                           preferred_element_type=F32)


def _dot_nt(a, b, precision=None):
    return lax.dot_general(a, b, (((1,), (1,)), ((), ())), precision=precision,
                           preferred_element_type=F32)


def _bf(x):
    return x.astype(BF16)


def _dot_split(x, w_bf16):
    hi = _bf(x)
    lo = _bf(x - hi.astype(F32))
    return _dot(hi, w_bf16) + _dot(lo, w_bf16)


def _head_ones(n, dtype=F32):
    r = lax.broadcasted_iota(jnp.int32, (n, n), 0) // HEAD_DIM
    c = lax.broadcasted_iota(jnp.int32, (n, n), 1) // HEAD_DIM
    return (r == c).astype(dtype)


def _lead_or_x(x_ref, meta_ref):
    lead = jnp.concatenate([jnp.zeros((PAD, D_MODEL), F32), meta_ref[...]], axis=0)
    return jnp.where(pl.program_id(1) == 0, lead, x_ref[...])


def _inproj_kernel(x_ref, meta_ref, nw_ref, w_ref, pr_ref, q_ref, kv_ref):
    x = _lead_or_x(x_ref, meta_ref)
    ms = jnp.mean(x * x, axis=-1, keepdims=True)
    u = (x * lax.rsqrt(ms + NORM_EPS) * nw_ref[...]).astype(BF16)
    for j in range(IN_DIM // 256):
        c = _dot(u, w_ref[:, 256 * j:256 * (j + 1)])
        lo = 256 * j
        if lo < RWKV_IN:
            pr_ref[:, lo:lo + 256] = c
        elif lo < RWKV_IN + ATTN_DIM:
            q_ref[:, lo - RWKV_IN:lo - RWKV_IN + 256] = c
        else:
            kv_ref[...] = c


def _x_tile_spec(seq):
    nx = seq // LEAD
    return pl.BlockSpec((LEAD, D_MODEL), lambda b, i: (b * nx + jnp.maximum(i - 1, 0), 0))


def _inproj(x2, meta, norm_w, w_in_bf16, batch, seq):
    nt = (LEAD + seq) // LEAD
    tp = batch * nt * LEAD
    row = lambda b, i: (b * nt + i, 0)
    const = lambda b, i: (0, 0)
    return pl.pallas_call(
        _inproj_kernel,
        grid=(batch, nt),
        in_specs=[
            _x_tile_spec(seq),
            pl.BlockSpec((N_META, D_MODEL), const),
            pl.BlockSpec((1, D_MODEL), const),
            pl.BlockSpec((D_MODEL, IN_DIM), const),
        ],
        out_specs=[
            pl.BlockSpec((LEAD, RWKV_IN), row),
            pl.BlockSpec((LEAD, ATTN_DIM), row),
            pl.BlockSpec((LEAD, 2 * KV_DIM), row),
        ],
        out_shape=[
            jax.ShapeDtypeStruct((tp, RWKV_IN), F32),
            jax.ShapeDtypeStruct((tp, ATTN_DIM), F32),
            jax.ShapeDtypeStruct((tp, 2 * KV_DIM), F32),
        ],
        compiler_params=pltpu.CompilerParams(
            dimension_semantics=("arbitrary", "arbitrary"), vmem_limit_bytes=VMEM_LIMIT),
    )(x2, meta, norm_w, w_in_bf16)


def _blockdiag(z, lane_lo):
    zero = jnp.zeros_like(z)
    return jnp.concatenate([jnp.where(lane_lo, z, zero), jnp.where(lane_lo, zero, z)], axis=0)


def _unit_lower_inverse(mats, eye, blk_masks, lane_lo):
    bd = lambda z: _blockdiag(z, lane_lo)
    d = [jnp.where(blk_masks[0], a, 0.0) for a in mats]
    db = [_bf(x) for x in d]
    bdd = [bd(x) for x in db]
    d2 = [_dot(x, y) for x, y in zip(db, bdd)]
    d2b = [_bf(x) for x in d2]
    d34 = [_dot(x, jnp.concatenate([y, bd(x)], axis=1)) for x, y in zip(d2b, bdd)]
    s3 = [eye + x + y + z[:, 0:128] for x, y, z in zip(d, d2, d34)]
    t = [x + _dot(_bf(z[:, 128:256]), bd(_bf(x))) for x, z in zip(s3, d34)]
    for m in blk_masks[1:]:
        tb = [_bf(x) for x in t]
        inner = [_bf(_dot(_bf(jnp.where(m, a, 0.0)), bd(x))) for a, x in zip(mats, tb)]
        t = [x + _dot(xb, bd(i)) for x, xb, i in zip(t, tb, inner)]
    return t


def _rwkv_kernel(p_ref, halo_ref, mu_ref, w0_ref, decup_ref, a0_ref, aup_ref, gup_ref,
                 kk_ref, ka_ref, lnw_ref, lnb_ref, rk_ref, o_ref, s_ref):
    step = pl.program_id(0)
    nb, rows, _ = p_ref.shape
    c = CHUNK

    @pl.when(step == 0)
    def _():
        s_ref[...] = jnp.zeros_like(s_ref)

    ti = lax.broadcasted_iota(jnp.int32, (c, 128), 0)
    lane = lax.broadcasted_iota(jnp.int32, (c, 128), 1)
    si = lane % HEAD_DIM
    lane_lo = lane < HEAD_DIM
    strict = si < ti
    incl = si <= ti
    eye = si == ti
    eye_f = eye.astype(F32)
    blk_masks = [(ti // 8) == (si // 8)]
    size = 8
    while size < c:
        blk_masks.append(((ti // (2 * size)) == (si // (2 * size))) & ((ti // size) != (si // size)))
        size *= 2
    ones = _head_ones(128, BF16)
    row = lax.broadcasted_iota(jnp.int32, (rows, 1), 0)
    bd = lambda z: _blockdiag(z, lane_lo)

    def pair_t(z):
        zt = bd(z).T
        return zt[0:c, :] + zt[c:2 * c, :]

    npair = HEADS // 2

    def head_sums(x, split=False):
        n = x.shape[0]
        xs = jnp.concatenate([x[:, 128 * j:128 * (j + 1)] for j in range(npair)], axis=0)
        ys = _dot_split(xs, ones) if split else _dot(_bf(xs), ones)
        return jnp.concatenate([ys[n * j:n * (j + 1)] for j in range(npair)], axis=1)

    prevs = []
    for b in range(nb):
        prev = pltpu.roll(p_ref[b], shift=1, axis=0)
        prev = jnp.where(row == 0, halo_ref[b, 7:8, :], prev)
        prevs.append(jnp.where((row == 0) & (step == 0), 0.0, prev))
    p = jnp.concatenate([p_ref[b] for b in range(nb)], axis=0)
    ps = p + (jnp.concatenate(prevs, axis=0) - p) * mu_ref[...]
    r = ps[:, 0:512]
    k = ps[:, 512:1024]
    v = ps[:, 1024:1536]
    wa = ps[:, 1536:1664]
    gd = ps[:, 1664:1792]
    nz = -(w0_ref[...] + _dot(jnp.tanh(wa), decup_ref[...], HI))
    softplus = jnp.maximum(nz, 0.0) + jnp.log1p(jnp.exp(-jnp.abs(nz)))
    lw = -jnp.exp(-softplus - 0.5)
    a = jax.nn.sigmoid(a0_ref[...] + _dot(_bf(wa), aup_ref[...]))
    g = _dot(_bf(jax.nn.sigmoid(gd)), gup_ref[...])
    kk = k * kk_ref[...]
    kn = kk / jnp.maximum(jnp.sqrt(head_sums(kk * kk)), 1e-12)
    bb = kn * a
    k = k * (1.0 + (a - 1.0) * ka_ref[...])
    bonus = head_sums(r * k * rk_ref[...]) * v

    rowc = lax.broadcasted_iota(jnp.int32, (c, 1), 0)
    chains = []
    for b in range(nb):
        for ch in range(rows // c):
            rs = slice(c * ch, c * (ch + 1))
            ra = slice(rows * b + c * ch, rows * b + c * (ch + 1))
            lwc = lw[ra]
            cum = lwc
            for sh in (1, 2, 4, 8, 16, 32):
                cum = cum + jnp.where(rowc >= sh, pltpu.roll(cum, shift=sh, axis=0), 0.0)
            cum_last = cum[c - 1:c, :]
            e_neg = jnp.exp(-cum)
            e_end = jnp.exp(cum_last - cum)
            rt = r[ra] * jnp.exp(cum)
            at = -kn[ra] * jnp.exp(cum - lwc)
            kt, bt = k[ra] * e_neg, bb[ra] * e_neg
            kh, bh = k[ra] * e_end, bb[ra] * e_end
            wc = jnp.exp(cum_last)
            for j in range(npair):
                sl = slice(128 * j, 128 * (j + 1))
                chains.append(dict(
                    b=b, ch=ch, j=j, rs=rs, sl=sl, rt=rt[:, sl], at=_bf(at[:, sl]),
                    rhs=jnp.concatenate([bd(_bf(bt[:, sl])), bd(_bf(kt[:, sl]))], axis=0),
                    vbd=bd(_bf(v[ra, sl])),
                    bht=_bf(pair_t(bh[:, sl])), kht=_bf(pair_t(kh[:, sl])),
                    wc=wc[:, sl], bonus=bonus[ra, sl], g=g[ra, sl]))

    for cd in chains:
        cd["pm"] = _dot_nt(jnp.concatenate([cd["at"], _bf(cd["rt"])], axis=0), cd.pop("rhs"))
    for cd in chains:
        pm = cd.pop("pm")
        cd["a_ab"] = jnp.where(strict, pm[0:c, 0:128], 0.0)
        cd["a_ak"] = _bf(jnp.where(strict, pm[0:c, 128:256], 0.0))
        cd["p_rb"] = _bf(jnp.where(incl, pm[c:2 * c, 0:128], 0.0))
        cd["p_rk"] = _bf(jnp.where(incl, pm[c:2 * c, 128:256], 0.0))
    tinv = _unit_lower_inverse([cd.pop("a_ab") for cd in chains], eye_f, blk_masks, lane_lo)
    for cd in chains:
        cd["xv"] = _dot(jnp.concatenate([cd["a_ak"], cd["p_rk"], cd["kht"]], axis=0), cd["vbd"])
    for cd, t in zip(chains, tinv):
        gu = _bf(_dot(_bf(t), jnp.concatenate([bd(cd["at"]), bd(_bf(cd["xv"][0:c]))], axis=1)))
        cd["gu"] = jnp.concatenate([bd(gu[:, 0:128]), bd(gu[:, 128:256])], axis=1)
    for cd in chains:
        r = _dot(jnp.concatenate([cd["p_rb"], cd["bht"]], axis=0), cd.pop("gu"))
        xv = cd.pop("xv")
        cd["y0"] = r[0:c, 128:256] + xv[c:2 * c]
        cd["n_add"] = r[c:2 * c, 128:256] + xv[2 * c:3 * c]
        cd["mq_lhs"] = _bf(jnp.concatenate([r[c:2 * c, 0:128], cd["rt"] + r[0:c, 0:128]], axis=0))
    wcc = _dot_split(jnp.concatenate(
        [jnp.where(eye, jnp.broadcast_to(cd["wc"], (c, 128)), 0.0) for cd in chains], axis=0), ones)
    for i, cd in enumerate(chains):
        cd["wc_col"] = wcc[c * i:c * (i + 1)]

    states = {(b, j): s_ref[b, j] for b in range(nb) for j in range(HEADS // 2)}
    for ch in range(rows // c):
        for cd in chains:
            if cd["ch"] != ch:
                continue
            st = states[(cd["b"], cd["j"])]
            mq = _dot(cd["mq_lhs"], bd(_bf(st)))
            states[(cd["b"], cd["j"])] = cd["wc_col"] * st + mq[0:c] + cd["n_add"]
            cd["y"] = mq[c:2 * c] + cd["y0"]
    for (b, j), st in states.items():
        s_ref[b, j] = st

    ys = jnp.concatenate([cd["y"] for cd in chains], axis=0)
    dy = ys - _dot_split(ys, ones) * (1.0 / HEAD_DIM)
    var = _dot(_bf(dy * dy), ones) * (1.0 / HEAD_DIM)
    zn = dy * lax.rsqrt(var + GN_EPS)
    for i, cd in enumerate(chains):
        sl = cd["sl"]
        yn = zn[c * i:c * (i + 1)] * lnw_ref[:, sl] + lnb_ref[:, sl]
        o_ref[cd["b"], cd["rs"], sl] = (yn + cd["bonus"]) * cd["g"]


def _rwkv(pr3, mu, w0, decup_pad, a0, aup_pad, gup, k_k, k_a, lnx_w, lnx_b, r_k):
    batch, lp, _ = pr3.shape
    rows = CHUNK * CHUNKS_PER_STEP
    const = lambda i: (0, 0)
    vec = pl.BlockSpec((1, RWKV_DIM), const)
    lora = pl.BlockSpec((128, RWKV_DIM), const)
    return pl.pallas_call(
        _rwkv_kernel,
        grid=(lp // rows,),
        in_specs=[
            pl.BlockSpec((batch, rows, RWKV_IN), lambda i: (0, i, 0)),
            pl.BlockSpec((batch, 8, RWKV_IN), lambda i: (0, jnp.maximum(i * (rows // 8) - 1, 0), 0)),
            pl.BlockSpec((1, RWKV_IN), const),
            vec, lora, vec, lora, lora, vec, vec, vec, vec, vec,
        ],
        out_specs=pl.BlockSpec((batch, rows, RWKV_DIM), lambda i: (0, i, 0)),
        out_shape=jax.ShapeDtypeStruct((batch, lp, RWKV_DIM), F32),
        scratch_shapes=[pltpu.VMEM((batch, HEADS // 2, CHUNK, 128), F32)],
        compiler_params=pltpu.CompilerParams(
            dimension_semantics=("arbitrary",), vmem_limit_bytes=VMEM_LIMIT),
    )(pr3, pr3, mu, w0, decup_pad, a0, aup_pad, gup, k_k, k_a, lnx_w, lnx_b, r_k)


def _t5_thresholds():
    n_log = REL_BUCKETS - REL_MAX_EXACT
    out = []
    for k in range(1, n_log):
        x = REL_MAX_EXACT * (REL_MAX_DIST / REL_MAX_EXACT) ** (k / n_log)
        assert min(x - math.floor(x), math.ceil(x) - x) > 1e-3
        out.append(math.ceil(x))
    return out


def _t5_bucket(d):
    d = jnp.maximum(d, 0)
    large = jnp.full(d.shape, REL_MAX_EXACT, jnp.int32)
    for t in _t5_thresholds():
        large = large + (d >= t).astype(jnp.int32)
    return jnp.where(d < REL_MAX_EXACT, d, large)


def _attn_kernel(rb_ref, sink_ref, q_ref, kv0_ref, kvp_ref, kvc_ref, qw_ref, kw_ref, o_ref, tbl_ref):
    n = pl.program_id(1)

    @pl.when((pl.program_id(0) == 0) & (n == 0))
    def _():
        q = lax.broadcasted_iota(jnp.int32, (BLOCK, 3 * BLOCK), 0)
        col = lax.broadcasted_iota(jnp.int32, (BLOCK, 3 * BLOCK), 1)
        is_meta = col < BLOCK
        for nn in range(3):
            d_meta = nn * BLOCK + q - col
            d = jnp.where(is_meta, d_meta, q + 2 * BLOCK - col)
            ok = is_meta & (col >= BLOCK - N_META) & (d_meta >= 0)
            if nn >= 2:
                ok = ok | ((col >= BLOCK) & (col < 2 * BLOCK) & (col - BLOCK > q))
            if nn >= 1:
                ok = ok | ((col >= 2 * BLOCK) & (col - 2 * BLOCK <= q))
            bucket = _t5_bucket(d)

            def per_head(h, carry):
                acc = jnp.zeros((BLOCK, 3 * BLOCK), F32)
                for bk in range(REL_BUCKETS):
                    acc = jnp.where(bucket == bk, rb_ref[bk, h], acc)
                tbl_ref[nn, h] = jnp.where(ok, acc, MASK_VALUE)
                return carry

            lax.fori_loop(0, HEADS, per_head, 0)

    ones = _head_ones(128, BF16)
    lane_lo = lax.broadcasted_iota(jnp.int32, (BLOCK, 128), 1) < HEAD_DIM
    lane_hi = jnp.logical_not(lane_lo)
    rr = lax.broadcasted_iota(jnp.int32, (128, 128), 0)
    cc = lax.broadcasted_iota(jnp.int32, (128, 128), 1)
    dup = [((rr // HEAD_DIM == g) & (rr % HEAD_DIM == cc % HEAD_DIM)).astype(BF16) for g in range(2)]

    def qk_norm(x, w):
        ms = _dot(_bf(x * x), ones) * (1.0 / HEAD_DIM)
        return x * lax.rsqrt(ms + NORM_EPS) * w

    nbk = q_ref.shape[0] // BLOCK
    grp = HEADS // 2
    qw = qw_ref[...] * HEAD_DIM ** -0.5

    kvb = [kv0_ref[...], kvp_ref[...]] + [kvc_ref[BLOCK * jb:BLOCK * (jb + 1), :] for jb in range(nbk)]
    kn = [_bf(qk_norm(x[:, 0:128], kw_ref[...])) for x in kvb]
    kd = [[_bf(_dot(x, dup[g])) for x in kn] for g in range(2)]
    vd = [[_bf(_dot(_bf(x[:, 128:256]), dup[g])) for x in kvb] for g in range(2)]
    qn = [[qk_norm(q_ref[BLOCK * jb:BLOCK * (jb + 1), 128 * c4:128 * (c4 + 1)], qw)
           for c4 in range(4)] for jb in range(nbk)]
    sinks = [jnp.concatenate([jnp.full((BLOCK, 128), sink_ref[0, grp * g + i], F32)
                              for i in range(grp)], axis=0) for g in range(2)]
    ones_cols = jnp.ones((3 * BLOCK, 128), BF16)

    chains = [(jb, g) for jb in range(nbk) for g in range(2)]
    s = []
    for jb, g in chains:
        lhs = jnp.concatenate(
            [jnp.where(lane_lo if e == 0 else lane_hi, qn[jb][2 * g + jj], 0.0)
             for jj in range(2) for e in range(2)], axis=0).astype(BF16)
        keys = jnp.concatenate([kd[g][0], kd[g][1 + jb], kd[g][2 + jb]], axis=0)
        tsel = jnp.clip(n * nbk + jb - META_BLOCK, 0, 2)
        tb = tbl_ref[tsel, pl.ds(grp * g, grp)].reshape(grp * BLOCK, 3 * BLOCK)
        s.append(_dot_nt(lhs, keys) + tb)
    mb = []
    for x, (_, g) in zip(s, chains):
        m3 = jnp.maximum(jnp.maximum(x[:, 0:128], x[:, 128:256]), x[:, 256:384])
        m = jnp.maximum(jnp.max(m3, axis=-1, keepdims=True), sinks[g][:, 0:1])
        mb.append(jnp.broadcast_to(m, (grp * BLOCK, 128)))
    ex = [jnp.concatenate([_bf(jnp.exp(x[:, 128 * i:128 * (i + 1)] - mm)) for i in range(3)], axis=1)
          for x, mm in zip(s, mb)]
    out = []
    for x, mm, (jb, g) in zip(ex, mb, chains):
        vals = jnp.concatenate([vd[g][0], vd[g][1 + jb], vd[g][2 + jb]], axis=0)
        od = _dot(x, jnp.concatenate([vals, ones_cols], axis=1))
        den = od[:, 128:256] + jnp.exp(sinks[g] - mm)
        out.append(od[:, 0:128] * (1.0 / den))
    for o, (jb, g) in zip(out, chains):
        for jj in range(2):
            col = 2 * g + jj
            o_ref[BLOCK * jb:BLOCK * (jb + 1), 128 * col:128 * (col + 1)] = jnp.where(
                lane_lo, o[256 * jj:256 * jj + 128], o[256 * jj + 128:256 * jj + 256])


def _attention(q, kv, rel_bias, sinks, q_norm_w2, k_norm_w2, batch, lp, tm):
    nb = lp // BLOCK
    nt = lp // tm
    nbk = tm // BLOCK
    return pl.pallas_call(
        _attn_kernel,
        grid=(batch, nt),
        in_specs=[
            pl.BlockSpec(memory_space=pltpu.SMEM),
            pl.BlockSpec(memory_space=pltpu.SMEM),
            pl.BlockSpec((tm, ATTN_DIM), lambda b, n: (b * nt + n, 0)),
            pl.BlockSpec((BLOCK, 2 * KV_DIM), lambda b, n: (b * nb + META_BLOCK, 0)),
            pl.BlockSpec((BLOCK, 2 * KV_DIM),
                         lambda b, n: (b * nb + jnp.maximum(n * nbk - 1, 0), 0)),
            pl.BlockSpec((tm, 2 * KV_DIM), lambda b, n: (b * nt + n, 0)),
            pl.BlockSpec((1, 128), lambda b, n: (0, 0)),
            pl.BlockSpec((1, 128), lambda b, n: (0, 0)),
        ],
        out_specs=pl.BlockSpec((tm, ATTN_DIM), lambda b, n: (b * nt + n, 0)),
        out_shape=jax.ShapeDtypeStruct(q.shape, F32),
        scratch_shapes=[pltpu.VMEM((3, HEADS, BLOCK, 3 * BLOCK), F32)],
        compiler_params=pltpu.CompilerParams(
            dimension_semantics=("arbitrary", "arbitrary"), vmem_limit_bytes=VMEM_LIMIT),
    )(rel_bias, sinks, q, kv, kv, kv, q_norm_w2, k_norm_w2)


def _ffn_kernel(x_ref, meta_ref, ya_ref, yb_ref, wo_ref, nw_ref, wup_ref, cw_ref, cb_ref, wdn_ref,
                o_ref, carry_ref, hbuf_ref):
    tm = x_ref.shape[0]
    pos = lax.broadcasted_iota(jnp.int32, (tm, 1), 0) + pl.program_id(1) * tm

    @pl.when(pl.program_id(1) == 0)
    def _():
        carry_ref[...] = jnp.zeros_like(carry_ref)

    h = (_lead_or_x(x_ref, meta_ref) + _dot(ya_ref[...].astype(BF16), wo_ref[0:RWKV_DIM, :])
         + _dot(yb_ref[...].astype(BF16), wo_ref[RWKV_DIM:, :]))
    h = jnp.where(pos >= PAD, h, 0.0)
    ms = jnp.mean(h * h, axis=-1, keepdims=True)
    u = (h * lax.rsqrt(ms + NORM_EPS) * nw_ref[...]).astype(BF16)
    acc = h
    cw = 256
    nj = D_FF // cw
    nslot = hbuf_ref.shape[0]

    def up(j):
        return [_dot(u, wup_ref[:, base + cw * j:base + cw * (j + 1)]) for base in (0, D_FF)]

    hids = up(0)
    for j in range(nj):
        nxt = up(j + 1) if j + 1 < nj else None
        halves = []
        for half, (base, hid) in enumerate(zip((0, D_FF), hids)):
            lo = base + cw * j
            buf = hbuf_ref.at[(2 * j + half) % nslot]
            buf[0:8, :] = carry_ref[:, lo:lo + cw]
            buf[8:8 + tm, :] = hid
            carry_ref[:, lo:lo + cw] = hid[tm - 8:tm, :]
            halves.append(hid * cw_ref[0:1, lo:lo + cw]
                          + buf[7:7 + tm, :] * cw_ref[1:2, lo:lo + cw]
                          + buf[6:6 + tm, :] * cw_ref[2:3, lo:lo + cw]
                          + cb_ref[:, lo:lo + cw])
        gate, val = halves
        act = (gate * jax.nn.sigmoid(gate) * val).astype(BF16)
        acc = acc + _dot(act, wdn_ref[cw * j:cw * (j + 1), :])
        hids = nxt
    o_ref[...] = acc


def _ffn(x2, meta, ya, yb, w_out, norm_w, w_up, conv_w, conv_b, w_down, batch, seq):
    tm = LEAD
    nt = (LEAD + seq) // tm
    nx = seq // tm
    row = lambda b, i: (b * nt + i, 0)
    const = lambda b, i: (0, 0)
    return pl.pallas_call(
        _ffn_kernel,
        grid=(batch, nt),
        in_specs=[
            _x_tile_spec(seq),
            pl.BlockSpec((N_META, D_MODEL), const),
            pl.BlockSpec((tm, RWKV_DIM), row),
            pl.BlockSpec((tm, ATTN_DIM), row),
            pl.BlockSpec((D_MODEL, D_MODEL), const),
            pl.BlockSpec((1, D_MODEL), const),
            pl.BlockSpec((D_MODEL, 2 * D_FF), const),
            pl.BlockSpec((3, 2 * D_FF), const),
            pl.BlockSpec((1, 2 * D_FF), const),
            pl.BlockSpec((D_FF, D_MODEL), const),
        ],
        out_specs=pl.BlockSpec((tm, D_MODEL), lambda b, i: (b * nx + jnp.maximum(i - 1, 0), 0)),
        out_shape=jax.ShapeDtypeStruct(x2.shape, F32),
        scratch_shapes=[pltpu.VMEM((8, 2 * D_FF), F32), pltpu.VMEM((4, tm + 8, 256), F32)],
        compiler_params=pltpu.CompilerParams(
            dimension_semantics=("arbitrary", "arbitrary"), vmem_limit_bytes=VMEM_LIMIT),
    )(x2, meta, ya, yb, w_out, norm_w, w_up, conv_w, conv_b, w_down)


def _row_tile(lp, cap, unit=BLOCK):
    n = lp // unit
    best = 1
    for f in range(1, n + 1):
        if n % f == 0 and f * unit <= cap:
            best = f
    return best * unit


def kernel(x, meta_tokens, rel_bias, norm1_w, w_in, shift_mu, decay_w0, decay_up, aaa_a0, aaa_up, gate_up, k_k, k_a, r_k, lnx_w, lnx_b, q_norm_w, k_norm_w, sinks, w_out, norm2_w, w_up, conv_w, conv_b, w_down):
    batch, seq, _ = x.shape
    assert norm1_w.shape[0] == 1 and seq % LEAD == 0
    lp = LEAD + seq
    layer = 0
    x2 = x.reshape(batch * seq, D_MODEL)
    meta = meta_tokens.astype(x.dtype)
    row2 = lambda t: t.reshape(1, -1)
    zeros64 = jnp.zeros((64, RWKV_DIM), F32)
    pr, q, kv = _inproj(x2, meta, row2(norm1_w[layer]), w_in[layer].astype(BF16), batch, seq)
    y_rwkv = _rwkv(
        pr.reshape(batch, lp, RWKV_IN), row2(shift_mu[layer]), row2(decay_w0[layer]),
        jnp.concatenate([decay_up[layer], zeros64], axis=0), row2(aaa_a0[layer]),
        jnp.concatenate([zeros64, aaa_up[layer]], axis=0).astype(BF16),
        gate_up[layer].astype(BF16), row2(k_k[layer]), row2(k_a[layer]),
        row2(lnx_w[layer]), row2(lnx_b[layer]), row2(r_k[layer]))
    y_attn = _attention(q, kv, rel_bias, sinks[layer].reshape(1, HEADS),
                        jnp.tile(q_norm_w[layer], 2).reshape(1, 128),
                        jnp.tile(k_norm_w[layer], 2).reshape(1, 128), batch, lp,
                        _row_tile(lp, 768))
    out = _ffn(x2, meta, y_rwkv.reshape(batch * lp, RWKV_DIM), y_attn, w_out[layer].astype(BF16),
               row2(norm2_w[layer]), w_up[layer].astype(BF16), conv_w[layer],
               row2(conv_b[layer]), w_down[layer].astype(BF16), batch, seq)
    return out.reshape(batch, seq, D_MODEL)
```

```python
import math

import jax
import jax.numpy as jnp
from jax import lax
from jax.experimental import pallas as pl
from jax.experimental.pallas import tpu as pltpu

F32 = jnp.float32
BF16 = jnp.bfloat16

D_MODEL = 1024
N_META = 16
BLOCK = 128
LEAD = 2 * BLOCK
PAD = LEAD - N_META
META_BLOCK = 1
HEADS = 8
HEAD_DIM = 64
RWKV_DIM = 512
RWKV_IN = 1792
ATTN_DIM = 512
KV_DIM = 128
IN_DIM = 2560
D_FF = 2816
CHUNK = 64
CHUNKS_PER_STEP = 4
NORM_EPS = 1e-6
GN_EPS = HEAD_DIM * 1e-5
MASK_VALUE = -1e30
REL_BUCKETS = 32
REL_MAX_EXACT = 16
REL_MAX_DIST = 128
VMEM_LIMIT = 56 * 1024 * 1024


def _dot(a, b, precision=None):
    return lax.dot_general(a, b, (((1,), (0,)), ((), ())), precision=precision,
                           preferred_element_type=F32)


def _dot_nt(a, b, precision=None):
    return lax.dot_general(a, b, (((1,), (1,)), ((), ())), precision=precision,
                           preferred_element_type=F32)


def _bf(x):
    return x.astype(BF16)


def _dot_split(x, w_bf16):
    hi = _bf(x)
    lo = _bf(x - hi.astype(F32))
    return _dot(hi, w_bf16) + _dot(lo, w_bf16)


def _head_ones(n, dtype=F32):
    r = lax.broadcasted_iota(jnp.int32, (n, n), 0) // HEAD_DIM
    c = lax.broadcasted_iota(jnp.int32, (n, n), 1) // HEAD_DIM
    return (r == c).astype(dtype)


def _lead_or_x(x_ref, meta_ref):
    lead = jnp.concatenate([jnp.zeros((PAD, D_MODEL), F32), meta_ref[...]], axis=0)
    return jnp.where(pl.program_id(1) == 0, lead, x_ref[...])


def _inproj_kernel(x_ref, meta_ref, nw_ref, w_ref, pr_ref, q_ref, kv_ref):
    x = _lead_or_x(x_ref, meta_ref)
    ms = jnp.mean(x * x, axis=-1, keepdims=True)
    u = (x * lax.rsqrt(ms + NORM_EPS) * nw_ref[...]).astype(BF16)
    for j in range(IN_DIM // 256):
        c = _dot(u, w_ref[:, 256 * j:256 * (j + 1)])
        lo = 256 * j
        if lo < RWKV_IN:
            pr_ref[:, lo:lo + 256] = c
        elif lo < RWKV_IN + ATTN_DIM:
            q_ref[:, lo - RWKV_IN:lo - RWKV_IN + 256] = c
        else:
            kv_ref[...] = c


def _x_tile_spec(seq):
    nx = seq // LEAD
    return pl.BlockSpec((LEAD, D_MODEL), lambda b, i: (b * nx + jnp.maximum(i - 1, 0), 0))


def _inproj(x2, meta, norm_w, w_in_bf16, batch, seq):
    nt = (LEAD + seq) // LEAD
    tp = batch * nt * LEAD
    row = lambda b, i: (b * nt + i, 0)
    const = lambda b, i: (0, 0)
    return pl.pallas_call(
        _inproj_kernel,
        grid=(batch, nt),
        in_specs=[
            _x_tile_spec(seq),
            pl.BlockSpec((N_META, D_MODEL), const),
            pl.BlockSpec((1, D_MODEL), const),
            pl.BlockSpec((D_MODEL, IN_DIM), const),
        ],
        out_specs=[
            pl.BlockSpec((LEAD, RWKV_IN), row),
            pl.BlockSpec((LEAD, ATTN_DIM), row),
            pl.BlockSpec((LEAD, 2 * KV_DIM), row),
        ],
        out_shape=[
            jax.ShapeDtypeStruct((tp, RWKV_IN), F32),
            jax.ShapeDtypeStruct((tp, ATTN_DIM), F32),
            jax.ShapeDtypeStruct((tp, 2 * KV_DIM), F32),
        ],
        compiler_params=pltpu.CompilerParams(
            dimension_semantics=("arbitrary", "arbitrary"), vmem_limit_bytes=VMEM_LIMIT),
    )(x2, meta, norm_w, w_in_bf16)


def _blockdiag(z, lane_lo):
    zero = jnp.zeros_like(z)
    return jnp.concatenate([jnp.where(lane_lo, z, zero), jnp.where(lane_lo, zero, z)], axis=0)


def _unit_lower_inverse(mats, eye, blk_masks, lane_lo):
    bd = lambda z: _blockdiag(z, lane_lo)
    d = [jnp.where(blk_masks[0], a, 0.0) for a in mats]
    db = [_bf(x) for x in d]
    bdd = [bd(x) for x in db]
    d2 = [_dot(x, y) for x, y in zip(db, bdd)]
    d2b = [_bf(x) for x in d2]
    d34 = [_dot(x, jnp.concatenate([y, bd(x)], axis=1)) for x, y in zip(d2b, bdd)]
    s3 = [eye + x + y + z[:, 0:128] for x, y, z in zip(d, d2, d34)]
    t = [x + _dot(_bf(z[:, 128:256]), bd(_bf(x))) for x, z in zip(s3, d34)]
    for m in blk_masks[1:]:
        tb = [_bf(x) for x in t]
        inner = [_bf(_dot(_bf(jnp.where(m, a, 0.0)), bd(x))) for a, x in zip(mats, tb)]
        t = [x + _dot(xb, bd(i)) for x, xb, i in zip(t, tb, inner)]
    return t


def _rwkv_kernel(p_ref, halo_ref, mu_ref, w0_ref, decup_ref, a0_ref, aup_ref, gup_ref,
                 kk_ref, ka_ref, lnw_ref, lnb_ref, rk_ref, o_ref, s_ref):
    step = pl.program_id(0)
    nb, rows, _ = p_ref.shape
    c = CHUNK

    @pl.when(step == 0)
    def _():
        s_ref[...] = jnp.zeros_like(s_ref)

    ti = lax.broadcasted_iota(jnp.int32, (c, 128), 0)
    lane = lax.broadcasted_iota(jnp.int32, (c, 128), 1)
    si = lane % HEAD_DIM
    lane_lo = lane < HEAD_DIM
    strict = si < ti
    incl = si <= ti
    eye = si == ti
    eye_f = eye.astype(F32)
    blk_masks = [(ti // 8) == (si // 8)]
    size = 8
    while size < c:
        blk_masks.append(((ti // (2 * size)) == (si // (2 * size))) & ((ti // size) != (si // size)))
        size *= 2
    ones = _head_ones(128, BF16)
    row = lax.broadcasted_iota(jnp.int32, (rows, 1), 0)
    bd = lambda z: _blockdiag(z, lane_lo)

    def pair_t(z):
        zt = bd(z).T
        return zt[0:c, :] + zt[c:2 * c, :]

    npair = HEADS // 2

    def head_sums(x, split=False):
        n = x.shape[0]
        xs = jnp.concatenate([x[:, 128 * j:128 * (j + 1)] for j in range(npair)], axis=0)
        ys = _dot_split(xs, ones) if split else _dot(_bf(xs), ones)
        return jnp.concatenate([ys[n * j:n * (j + 1)] for j in range(npair)], axis=1)

    prevs = []
    for b in range(nb):
        prev = pltpu.roll(p_ref[b], shift=1, axis=0)
        prev = jnp.where(row == 0, halo_ref[b, 7:8, :], prev)
        prevs.append(jnp.where((row == 0) & (step == 0), 0.0, prev))
    p = jnp.concatenate([p_ref[b] for b in range(nb)], axis=0)
    ps = p + (jnp.concatenate(prevs, axis=0) - p) * mu_ref[...]
    r = ps[:, 0:512]
    k = ps[:, 512:1024]
    v = ps[:, 1024:1536]
    wa = ps[:, 1536:1664]
    gd = ps[:, 1664:1792]
    th = jnp.tanh(wa)
    th_hi = _bf(th)
    th_lo = _bf(th - th_hi.astype(F32))
    z = w0_ref[...] + _dot(jnp.concatenate([th_hi, th_lo, th_hi], axis=1), decup_ref[...])
    lw = -math.exp(-0.5) * jax.nn.sigmoid(z)
    a = jax.nn.sigmoid(a0_ref[...] + _dot(_bf(wa), aup_ref[...]))
    g = _dot(_bf(jax.nn.sigmoid(gd)), gup_ref[...])
    kk = k * kk_ref[...]
    kn = kk * lax.rsqrt(jnp.maximum(head_sums(kk * kk), 1e-24))
    bb = kn * a
    k = k * (1.0 + (a - 1.0) * ka_ref[...])
    bonus = head_sums(r * k * rk_ref[...]) * v

    rowc = lax.broadcasted_iota(jnp.int32, (c, 1), 0)
    chains = []
    for b in range(nb):
        for ch in range(rows // c):
            rs = slice(c * ch, c * (ch + 1))
            ra = slice(rows * b + c * ch, rows * b + c * (ch + 1))
            lwc = lw[ra]
            cum = lwc
            for sh in (1, 2, 4, 8, 16, 32):
                cum = cum + jnp.where(rowc >= sh, pltpu.roll(cum, shift=sh, axis=0), 0.0)
            cum_last = cum[c - 1:c, :]
            e_neg = jnp.exp(-cum)
            e_end = jnp.exp(cum_last - cum)
            rt = r[ra] * jnp.exp(cum)
            at = -kn[ra] * jnp.exp(cum - lwc)
            kt, bt = k[ra] * e_neg, bb[ra] * e_neg
            kh, bh = k[ra] * e_end, bb[ra] * e_end
            wc = jnp.exp(cum_last)
            for j in range(npair):
                sl = slice(128 * j, 128 * (j + 1))
                chains.append(dict(
                    b=b, ch=ch, j=j, rs=rs, sl=sl, rt=rt[:, sl], at=_bf(at[:, sl]),
                    rhs=jnp.concatenate([bd(_bf(bt[:, sl])), bd(_bf(kt[:, sl]))], axis=0),
                    vbd=bd(_bf(v[ra, sl])),
                    bht=_bf(pair_t(bh[:, sl])), kht=_bf(pair_t(kh[:, sl])),
                    wc=wc[:, sl], bonus=bonus[ra, sl], g=g[ra, sl]))

    for cd in chains:
        cd["pm"] = _dot_nt(jnp.concatenate([cd["at"], _bf(cd["rt"])], axis=0), cd.pop("rhs"))
    for cd in chains:
        pm = cd.pop("pm")
        cd["a_ab"] = jnp.where(strict, pm[0:c, 0:128], 0.0)
        cd["a_ak"] = _bf(jnp.where(strict, pm[0:c, 128:256], 0.0))
        cd["p_rb"] = _bf(jnp.where(incl, pm[c:2 * c, 0:128], 0.0))
        cd["p_rk"] = _bf(jnp.where(incl, pm[c:2 * c, 128:256], 0.0))
    tinv = _unit_lower_inverse([cd.pop("a_ab") for cd in chains], eye_f, blk_masks, lane_lo)
    for cd in chains:
        cd["xv"] = _dot(jnp.concatenate([cd["a_ak"], cd["p_rk"], cd["kht"]], axis=0), cd["vbd"])
    for cd, t in zip(chains, tinv):
        gu = _bf(_dot(_bf(t), jnp.concatenate([bd(cd["at"]), bd(_bf(cd["xv"][0:c]))], axis=1)))
        cd["gu"] = jnp.concatenate([bd(gu[:, 0:128]), bd(gu[:, 128:256])], axis=1)
    for cd in chains:
        r = _dot(jnp.concatenate([cd["p_rb"], cd["bht"]], axis=0), cd.pop("gu"))
        xv = cd.pop("xv")
        cd["y0"] = r[0:c, 128:256] + xv[c:2 * c]
        cd["n_add"] = r[c:2 * c, 128:256] + xv[2 * c:3 * c]
        cd["mq_lhs"] = _bf(jnp.concatenate([r[c:2 * c, 0:128], cd["rt"] + r[0:c, 0:128]], axis=0))
    wcc = _dot_split(jnp.concatenate(
        [jnp.where(eye, jnp.broadcast_to(cd["wc"], (c, 128)), 0.0) for cd in chains], axis=0), ones)
    for i, cd in enumerate(chains):
        cd["wc_col"] = wcc[c * i:c * (i + 1)]

    states = {(b, j): s_ref[b, j] for b in range(nb) for j in range(HEADS // 2)}
    for ch in range(rows // c):
        for cd in chains:
            if cd["ch"] != ch:
                continue
            st = states[(cd["b"], cd["j"])]
            mq = _dot(cd["mq_lhs"], bd(_bf(st)))
            states[(cd["b"], cd["j"])] = cd["wc_col"] * st + mq[0:c] + cd["n_add"]
            cd["y"] = mq[c:2 * c] + cd["y0"]
    for (b, j), st in states.items():
        s_ref[b, j] = st

    ys = jnp.concatenate([cd["y"] for cd in chains], axis=0)
    dy = ys - _dot_split(ys, ones) * (1.0 / HEAD_DIM)
    var = _dot(_bf(dy * dy), ones) * (1.0 / HEAD_DIM)
    zn = dy * lax.rsqrt(var + GN_EPS)
    for i, cd in enumerate(chains):
        sl = cd["sl"]
        yn = zn[c * i:c * (i + 1)] * lnw_ref[:, sl] + lnb_ref[:, sl]
        o_ref[cd["b"], cd["rs"], sl] = (yn + cd["bonus"]) * cd["g"]


def _rwkv(pr3, mu, w0, decup_pad, a0, aup_pad, gup, k_k, k_a, lnx_w, lnx_b, r_k):
    batch, lp, _ = pr3.shape
    rows = CHUNK * CHUNKS_PER_STEP
    const = lambda i: (0, 0)
    vec = pl.BlockSpec((1, RWKV_DIM), const)
    lora = pl.BlockSpec((128, RWKV_DIM), const)
    return pl.pallas_call(
        _rwkv_kernel,
        grid=(lp // rows,),
        in_specs=[
            pl.BlockSpec((batch, rows, RWKV_IN), lambda i: (0, i, 0)),
            pl.BlockSpec((batch, 8, RWKV_IN), lambda i: (0, jnp.maximum(i * (rows // 8) - 1, 0), 0)),
            pl.BlockSpec((1, RWKV_IN), const),
            vec, pl.BlockSpec((3 * 128, RWKV_DIM), const), vec, lora, lora, vec, vec, vec, vec, vec,
        ],
        out_specs=pl.BlockSpec((batch, rows, RWKV_DIM), lambda i: (0, i, 0)),
        out_shape=jax.ShapeDtypeStruct((batch, lp, RWKV_DIM), F32),
        scratch_shapes=[pltpu.VMEM((batch, HEADS // 2, CHUNK, 128), F32)],
        compiler_params=pltpu.CompilerParams(
            dimension_semantics=("arbitrary",), vmem_limit_bytes=VMEM_LIMIT),
    )(pr3, pr3, mu, w0, decup_pad, a0, aup_pad, gup, k_k, k_a, lnx_w, lnx_b, r_k)


def _t5_thresholds():
    n_log = REL_BUCKETS - REL_MAX_EXACT
    out = []
    for k in range(1, n_log):
        x = REL_MAX_EXACT * (REL_MAX_DIST / REL_MAX_EXACT) ** (k / n_log)
        assert min(x - math.floor(x), math.ceil(x) - x) > 1e-3
        out.append(math.ceil(x))
    return out


def _t5_bucket(d):
    d = jnp.maximum(d, 0)
    large = jnp.full(d.shape, REL_MAX_EXACT, jnp.int32)
    for t in _t5_thresholds():
        large = large + (d >= t).astype(jnp.int32)
    return jnp.where(d < REL_MAX_EXACT, d, large)


def _attn_kernel(rb_ref, sink_ref, q_ref, kv0_ref, kvp_ref, kvc_ref, qw_ref, kw_ref, o_ref, tbl_ref):
    n = pl.program_id(1)

    @pl.when((pl.program_id(0) == 0) & (n == 0))
    def _():
        q = lax.broadcasted_iota(jnp.int32, (BLOCK, 3 * BLOCK), 0)
        col = lax.broadcasted_iota(jnp.int32, (BLOCK, 3 * BLOCK), 1)
        is_meta = col < BLOCK
        for nn in range(3):
            d_meta = nn * BLOCK + q - col
            d = jnp.where(is_meta, d_meta, q + 2 * BLOCK - col)
            ok = is_meta & (col >= BLOCK - N_META) & (d_meta >= 0)
            if nn >= 2:
                ok = ok | ((col >= BLOCK) & (col < 2 * BLOCK) & (col - BLOCK > q))
            if nn >= 1:
                ok = ok | ((col >= 2 * BLOCK) & (col - 2 * BLOCK <= q))
            bucket = _t5_bucket(d)

            def per_head(h, carry):
                acc = jnp.zeros((BLOCK, 3 * BLOCK), F32)
                for bk in range(REL_BUCKETS):
                    acc = jnp.where(bucket == bk, rb_ref[bk, h], acc)
                tbl_ref[nn, h] = jnp.where(ok, acc, MASK_VALUE)
                return carry

            lax.fori_loop(0, HEADS, per_head, 0)

    ones = _head_ones(128, BF16)
    lane_lo = lax.broadcasted_iota(jnp.int32, (BLOCK, 128), 1) < HEAD_DIM
    lane_hi = jnp.logical_not(lane_lo)
    rr = lax.broadcasted_iota(jnp.int32, (128, 128), 0)
    cc = lax.broadcasted_iota(jnp.int32, (128, 128), 1)
    dup = [((rr // HEAD_DIM == g) & (rr % HEAD_DIM == cc % HEAD_DIM)).astype(BF16) for g in range(2)]

    def qk_norm(x, w):
        ms = _dot(_bf(x * x), ones) * (1.0 / HEAD_DIM)
        return x * lax.rsqrt(ms + NORM_EPS) * w

    nbk = q_ref.shape[0] // BLOCK
    grp = HEADS // 2
    qw = qw_ref[...] * HEAD_DIM ** -0.5

    kvb = [kv0_ref[...], kvp_ref[...]] + [kvc_ref[BLOCK * jb:BLOCK * (jb + 1), :] for jb in range(nbk)]
    kn = [_bf(qk_norm(x[:, 0:128], kw_ref[...])) for x in kvb]
    kd = [[_bf(_dot(x, dup[g])) for x in kn] for g in range(2)]
    vd = [[_bf(_dot(_bf(x[:, 128:256]), dup[g])) for x in kvb] for g in range(2)]
    qn = [[qk_norm(q_ref[BLOCK * jb:BLOCK * (jb + 1), 128 * c4:128 * (c4 + 1)], qw)
           for c4 in range(4)] for jb in range(nbk)]
    sinks = [jnp.concatenate([jnp.full((BLOCK, 128), sink_ref[0, grp * g + i], F32)
                              for i in range(grp)], axis=0) for g in range(2)]
    ones_cols = jnp.ones((3 * BLOCK, 128), BF16)

    chains = [(jb, g) for jb in range(nbk) for g in range(2)]
    s = []
    for jb, g in chains:
        lhs = jnp.concatenate(
            [jnp.where(lane_lo if e == 0 else lane_hi, qn[jb][2 * g + jj], 0.0)
             for jj in range(2) for e in range(2)], axis=0).astype(BF16)
        keys = jnp.concatenate([kd[g][0], kd[g][1 + jb], kd[g][2 + jb]], axis=0)
        tsel = jnp.clip(n * nbk + jb - META_BLOCK, 0, 2)
        tb = tbl_ref[tsel, pl.ds(grp * g, grp)].reshape(grp * BLOCK, 3 * BLOCK)
        s.append(_dot_nt(lhs, keys) + tb)
    mb = []
    for x, (_, g) in zip(s, chains):
        m3 = jnp.maximum(jnp.maximum(x[:, 0:128], x[:, 128:256]), x[:, 256:384])
        m = jnp.maximum(jnp.max(m3, axis=-1, keepdims=True), sinks[g][:, 0:1])
        mb.append(jnp.broadcast_to(m, (grp * BLOCK, 128)))
    ex = [jnp.concatenate([_bf(jnp.exp(x[:, 128 * i:128 * (i + 1)] - mm)) for i in range(3)], axis=1)
          for x, mm in zip(s, mb)]
    out = []
    for x, mm, (jb, g) in zip(ex, mb, chains):
        vals = jnp.concatenate([vd[g][0], vd[g][1 + jb], vd[g][2 + jb]], axis=0)
        od = _dot(x, jnp.concatenate([vals, ones_cols], axis=1))
        den = od[:, 128:256] + jnp.exp(sinks[g] - mm)
        out.append(od[:, 0:128] * (1.0 / den))
    for o, (jb, g) in zip(out, chains):
        for jj in range(2):
            col = 2 * g + jj
            o_ref[BLOCK * jb:BLOCK * (jb + 1), 128 * col:128 * (col + 1)] = jnp.where(
                lane_lo, o[256 * jj:256 * jj + 128], o[256 * jj + 128:256 * jj + 256])


def _attention(q, kv, rel_bias, sinks, q_norm_w2, k_norm_w2, batch, lp, tm):
    nb = lp // BLOCK
    nt = lp // tm
    nbk = tm // BLOCK
    return pl.pallas_call(
        _attn_kernel,
        grid=(batch, nt),
        in_specs=[
            pl.BlockSpec(memory_space=pltpu.SMEM),
            pl.BlockSpec(memory_space=pltpu.SMEM),
            pl.BlockSpec((tm, ATTN_DIM), lambda b, n: (b * nt + n, 0)),
            pl.BlockSpec((BLOCK, 2 * KV_DIM), lambda b, n: (b * nb + META_BLOCK, 0)),
            pl.BlockSpec((BLOCK, 2 * KV_DIM),
                         lambda b, n: (b * nb + jnp.maximum(n * nbk - 1, 0), 0)),
            pl.BlockSpec((tm, 2 * KV_DIM), lambda b, n: (b * nt + n, 0)),
            pl.BlockSpec((1, 128), lambda b, n: (0, 0)),
            pl.BlockSpec((1, 128), lambda b, n: (0, 0)),
        ],
        out_specs=pl.BlockSpec((tm, ATTN_DIM), lambda b, n: (b * nt + n, 0)),
        out_shape=jax.ShapeDtypeStruct(q.shape, F32),
        scratch_shapes=[pltpu.VMEM((3, HEADS, BLOCK, 3 * BLOCK), F32)],
        compiler_params=pltpu.CompilerParams(
            dimension_semantics=("arbitrary", "arbitrary"), vmem_limit_bytes=VMEM_LIMIT),
    )(rel_bias, sinks, q, kv, kv, kv, q_norm_w2, k_norm_w2)


def _ffn_kernel(x_ref, meta_ref, ya_ref, yb_ref, wo_ref, nw_ref, wup_ref, cw_ref, cb_ref, wdn_ref,
                o_ref, carry_ref, hbuf_ref):
    tm = x_ref.shape[0]
    pos = lax.broadcasted_iota(jnp.int32, (tm, 1), 0) + pl.program_id(1) * tm

    @pl.when(pl.program_id(1) == 0)
    def _():
        carry_ref[...] = jnp.zeros_like(carry_ref)

    h = (_lead_or_x(x_ref, meta_ref) + _dot(ya_ref[...].astype(BF16), wo_ref[0:RWKV_DIM, :])
         + _dot(yb_ref[...].astype(BF16), wo_ref[RWKV_DIM:, :]))
    h = jnp.where(pos >= PAD, h, 0.0)
    ms = jnp.mean(h * h, axis=-1, keepdims=True)
    u = (h * lax.rsqrt(ms + NORM_EPS) * nw_ref[...]).astype(BF16)
    acc = h
    cw = 256
    nj = D_FF // cw
    nslot = hbuf_ref.shape[0]

    def up(j):
        return [_dot(u, wup_ref[:, base + cw * j:base + cw * (j + 1)]) for base in (0, D_FF)]

    hids = up(0)
    for j in range(nj):
        nxt = up(j + 1) if j + 1 < nj else None
        halves = []
        for half, (base, hid) in enumerate(zip((0, D_FF), hids)):
            lo = base + cw * j
            buf = hbuf_ref.at[(2 * j + half) % nslot]
            buf[0:8, :] = carry_ref[:, lo:lo + cw]
            buf[8:8 + tm, :] = hid
            carry_ref[:, lo:lo + cw] = hid[tm - 8:tm, :]
            halves.append(hid * cw_ref[0:1, lo:lo + cw]
                          + buf[7:7 + tm, :] * cw_ref[1:2, lo:lo + cw]
                          + buf[6:6 + tm, :] * cw_ref[2:3, lo:lo + cw]
                          + cb_ref[:, lo:lo + cw])
        gate, val = halves
        act = (gate * jax.nn.sigmoid(gate) * val).astype(BF16)
        acc = acc + _dot(act, wdn_ref[cw * j:cw * (j + 1), :])
        hids = nxt
    o_ref[...] = acc


def _ffn(x2, meta, ya, yb, w_out, norm_w, w_up, conv_w, conv_b, w_down, batch, seq):
    tm = LEAD
    nt = (LEAD + seq) // tm
    nx = seq // tm
    row = lambda b, i: (b * nt + i, 0)
    const = lambda b, i: (0, 0)
    return pl.pallas_call(
        _ffn_kernel,
        grid=(batch, nt),
        in_specs=[
            _x_tile_spec(seq),
            pl.BlockSpec((N_META, D_MODEL), const),
            pl.BlockSpec((tm, RWKV_DIM), row),
            pl.BlockSpec((tm, ATTN_DIM), row),
            pl.BlockSpec((D_MODEL, D_MODEL), const),
            pl.BlockSpec((1, D_MODEL), const),
            pl.BlockSpec((D_MODEL, 2 * D_FF), const),
            pl.BlockSpec((3, 2 * D_FF), const),
            pl.BlockSpec((1, 2 * D_FF), const),
            pl.BlockSpec((D_FF, D_MODEL), const),
        ],
        out_specs=pl.BlockSpec((tm, D_MODEL), lambda b, i: (b * nx + jnp.maximum(i - 1, 0), 0)),
        out_shape=jax.ShapeDtypeStruct(x2.shape, F32),
        scratch_shapes=[pltpu.VMEM((8, 2 * D_FF), F32), pltpu.VMEM((4, tm + 8, 256), F32)],
        compiler_params=pltpu.CompilerParams(
            dimension_semantics=("arbitrary", "arbitrary"), vmem_limit_bytes=VMEM_LIMIT),
    )(x2, meta, ya, yb, w_out, norm_w, w_up, conv_w, conv_b, w_down)


def _row_tile(lp, cap, unit=BLOCK):
    n = lp // unit
    best = 1
    for f in range(1, n + 1):
        if n % f == 0 and f * unit <= cap:
            best = f
    return best * unit


def kernel(x, meta_tokens, rel_bias, norm1_w, w_in, shift_mu, decay_w0, decay_up, aaa_a0, aaa_up, gate_up, k_k, k_a, r_k, lnx_w, lnx_b, q_norm_w, k_norm_w, sinks, w_out, norm2_w, w_up, conv_w, conv_b, w_down):
    batch, seq, _ = x.shape
    assert norm1_w.shape[0] == 1 and seq % LEAD == 0
    lp = LEAD + seq
    layer = 0
    x2 = x.reshape(batch * seq, D_MODEL)
    meta = meta_tokens.astype(x.dtype)
    row2 = lambda t: t.reshape(1, -1)
    zeros64 = jnp.zeros((64, RWKV_DIM), F32)
    dec = jnp.concatenate([decay_up[layer], zeros64], axis=0)
    dec_hi = dec.astype(BF16)
    dec_lo = (dec - dec_hi.astype(F32)).astype(BF16)
    pr, q, kv = _inproj(x2, meta, row2(norm1_w[layer]), w_in[layer].astype(BF16), batch, seq)
    y_rwkv = _rwkv(
        pr.reshape(batch, lp, RWKV_IN), row2(shift_mu[layer]), row2(decay_w0[layer]),
        jnp.concatenate([dec_hi, dec_hi, dec_lo], axis=0), row2(aaa_a0[layer]),
        jnp.concatenate([zeros64, aaa_up[layer]], axis=0).astype(BF16),
        gate_up[layer].astype(BF16), row2(k_k[layer]), row2(k_a[layer]),
        row2(lnx_w[layer]), row2(lnx_b[layer]), row2(r_k[layer]))
    y_attn = _attention(q, kv, rel_bias, sinks[layer].reshape(1, HEADS),
                        jnp.tile(q_norm_w[layer], 2).reshape(1, 128),
                        jnp.tile(k_norm_w[layer], 2).reshape(1, 128), batch, lp,
                        _row_tile(lp, 768))
    out = _ffn(x2, meta, y_rwkv.reshape(batch * lp, RWKV_DIM), y_attn, w_out[layer].astype(BF16),
               row2(norm2_w[layer]), w_up[layer].astype(BF16), conv_w[layer],
               row2(conv_b[layer]), w_down[layer].astype(BF16), batch, seq)
    return out.reshape(batch, seq, D_MODEL)
```

```python
import functools
import math

import jax
import jax.numpy as jnp
from jax import lax
from jax.experimental import pallas as pl
from jax.experimental.pallas import tpu as pltpu

F32 = jnp.float32
BF16 = jnp.bfloat16

D_MODEL = 1024
N_META = 16
BLOCK = 128
LEAD = 2 * BLOCK
PAD = LEAD - N_META
META_BLOCK = 1
HEADS = 8
HEAD_DIM = 64
RWKV_DIM = 512
RWKV_IN = 1792
ATTN_DIM = 512
KV_DIM = 128
IN_DIM = 2560
D_FF = 2816
CHUNK = 64
CHUNKS_PER_STEP = 4
NORM_EPS = 1e-6
GN_EPS = HEAD_DIM * 1e-5
MASK_VALUE = -1e30
REL_BUCKETS = 32
REL_MAX_EXACT = 16
REL_MAX_DIST = 128
VMEM_LIMIT = 56 * 1024 * 1024


def _dot(a, b, precision=None):
    return lax.dot_general(a, b, (((1,), (0,)), ((), ())), precision=precision,
                           preferred_element_type=F32)


def _dot_nt(a, b, precision=None):
    return lax.dot_general(a, b, (((1,), (1,)), ((), ())), precision=precision,
                           preferred_element_type=F32)


def _bf(x):
    return x.astype(BF16)


def _dot_split(x, w_bf16):
    hi = _bf(x)
    lo = _bf(x - hi.astype(F32))
    return _dot(hi, w_bf16) + _dot(lo, w_bf16)


def _head_ones(n, dtype=F32):
    r = lax.broadcasted_iota(jnp.int32, (n, n), 0) // HEAD_DIM
    c = lax.broadcasted_iota(jnp.int32, (n, n), 1) // HEAD_DIM
    return (r == c).astype(dtype)


def _lead_or_x(is_lead, x_ref, meta_ref):
    lead = jnp.concatenate([jnp.zeros((PAD, D_MODEL), F32), meta_ref[...]], axis=0)
    return jnp.where(is_lead, lead, x_ref[...])


def _inproj_kernel(x_ref, meta_ref, nw_ref, w_ref, pr_ref, q_ref, kv_ref):
    x = _lead_or_x(pl.program_id(1) == 0, x_ref, meta_ref)
    ms = jnp.mean(x * x, axis=-1, keepdims=True)
    u = (x * lax.rsqrt(ms + NORM_EPS) * nw_ref[...]).astype(BF16)
    for j in range(IN_DIM // 256):
        c = _dot(u, w_ref[:, 256 * j:256 * (j + 1)])
        lo = 256 * j
        if lo < RWKV_IN:
            pr_ref[:, lo:lo + 256] = c
        elif lo < RWKV_IN + ATTN_DIM:
            q_ref[:, lo - RWKV_IN:lo - RWKV_IN + 256] = c
        else:
            kv_ref[...] = c


def _x_tile_spec(seq):
    nx = seq // LEAD
    return pl.BlockSpec((LEAD, D_MODEL), lambda b, i: (b * nx + jnp.maximum(i - 1, 0), 0))


def _inproj(x2, meta, norm_w, w_in_bf16, batch, seq):
    nt = (LEAD + seq) // LEAD
    tp = batch * nt * LEAD
    row = lambda b, i: (b * nt + i, 0)
    const = lambda b, i: (0, 0)
    return pl.pallas_call(
        _inproj_kernel,
        grid=(batch, nt),
        in_specs=[
            _x_tile_spec(seq),
            pl.BlockSpec((N_META, D_MODEL), const),
            pl.BlockSpec((1, D_MODEL), const),
            pl.BlockSpec((D_MODEL, IN_DIM), const),
        ],
        out_specs=[
            pl.BlockSpec((LEAD, RWKV_IN), row),
            pl.BlockSpec((LEAD, ATTN_DIM), row),
            pl.BlockSpec((LEAD, 2 * KV_DIM), row),
        ],
        out_shape=[
            jax.ShapeDtypeStruct((tp, RWKV_IN), F32),
            jax.ShapeDtypeStruct((tp, ATTN_DIM), F32),
            jax.ShapeDtypeStruct((tp, 2 * KV_DIM), F32),
        ],
        compiler_params=pltpu.CompilerParams(
            dimension_semantics=("arbitrary", "arbitrary"), vmem_limit_bytes=VMEM_LIMIT),
    )(x2, meta, norm_w, w_in_bf16)


def _blockdiag(z, lane_lo):
    zero = jnp.zeros_like(z)
    return jnp.concatenate([jnp.where(lane_lo, z, zero), jnp.where(lane_lo, zero, z)], axis=0)


def _unit_lower_inverse(mats, eye, blk_masks, lane_lo):
    bd = lambda z: _blockdiag(z, lane_lo)
    d = [jnp.where(blk_masks[0], a, 0.0) for a in mats]
    db = [_bf(x) for x in d]
    bdd = [bd(x) for x in db]
    d2 = [_dot(x, y) for x, y in zip(db, bdd)]
    d2b = [_bf(x) for x in d2]
    d34 = [_dot(x, jnp.concatenate([y, bd(x)], axis=1)) for x, y in zip(d2b, bdd)]
    s3 = [eye + x + y + z[:, 0:128] for x, y, z in zip(d, d2, d34)]
    t = [x + _dot(_bf(z[:, 128:256]), bd(_bf(x))) for x, z in zip(s3, d34)]
    for m in blk_masks[1:]:
        tb = [_bf(x) for x in t]
        inner = [_bf(_dot(_bf(jnp.where(m, a, 0.0)), bd(x))) for a, x in zip(mats, tb)]
        t = [x + _dot(xb, bd(i)) for x, xb, i in zip(t, tb, inner)]
    return t


def _rwkv_kernel(p_ref, halo_ref, mu_ref, w0_ref, decup_ref, a0_ref, aup_ref, gup_ref,
                 kk_ref, ka_ref, lnw_ref, lnb_ref, rk_ref, o_ref, s_ref):
    step = pl.program_id(0)
    nb, rows, _ = p_ref.shape
    c = CHUNK

    @pl.when(step == 0)
    def _():
        s_ref[...] = jnp.zeros_like(s_ref)

    ti = lax.broadcasted_iota(jnp.int32, (c, 128), 0)
    lane = lax.broadcasted_iota(jnp.int32, (c, 128), 1)
    si = lane % HEAD_DIM
    lane_lo = lane < HEAD_DIM
    strict = si < ti
    incl = si <= ti
    eye = si == ti
    eye_f = eye.astype(F32)
    blk_masks = [(ti // 8) == (si // 8)]
    size = 8
    while size < c:
        blk_masks.append(((ti // (2 * size)) == (si // (2 * size))) & ((ti // size) != (si // size)))
        size *= 2
    ones = _head_ones(128, BF16)
    row = lax.broadcasted_iota(jnp.int32, (rows, 1), 0)
    bd = lambda z: _blockdiag(z, lane_lo)

    def pair_t(z):
        zt = bd(z).T
        return zt[0:c, :] + zt[c:2 * c, :]

    npair = HEADS // 2

    def head_sums(x, split=False):
        n = x.shape[0]
        xs = jnp.concatenate([x[:, 128 * j:128 * (j + 1)] for j in range(npair)], axis=0)
        ys = _dot_split(xs, ones) if split else _dot(_bf(xs), ones)
        return jnp.concatenate([ys[n * j:n * (j + 1)] for j in range(npair)], axis=1)

    prevs = []
    for b in range(nb):
        prev = pltpu.roll(p_ref[b], shift=1, axis=0)
        prev = jnp.where(row == 0, halo_ref[b, 7:8, :], prev)
        prevs.append(jnp.where((row == 0) & (step == 0), 0.0, prev))
    p = jnp.concatenate([p_ref[b] for b in range(nb)], axis=0)
    ps = p + (jnp.concatenate(prevs, axis=0) - p) * mu_ref[...]
    r = ps[:, 0:512]
    k = ps[:, 512:1024]
    v = ps[:, 1024:1536]
    wa = ps[:, 1536:1664]
    gd = ps[:, 1664:1792]
    th = jnp.tanh(wa)
    th_hi = _bf(th)
    th_lo = _bf(th - th_hi.astype(F32))
    z = w0_ref[...] + _dot(jnp.concatenate([th_hi, th_lo, th_hi], axis=1), decup_ref[...])
    lw = -math.exp(-0.5) * jax.nn.sigmoid(z)
    a = jax.nn.sigmoid(a0_ref[...] + _dot(_bf(wa), aup_ref[...]))
    g = _dot(_bf(jax.nn.sigmoid(gd)), gup_ref[...])
    kk = k * kk_ref[...]
    kn = kk * lax.rsqrt(jnp.maximum(head_sums(kk * kk), 1e-24))
    bb = kn * a
    k = k * (1.0 + (a - 1.0) * ka_ref[...])
    bonus = head_sums(r * k * rk_ref[...]) * v

    rowc = lax.broadcasted_iota(jnp.int32, (c, 1), 0)
    chains = []
    for b in range(nb):
        for ch in range(rows // c):
            rs = slice(c * ch, c * (ch + 1))
            ra = slice(rows * b + c * ch, rows * b + c * (ch + 1))
            lwc = lw[ra]
            cum = lwc
            for sh in (1, 2, 4, 8, 16, 32):
                cum = cum + jnp.where(rowc >= sh, pltpu.roll(cum, shift=sh, axis=0), 0.0)
            cum_last = cum[c - 1:c, :]
            e_neg = jnp.exp(-cum)
            e_end = jnp.exp(cum_last - cum)
            rt = r[ra] * jnp.exp(cum)
            at = -kn[ra] * jnp.exp(cum - lwc)
            kt, bt = k[ra] * e_neg, bb[ra] * e_neg
            kh, bh = k[ra] * e_end, bb[ra] * e_end
            wc = jnp.exp(cum_last)
            for j in range(npair):
                sl = slice(128 * j, 128 * (j + 1))
                chains.append(dict(
                    b=b, ch=ch, j=j, rs=rs, sl=sl, rt=rt[:, sl], at=_bf(at[:, sl]),
                    rhs=jnp.concatenate([bd(_bf(bt[:, sl])), bd(_bf(kt[:, sl]))], axis=0),
                    vbd=bd(_bf(v[ra, sl])),
                    bht=_bf(pair_t(bh[:, sl])), kht=_bf(pair_t(kh[:, sl])),
                    wc=wc[:, sl], bonus=bonus[ra, sl], g=g[ra, sl]))

    for cd in chains:
        cd["pm"] = _dot_nt(jnp.concatenate([cd["at"], _bf(cd["rt"])], axis=0), cd.pop("rhs"))
    for cd in chains:
        pm = cd.pop("pm")
        cd["a_ab"] = jnp.where(strict, pm[0:c, 0:128], 0.0)
        cd["a_ak"] = _bf(jnp.where(strict, pm[0:c, 128:256], 0.0))
        cd["p_rb"] = _bf(jnp.where(incl, pm[c:2 * c, 0:128], 0.0))
        cd["p_rk"] = _bf(jnp.where(incl, pm[c:2 * c, 128:256], 0.0))
    tinv = _unit_lower_inverse([cd.pop("a_ab") for cd in chains], eye_f, blk_masks, lane_lo)
    for cd in chains:
        cd["xv"] = _dot(jnp.concatenate([cd["a_ak"], cd["p_rk"], cd["kht"]], axis=0), cd["vbd"])
    for cd, t in zip(chains, tinv):
        gu = _bf(_dot(_bf(t), jnp.concatenate([bd(cd["at"]), bd(_bf(cd["xv"][0:c]))], axis=1)))
        cd["gu"] = jnp.concatenate([bd(gu[:, 0:128]), bd(gu[:, 128:256])], axis=1)
    for cd in chains:
        r = _dot(jnp.concatenate([cd["p_rb"], cd["bht"]], axis=0), cd.pop("gu"))
        xv = cd.pop("xv")
        cd["y0"] = r[0:c, 128:256] + xv[c:2 * c]
        cd["n_add"] = r[c:2 * c, 128:256] + xv[2 * c:3 * c]
        cd["mq_lhs"] = _bf(jnp.concatenate([r[c:2 * c, 0:128], cd["rt"] + r[0:c, 0:128]], axis=0))
    wcc = _dot_split(jnp.concatenate(
        [jnp.where(eye, jnp.broadcast_to(cd["wc"], (c, 128)), 0.0) for cd in chains], axis=0), ones)
    for i, cd in enumerate(chains):
        cd["wc_col"] = wcc[c * i:c * (i + 1)]

    states = {(b, j): s_ref[b, j] for b in range(nb) for j in range(HEADS // 2)}
    for ch in range(rows // c):
        for cd in chains:
            if cd["ch"] != ch:
                continue
            st = states[(cd["b"], cd["j"])]
            mq = _dot(cd["mq_lhs"], bd(_bf(st)))
            states[(cd["b"], cd["j"])] = cd["wc_col"] * st + mq[0:c] + cd["n_add"]
            cd["y"] = mq[c:2 * c] + cd["y0"]
    for (b, j), st in states.items():
        s_ref[b, j] = st

    ys = jnp.concatenate([cd["y"] for cd in chains], axis=0)
    dy = ys - _dot_split(ys, ones) * (1.0 / HEAD_DIM)
    var = _dot(_bf(dy * dy), ones) * (1.0 / HEAD_DIM)
    zn = dy * lax.rsqrt(var + GN_EPS)
    for i, cd in enumerate(chains):
        sl = cd["sl"]
        yn = zn[c * i:c * (i + 1)] * lnw_ref[:, sl] + lnb_ref[:, sl]
        o_ref[cd["b"], cd["rs"], sl] = (yn + cd["bonus"]) * cd["g"]


def _rwkv(pr3, mu, w0, decup_pad, a0, aup_pad, gup, k_k, k_a, lnx_w, lnx_b, r_k):
    batch, lp, _ = pr3.shape
    rows = CHUNK * CHUNKS_PER_STEP
    const = lambda i: (0, 0)
    vec = pl.BlockSpec((1, RWKV_DIM), const)
    lora = pl.BlockSpec((128, RWKV_DIM), const)
    return pl.pallas_call(
        _rwkv_kernel,
        grid=(lp // rows,),
        in_specs=[
            pl.BlockSpec((batch, rows, RWKV_IN), lambda i: (0, i, 0)),
            pl.BlockSpec((batch, 8, RWKV_IN), lambda i: (0, jnp.maximum(i * (rows // 8) - 1, 0), 0)),
            pl.BlockSpec((1, RWKV_IN), const),
            vec, pl.BlockSpec((3 * 128, RWKV_DIM), const), vec, lora, lora, vec, vec, vec, vec, vec,
        ],
        out_specs=pl.BlockSpec((batch, rows, RWKV_DIM), lambda i: (0, i, 0)),
        out_shape=jax.ShapeDtypeStruct((batch, lp, RWKV_DIM), F32),
        scratch_shapes=[pltpu.VMEM((batch, HEADS // 2, CHUNK, 128), F32)],
        compiler_params=pltpu.CompilerParams(
            dimension_semantics=("arbitrary",), vmem_limit_bytes=VMEM_LIMIT),
    )(pr3, pr3, mu, w0, decup_pad, a0, aup_pad, gup, k_k, k_a, lnx_w, lnx_b, r_k)


def _t5_thresholds():
    n_log = REL_BUCKETS - REL_MAX_EXACT
    out = []
    for k in range(1, n_log):
        x = REL_MAX_EXACT * (REL_MAX_DIST / REL_MAX_EXACT) ** (k / n_log)
        assert min(x - math.floor(x), math.ceil(x) - x) > 1e-3
        out.append(math.ceil(x))
    return out


def _t5_bucket(d):
    d = jnp.maximum(d, 0)
    large = jnp.full(d.shape, REL_MAX_EXACT, jnp.int32)
    for t in _t5_thresholds():
        large = large + (d >= t).astype(jnp.int32)
    return jnp.where(d < REL_MAX_EXACT, d, large)


def _build_bias_tables(rb_ref, tbl_ref):
    q = lax.broadcasted_iota(jnp.int32, (BLOCK, 3 * BLOCK), 0)
    col = lax.broadcasted_iota(jnp.int32, (BLOCK, 3 * BLOCK), 1)
    is_meta = col < BLOCK
    for nn in range(3):
        d_meta = nn * BLOCK + q - col
        d = jnp.where(is_meta, d_meta, q + 2 * BLOCK - col)
        ok = is_meta & (col >= BLOCK - N_META) & (d_meta >= 0)
        if nn >= 2:
            ok = ok | ((col >= BLOCK) & (col < 2 * BLOCK) & (col - BLOCK > q))
        if nn >= 1:
            ok = ok | ((col >= 2 * BLOCK) & (col - 2 * BLOCK <= q))
        bucket = _t5_bucket(d)

        def per_head(h, carry):
            acc = jnp.zeros((BLOCK, 3 * BLOCK), F32)
            for bk in range(REL_BUCKETS):
                acc = jnp.where(bucket == bk, rb_ref[bk, h], acc)
            tbl_ref[nn, h] = jnp.where(ok, acc, MASK_VALUE)
            return carry

        lax.fori_loop(0, HEADS, per_head, 0)


def _attention_stages(first_block, sink_ref, q_ref, kv0_ref, kvp_ref, kvc_ref, qw_ref, kw_ref,
                      tbl_ref, o_ref):
    st = {}

    ones = _head_ones(128, BF16)
    lane_lo = lax.broadcasted_iota(jnp.int32, (BLOCK, 128), 1) < HEAD_DIM
    lane_hi = jnp.logical_not(lane_lo)
    rr = lax.broadcasted_iota(jnp.int32, (128, 128), 0)
    cc = lax.broadcasted_iota(jnp.int32, (128, 128), 1)
    dup = [((rr // HEAD_DIM == g) & (rr % HEAD_DIM == cc % HEAD_DIM)).astype(BF16) for g in range(2)]

    def qk_norm(x, w):
        ms = _dot(_bf(x * x), ones) * (1.0 / HEAD_DIM)
        return x * lax.rsqrt(ms + NORM_EPS) * w

    nbk = q_ref.shape[0] // BLOCK
    grp = HEADS // 2
    chains = [(jb, g) for jb in range(nbk) for g in range(2)]

    def prep():
        qw = qw_ref[...] * HEAD_DIM ** -0.5
        kvb = [kv0_ref[...], kvp_ref[...]] + [kvc_ref[BLOCK * jb:BLOCK * (jb + 1), :]
                                              for jb in range(nbk)]
        kn = [_bf(qk_norm(x[:, 0:128], kw_ref[...])) for x in kvb]
        st["kd"] = [[_bf(_dot(x, dup[g])) for x in kn] for g in range(2)]
        st["vd"] = [[_bf(_dot(_bf(x[:, 128:256]), dup[g])) for x in kvb] for g in range(2)]
        st["qn"] = [[qk_norm(q_ref[BLOCK * jb:BLOCK * (jb + 1), 128 * c4:128 * (c4 + 1)], qw)
                     for c4 in range(4)] for jb in range(nbk)]
        st["sinks"] = [jnp.concatenate([jnp.full((BLOCK, 128), sink_ref[0, grp * g + i], F32)
                                        for i in range(grp)], axis=0) for g in range(2)]

    def scores():
        kd, qn = st.pop("kd"), st.pop("qn")
        s = []
        for jb, g in chains:
            lhs = jnp.concatenate(
                [jnp.where(lane_lo if e == 0 else lane_hi, qn[jb][2 * g + jj], 0.0)
                 for jj in range(2) for e in range(2)], axis=0).astype(BF16)
            keys = jnp.concatenate([kd[g][0], kd[g][1 + jb], kd[g][2 + jb]], axis=0)
            tsel = jnp.clip(first_block + jb - META_BLOCK, 0, 2)
            tb = tbl_ref[tsel, pl.ds(grp * g, grp)].reshape(grp * BLOCK, 3 * BLOCK)
            s.append(_dot_nt(lhs, keys) + tb)
        st["s"] = s

    def row_max():
        mb = []
        for x, (_, g) in zip(st["s"], chains):
            m3 = jnp.maximum(jnp.maximum(x[:, 0:128], x[:, 128:256]), x[:, 256:384])
            m = jnp.maximum(jnp.max(m3, axis=-1, keepdims=True), st["sinks"][g][:, 0:1])
            mb.append(jnp.broadcast_to(m, (grp * BLOCK, 128)))
        st["mb"] = mb

    def exps():
        st["ex"] = [jnp.concatenate([_bf(jnp.exp(x[:, 128 * i:128 * (i + 1)] - mm))
                                     for i in range(3)], axis=1)
                    for x, mm in zip(st.pop("s"), st["mb"])]

    def values():
        ones_cols = jnp.ones((3 * BLOCK, 128), BF16)
        vd, out = st.pop("vd"), []
        for x, mm, (jb, g) in zip(st.pop("ex"), st.pop("mb"), chains):
            vals = jnp.concatenate([vd[g][0], vd[g][1 + jb], vd[g][2 + jb]], axis=0)
            od = _dot(x, jnp.concatenate([vals, ones_cols], axis=1))
            den = od[:, 128:256] + jnp.exp(st["sinks"][g] - mm)
            out.append(od[:, 0:128] * (1.0 / den))
        st["out"] = out

    def store():
        for o, (jb, g) in zip(st.pop("out"), chains):
            for jj in range(2):
                col = 2 * g + jj
                o_ref[BLOCK * jb:BLOCK * (jb + 1), 128 * col:128 * (col + 1)] = jnp.where(
                    lane_lo, o[256 * jj:256 * jj + 128], o[256 * jj + 128:256 * jj + 256])

    return [prep, scores, row_max, exps, values, store]


def _ffn_stages(tile, x_ref, meta_ref, yr_ref, ya_ref, wo_ref, nw_ref, wup_ref, cw_ref, cb_ref,
                wdn_ref, o_ref, carry_ref, hbuf_ref):
    tm = x_ref.shape[0]
    cw = 256
    nj = D_FF // cw
    nslot = hbuf_ref.shape[0]
    st = {}

    def up(j):
        return [_dot(st["u"], wup_ref[:, base + cw * j:base + cw * (j + 1)]) for base in (0, D_FF)]

    def head():
        pos = lax.broadcasted_iota(jnp.int32, (tm, 1), 0) + tile * tm
        h = (_lead_or_x(tile == 0, x_ref, meta_ref)
             + _dot(yr_ref[...].astype(BF16), wo_ref[0:RWKV_DIM, :])
             + _dot(ya_ref[...].astype(BF16), wo_ref[RWKV_DIM:, :]))
        h = jnp.where(pos >= PAD, h, 0.0)
        ms = jnp.mean(h * h, axis=-1, keepdims=True)
        st["u"] = (h * lax.rsqrt(ms + NORM_EPS) * nw_ref[...]).astype(BF16)
        st["acc"] = h
        st["hids"] = up(0)

    def chunk(j):
        hids = st["hids"]
        st["hids"] = up(j + 1) if j + 1 < nj else None
        halves = []
        for half, (base, hid) in enumerate(zip((0, D_FF), hids)):
            lo = base + cw * j
            buf = hbuf_ref.at[(2 * j + half) % nslot]
            buf[0:8, :] = carry_ref[:, lo:lo + cw]
            buf[8:8 + tm, :] = hid
            carry_ref[:, lo:lo + cw] = hid[tm - 8:tm, :]
            halves.append(hid * cw_ref[0:1, lo:lo + cw]
                          + buf[7:7 + tm, :] * cw_ref[1:2, lo:lo + cw]
                          + buf[6:6 + tm, :] * cw_ref[2:3, lo:lo + cw]
                          + cb_ref[:, lo:lo + cw])
        gate, val = halves
        act = (gate * jax.nn.sigmoid(gate) * val).astype(BF16)
        st["acc"] = st["acc"] + _dot(act, wdn_ref[cw * j:cw * (j + 1), :])

    def store():
        o_ref[...] = st.pop("acc")

    return [head] + [lambda j=j: chunk(j) for j in range(nj)] + [store]


_ATTN_AFTER_FFN_STAGE = {1: 0, 2: 1, 4: 2, 6: 3, 8: 4, 12: 5}


def _mix_kernel(nt, ntiles, rb_ref, sink_ref, q_ref, kv0_ref, kvp_ref, kvc_ref, qw_ref, kw_ref,
                x_ref, meta_ref, yr_ref, nw_ref, cw_ref, cb_ref, wo_hbm, wup_hbm, wdn_hbm,
                o_ref, tbl_ref, ya_ref, carry_ref, hbuf_ref, wo_ref, wup_ref, wdn_ref, sem):
    g = pl.program_id(0)
    tile_a = lax.rem(jnp.minimum(g, ntiles - 1), nt)
    tile_f = lax.rem(jnp.maximum(g - 1, 0), nt)

    @pl.when(g == 0)
    def _():
        copies = [pltpu.make_async_copy(src, dst, sem.at[i]) for i, (src, dst) in
                  enumerate(((wo_hbm, wo_ref), (wup_hbm, wup_ref), (wdn_hbm, wdn_ref)))]
        for cp in copies:
            cp.start()
        _build_bias_tables(rb_ref, tbl_ref)
        ya_ref[...] = jnp.zeros_like(ya_ref)
        for cp in copies:
            cp.wait()

    @pl.when(tile_f == 0)
    def _():
        carry_ref[...] = jnp.zeros_like(carry_ref)

    ffn = _ffn_stages(tile_f, x_ref, meta_ref, yr_ref, ya_ref, wo_ref, nw_ref, wup_ref, cw_ref,
                      cb_ref, wdn_ref, o_ref, carry_ref, hbuf_ref)
    attn = _attention_stages(tile_a * (LEAD // BLOCK), sink_ref, q_ref, kv0_ref, kvp_ref, kvc_ref,
                             qw_ref, kw_ref, tbl_ref, ya_ref)
    for i, stage in enumerate(ffn):
        stage()
        if i in _ATTN_AFTER_FFN_STAGE:
            attn[_ATTN_AFTER_FFN_STAGE[i]]()


def _mix(x2, meta, q, kv, y_rwkv, rel_bias, sinks, q_norm_w2, k_norm_w2, w_out, norm_w, w_up,
         conv_w, conv_b, w_down, batch, seq):
    tm = LEAD
    nt = (LEAD + seq) // tm
    nx = seq // tm
    ntiles = batch * nt
    nb = nt * (tm // BLOCK)
    const = lambda g: (0, 0)
    tile_a = lambda g: jnp.minimum(g, ntiles - 1)
    tile_f = lambda g: jnp.maximum(g - 1, 0)
    x_block = lambda g: ((tile_f(g) // nt) * nx + jnp.maximum(tile_f(g) % nt - 1, 0), 0)
    hbm = pl.BlockSpec(memory_space=pl.ANY)
    return pl.pallas_call(
        functools.partial(_mix_kernel, nt, ntiles),
        grid=(ntiles + 1,),
        in_specs=[
            pl.BlockSpec(memory_space=pltpu.SMEM),
            pl.BlockSpec(memory_space=pltpu.SMEM),
            pl.BlockSpec((tm, ATTN_DIM), lambda g: (tile_a(g), 0)),
            pl.BlockSpec((BLOCK, 2 * KV_DIM), lambda g: ((tile_a(g) // nt) * nb + META_BLOCK, 0)),
            pl.BlockSpec((BLOCK, 2 * KV_DIM),
                         lambda g: (jnp.maximum(tile_a(g) * (tm // BLOCK) - 1, 0), 0)),
            pl.BlockSpec((tm, 2 * KV_DIM), lambda g: (tile_a(g), 0)),
            pl.BlockSpec((1, 128), const),
            pl.BlockSpec((1, 128), const),
            pl.BlockSpec((tm, D_MODEL), x_block),
            pl.BlockSpec((N_META, D_MODEL), const),
            pl.BlockSpec((tm, RWKV_DIM), lambda g: (tile_f(g), 0)),
            pl.BlockSpec((1, D_MODEL), const),
            pl.BlockSpec((3, 2 * D_FF), const),
            pl.BlockSpec((1, 2 * D_FF), const),
            hbm, hbm, hbm,
        ],
        out_specs=pl.BlockSpec((tm, D_MODEL), x_block),
        out_shape=jax.ShapeDtypeStruct(x2.shape, F32),
        scratch_shapes=[
            pltpu.VMEM((3, HEADS, BLOCK, 3 * BLOCK), F32),
            pltpu.VMEM((tm, ATTN_DIM), F32),
            pltpu.VMEM((8, 2 * D_FF), F32),
            pltpu.VMEM((4, tm + 8, 256), F32),
            pltpu.VMEM(w_out.shape, BF16),
            pltpu.VMEM(w_up.shape, BF16),
            pltpu.VMEM(w_down.shape, BF16),
            pltpu.SemaphoreType.DMA((3,)),
        ],
        compiler_params=pltpu.CompilerParams(
            dimension_semantics=("arbitrary",), vmem_limit_bytes=VMEM_LIMIT),
    )(rel_bias, sinks, q, kv, kv, kv, q_norm_w2, k_norm_w2, x2, meta, y_rwkv, norm_w, conv_w,
      conv_b, w_out, w_up, w_down)


def _row_tile(lp, cap, unit=BLOCK):
    n = lp // unit
    best = 1
    for f in range(1, n + 1):
        if n % f == 0 and f * unit <= cap:
            best = f
    return best * unit


def kernel(x, meta_tokens, rel_bias, norm1_w, w_in, shift_mu, decay_w0, decay_up, aaa_a0, aaa_up, gate_up, k_k, k_a, r_k, lnx_w, lnx_b, q_norm_w, k_norm_w, sinks, w_out, norm2_w, w_up, conv_w, conv_b, w_down):
    batch, seq, _ = x.shape
    assert norm1_w.shape[0] == 1 and seq % LEAD == 0
    lp = LEAD + seq
    layer = 0
    x2 = x.reshape(batch * seq, D_MODEL)
    meta = meta_tokens.astype(x.dtype)
    row2 = lambda t: t.reshape(1, -1)
    zeros64 = jnp.zeros((64, RWKV_DIM), F32)
    dec = jnp.concatenate([decay_up[layer], zeros64], axis=0)
    dec_hi = dec.astype(BF16)
    dec_lo = (dec - dec_hi.astype(F32)).astype(BF16)
    pr, q, kv = _inproj(x2, meta, row2(norm1_w[layer]), w_in[layer].astype(BF16), batch, seq)
    y_rwkv = _rwkv(
        pr.reshape(batch, lp, RWKV_IN), row2(shift_mu[layer]), row2(decay_w0[layer]),
        jnp.concatenate([dec_hi, dec_hi, dec_lo], axis=0), row2(aaa_a0[layer]),
        jnp.concatenate([zeros64, aaa_up[layer]], axis=0).astype(BF16),
        gate_up[layer].astype(BF16), row2(k_k[layer]), row2(k_a[layer]),
        row2(lnx_w[layer]), row2(lnx_b[layer]), row2(r_k[layer]))
    out = _mix(x2, meta, q, kv, y_rwkv.reshape(batch * lp, RWKV_DIM), rel_bias,
               sinks[layer].reshape(1, HEADS), jnp.tile(q_norm_w[layer], 2).reshape(1, 128),
               jnp.tile(k_norm_w[layer], 2).reshape(1, 128), w_out[layer].astype(BF16),
               row2(norm2_w[layer]), w_up[layer].astype(BF16), conv_w[layer],
               row2(conv_b[layer]), w_down[layer].astype(BF16), batch, seq)
    return out.reshape(batch, seq, D_MODEL)
```

```python
import math

import jax
import jax.numpy as jnp
from jax import lax
from jax.experimental import pallas as pl
from jax.experimental.pallas import tpu as pltpu

F32 = jnp.float32
BF16 = jnp.bfloat16

D_MODEL = 1024
N_META = 16
BLOCK = 128
LEAD = 2 * BLOCK
PAD = LEAD - N_META
META_BLOCK = 1
HEADS = 8
HEAD_DIM = 64
RWKV_DIM = 512
RWKV_IN = 1792
ATTN_DIM = 512
KV_DIM = 128
IN_DIM = 2560
D_FF = 2816
CHUNK = 64
CHUNKS_PER_STEP = 4
NORM_EPS = 1e-6
GN_EPS = HEAD_DIM * 1e-5
MASK_VALUE = -1e30
REL_BUCKETS = 32
REL_MAX_EXACT = 16
REL_MAX_DIST = 128
VMEM_LIMIT = 56 * 1024 * 1024


def _dot(a, b, precision=None):
    return lax.dot_general(a, b, (((1,), (0,)), ((), ())), precision=precision,
                           preferred_element_type=F32)


def _dot_nt(a, b, precision=None):
    return lax.dot_general(a, b, (((1,), (1,)), ((), ())), precision=precision,
                           preferred_element_type=F32)


def _bf(x):
    return x.astype(BF16)


def _dot_split(x, w_bf16):
    hi = _bf(x)
    lo = _bf(x - hi.astype(F32))
    return _dot(hi, w_bf16) + _dot(lo, w_bf16)


def _head_ones(n, dtype=F32):
    r = lax.broadcasted_iota(jnp.int32, (n, n), 0) // HEAD_DIM
    c = lax.broadcasted_iota(jnp.int32, (n, n), 1) // HEAD_DIM
    return (r == c).astype(dtype)


def _lead_or_x(is_lead, x_ref, meta_ref):
    lead = jnp.concatenate([jnp.zeros((PAD, D_MODEL), F32), meta_ref[...]], axis=0)
    return jnp.where(is_lead, lead, x_ref[...])


def _inproj_kernel(*refs):
    nsub = len(refs) - 6
    x_refs, (meta_ref, nw_ref, w_ref, pr_ref, q_ref, kv_ref) = refs[:nsub], refs[nsub:]
    first = pl.program_id(1) == 0
    x = jnp.concatenate([_lead_or_x(first if k == 0 else False, x_refs[k], meta_ref)
                         for k in range(nsub)], axis=0)
    ms = jnp.mean(x * x, axis=-1, keepdims=True)
    u = (x * lax.rsqrt(ms + NORM_EPS) * nw_ref[...]).astype(BF16)
    for j in range(IN_DIM // 256):
        c = _dot(u, w_ref[:, 256 * j:256 * (j + 1)])
        lo = 256 * j
        if lo < RWKV_IN:
            pr_ref[:, lo:lo + 256] = c
        elif lo < RWKV_IN + ATTN_DIM:
            q_ref[:, lo - RWKV_IN:lo - RWKV_IN + 256] = c
        else:
            kv_ref[...] = c


def _x_tile_spec(seq, nsub=1, k=0):
    nx = seq // LEAD
    return pl.BlockSpec((LEAD, D_MODEL),
                        lambda b, i: (b * nx + jnp.maximum(nsub * i + k - 1, 0), 0))


def _inproj(x2, meta, norm_w, w_in_bf16, batch, seq):
    nlead = (LEAD + seq) // LEAD
    nsub = max(m for m in (3, 2, 1) if nlead % m == 0)
    tm = nsub * LEAD
    nt = nlead // nsub
    tp = batch * nlead * LEAD
    row = lambda b, i: (b * nt + i, 0)
    const = lambda b, i: (0, 0)
    return pl.pallas_call(
        _inproj_kernel,
        grid=(batch, nt),
        in_specs=[_x_tile_spec(seq, nsub, k) for k in range(nsub)] + [
            pl.BlockSpec((N_META, D_MODEL), const),
            pl.BlockSpec((1, D_MODEL), const),
            pl.BlockSpec((D_MODEL, IN_DIM), const),
        ],
        out_specs=[
            pl.BlockSpec((tm, RWKV_IN), row),
            pl.BlockSpec((tm, ATTN_DIM), row),
            pl.BlockSpec((tm, 2 * KV_DIM), row),
        ],
        out_shape=[
            jax.ShapeDtypeStruct((tp, RWKV_IN), F32),
            jax.ShapeDtypeStruct((tp, ATTN_DIM), F32),
            jax.ShapeDtypeStruct((tp, 2 * KV_DIM), F32),
        ],
        compiler_params=pltpu.CompilerParams(
            dimension_semantics=("arbitrary", "arbitrary"), vmem_limit_bytes=VMEM_LIMIT),
    )(*([x2] * nsub), meta, norm_w, w_in_bf16)


def _blockdiag(z, lane_lo):
    zero = jnp.zeros_like(z)
    return jnp.concatenate([jnp.where(lane_lo, z, zero), jnp.where(lane_lo, zero, z)], axis=0)


def _rwkv_kernel(p_ref, halo_ref, mu_ref, w0_ref, decup_ref, a0_ref, aup_ref, gup_ref,
                 kk_ref, ka_ref, lnw_ref, lnb_ref, rk_ref, o_ref, s_ref):
    step = pl.program_id(0)
    nb, rows, _ = p_ref.shape
    c = CHUNK

    @pl.when(step == 0)
    def _():
        s_ref[...] = jnp.zeros_like(s_ref)

    ti = lax.broadcasted_iota(jnp.int32, (c, 128), 0)
    lane = lax.broadcasted_iota(jnp.int32, (c, 128), 1)
    si = lane % HEAD_DIM
    lane_lo = lane < HEAD_DIM
    strict = si < ti
    incl = si <= ti
    eye = si == ti
    eye_f = eye.astype(F32)
    blk_masks = [(ti // 8) == (si // 8)]
    size = 8
    while size < c:
        blk_masks.append(((ti // (2 * size)) == (si // (2 * size))) & ((ti // size) != (si // size)))
        size *= 2
    ones = _head_ones(128, BF16)
    row = lax.broadcasted_iota(jnp.int32, (rows, 1), 0)
    bd = lambda z: _blockdiag(z, lane_lo)

    def pair_t(z):
        zt = bd(z).T
        return zt[0:c, :] + zt[c:2 * c, :]

    npair = HEADS // 2

    def head_sums(x, split=False):
        n = x.shape[0]
        xs = jnp.concatenate([x[:, 128 * j:128 * (j + 1)] for j in range(npair)], axis=0)
        ys = _dot_split(xs, ones) if split else _dot(_bf(xs), ones)
        return jnp.concatenate([ys[n * j:n * (j + 1)] for j in range(npair)], axis=1)

    prevs = []
    for b in range(nb):
        prev = pltpu.roll(p_ref[b], shift=1, axis=0)
        prev = jnp.where(row == 0, halo_ref[b, 7:8, :], prev)
        prevs.append(jnp.where((row == 0) & (step == 0), 0.0, prev))
    p = jnp.concatenate([p_ref[b] for b in range(nb)], axis=0)
    ps = p + (jnp.concatenate(prevs, axis=0) - p) * mu_ref[...]
    r = ps[:, 0:512]
    k = ps[:, 512:1024]
    v = ps[:, 1024:1536]
    wa = ps[:, 1536:1664]
    gd = ps[:, 1664:1792]
    th = jnp.tanh(wa)
    th_hi = _bf(th)
    th_lo = _bf(th - th_hi.astype(F32))
    z = w0_ref[...] + _dot(jnp.concatenate([th_hi, th_lo, th_hi], axis=1), decup_ref[...])
    lw = -math.exp(-0.5) * jax.nn.sigmoid(z)
    a = jax.nn.sigmoid(a0_ref[...] + _dot(_bf(wa), aup_ref[...]))
    g = _dot(_bf(jax.nn.sigmoid(gd)), gup_ref[...])
    kk = k * kk_ref[...]
    kn = kk * lax.rsqrt(jnp.maximum(head_sums(kk * kk), 1e-24))
    bb = kn * a
    k = k * (1.0 + (a - 1.0) * ka_ref[...])
    bonus = head_sums(r * k * rk_ref[...]) * v

    rowc = lax.broadcasted_iota(jnp.int32, (c, 1), 0)
    chains = []
    for b in range(nb):
        for ch in range(rows // c):
            rs = slice(c * ch, c * (ch + 1))
            ra = slice(rows * b + c * ch, rows * b + c * (ch + 1))
            lwc = lw[ra]
            cum = lwc
            for sh in (1, 2, 4, 8, 16, 32):
                cum = cum + jnp.where(rowc >= sh, pltpu.roll(cum, shift=sh, axis=0), 0.0)
            cum_last = cum[c - 1:c, :]
            e_neg = jnp.exp(-cum)
            e_end = jnp.exp(cum_last - cum)
            rt = r[ra] * jnp.exp(cum)
            at = -kn[ra] * jnp.exp(cum - lwc)
            kt, bt = k[ra] * e_neg, bb[ra] * e_neg
            kh, bh = k[ra] * e_end, bb[ra] * e_end
            wc = jnp.exp(cum_last)
            for j in range(npair):
                sl = slice(128 * j, 128 * (j + 1))
                chains.append(dict(
                    b=b, ch=ch, j=j, rs=rs, sl=sl, rt=rt[:, sl], at=_bf(at[:, sl]),
                    rhs=jnp.concatenate([bd(_bf(bt[:, sl])), bd(_bf(kt[:, sl]))], axis=0),
                    vbd=bd(_bf(v[ra, sl])),
                    bht=_bf(pair_t(bh[:, sl])), kht=_bf(pair_t(kh[:, sl])),
                    wc=wc[:, sl], bonus=bonus[ra, sl], g=g[ra, sl]))

    def s_products(cd):
        cd["pm"] = _dot_nt(jnp.concatenate([cd["at"], _bf(cd["rt"])], axis=0), cd.pop("rhs"))

    def s_masks(cd):
        pm = cd.pop("pm")
        cd["a_ab"] = jnp.where(strict, pm[0:c, 0:128], 0.0)
        cd["p_rb"] = _bf(jnp.where(incl, pm[c:2 * c, 0:128], 0.0))
        a_ak = _bf(jnp.where(strict, pm[0:c, 128:256], 0.0))
        p_rk = _bf(jnp.where(incl, pm[c:2 * c, 128:256], 0.0))
        cd["xv"] = _dot(jnp.concatenate([a_ak, p_rk, cd.pop("kht")], axis=0), cd.pop("vbd"))

    def s_d2(cd):
        cd["d"] = jnp.where(blk_masks[0], cd["a_ab"], 0.0)
        db = _bf(cd["d"])
        cd["bdd"] = bd(db)
        cd["d2"] = _dot(db, cd["bdd"])

    def s_d34(cd):
        d2b = _bf(cd["d2"])
        cd["d34"] = _dot(d2b, jnp.concatenate([cd.pop("bdd"), bd(d2b)], axis=1))

    def s_t0(cd):
        d34 = cd.pop("d34")
        s3 = eye_f + cd.pop("d") + cd.pop("d2") + d34[:, 0:128]
        cd["t"] = s3 + _dot(_bf(d34[:, 128:256]), bd(_bf(s3)))

    def s_inner(level):
        def run(cd):
            cd["tb"] = _bf(cd["t"])
            cd["inner"] = _bf(_dot(_bf(jnp.where(blk_masks[level], cd["a_ab"], 0.0)), bd(cd["tb"])))
        return run

    def s_merge(cd):
        cd["t"] = cd["t"] + _dot(cd.pop("tb"), bd(cd.pop("inner")))

    def s_gu(cd):
        cd.pop("a_ab")
        gu = _bf(_dot(_bf(cd.pop("t")),
                      jnp.concatenate([bd(cd.pop("at")), bd(_bf(cd["xv"][0:c]))], axis=1)))
        cd["gu"] = jnp.concatenate([bd(gu[:, 0:128]), bd(gu[:, 128:256])], axis=1)

    def s_maps(cd):
        r = _dot(jnp.concatenate([cd.pop("p_rb"), cd.pop("bht")], axis=0), cd.pop("gu"))
        xv = cd.pop("xv")
        cd["y0"] = r[0:c, 128:256] + xv[c:2 * c]
        cd["n_add"] = r[c:2 * c, 128:256] + xv[2 * c:3 * c]
        cd["mq_lhs"] = _bf(jnp.concatenate([r[c:2 * c, 0:128], cd.pop("rt") + r[0:c, 0:128]], axis=0))

    stages = [s_products, s_masks, s_d2, s_d34, s_t0]
    for level in range(1, len(blk_masks)):
        stages += [s_inner(level), s_merge]
    stages += [s_gu, s_maps]
    for stage in stages:
        for cd in chains:
            stage(cd)
    wcc = _dot_split(jnp.concatenate(
        [jnp.where(eye, jnp.broadcast_to(cd["wc"], (c, 128)), 0.0) for cd in chains], axis=0), ones)
    for i, cd in enumerate(chains):
        cd["wc_col"] = wcc[c * i:c * (i + 1)]

    states = {(b, j): s_ref[b, j] for b in range(nb) for j in range(HEADS // 2)}
    for ch in range(rows // c):
        for cd in chains:
            if cd["ch"] != ch:
                continue
            st = states[(cd["b"], cd["j"])]
            mq = _dot(cd["mq_lhs"], bd(_bf(st)))
            states[(cd["b"], cd["j"])] = cd["wc_col"] * st + mq[0:c] + cd["n_add"]
            cd["y"] = mq[c:2 * c] + cd["y0"]
    for (b, j), st in states.items():
        s_ref[b, j] = st

    ys = jnp.concatenate([cd["y"] for cd in chains], axis=0)
    dy = ys - _dot_split(ys, ones) * (1.0 / HEAD_DIM)
    var = _dot(_bf(dy * dy), ones) * (1.0 / HEAD_DIM)
    zn = dy * lax.rsqrt(var + GN_EPS)
    for i, cd in enumerate(chains):
        sl = cd["sl"]
        yn = zn[c * i:c * (i + 1)] * lnw_ref[:, sl] + lnb_ref[:, sl]
        o_ref[cd["b"], cd["rs"], sl] = (yn + cd["bonus"]) * cd["g"]


def _rwkv(pr3, mu, w0, decup_pad, a0, aup_pad, gup, k_k, k_a, lnx_w, lnx_b, r_k):
    batch, lp, _ = pr3.shape
    rows = CHUNK * CHUNKS_PER_STEP
    const = lambda i: (0, 0)
    vec = pl.BlockSpec((1, RWKV_DIM), const)
    lora = pl.BlockSpec((128, RWKV_DIM), const)
    return pl.pallas_call(
        _rwkv_kernel,
        grid=(lp // rows,),
        in_specs=[
            pl.BlockSpec((batch, rows, RWKV_IN), lambda i: (0, i, 0)),
            pl.BlockSpec((batch, 8, RWKV_IN), lambda i: (0, jnp.maximum(i * (rows // 8) - 1, 0), 0)),
            pl.BlockSpec((1, RWKV_IN), const),
            vec, pl.BlockSpec((3 * 128, RWKV_DIM), const), vec, lora, lora, vec, vec, vec, vec, vec,
        ],
        out_specs=pl.BlockSpec((batch, rows, RWKV_DIM), lambda i: (0, i, 0)),
        out_shape=jax.ShapeDtypeStruct((batch, lp, RWKV_DIM), F32),
        scratch_shapes=[pltpu.VMEM((batch, HEADS // 2, CHUNK, 128), F32)],
        compiler_params=pltpu.CompilerParams(
            dimension_semantics=("arbitrary",), vmem_limit_bytes=VMEM_LIMIT),
    )(pr3, pr3, mu, w0, decup_pad, a0, aup_pad, gup, k_k, k_a, lnx_w, lnx_b, r_k)


def _t5_thresholds():
    n_log = REL_BUCKETS - REL_MAX_EXACT
    out = []
    for k in range(1, n_log):
        x = REL_MAX_EXACT * (REL_MAX_DIST / REL_MAX_EXACT) ** (k / n_log)
        assert min(x - math.floor(x), math.ceil(x) - x) > 1e-3
        out.append(math.ceil(x))
    return out


def _t5_bucket(d):
    d = jnp.maximum(d, 0)
    large = jnp.full(d.shape, REL_MAX_EXACT, jnp.int32)
    for t in _t5_thresholds():
        large = large + (d >= t).astype(jnp.int32)
    return jnp.where(d < REL_MAX_EXACT, d, large)


def _attn_kernel(rb_ref, sink_ref, q_ref, kv0_ref, kvp_ref, kvc_ref, qw_ref, kw_ref, o_ref, tbl_ref):
    n = pl.program_id(1)

    @pl.when((pl.program_id(0) == 0) & (n == 0))
    def _():
        q = lax.broadcasted_iota(jnp.int32, (BLOCK, 3 * BLOCK), 0)
        col = lax.broadcasted_iota(jnp.int32, (BLOCK, 3 * BLOCK), 1)
        is_meta = col < BLOCK
        for nn in range(3):
            d_meta = nn * BLOCK + q - col
            d = jnp.where(is_meta, d_meta, q + 2 * BLOCK - col)
            ok = is_meta & (col >= BLOCK - N_META) & (d_meta >= 0)
            if nn >= 2:
                ok = ok | ((col >= BLOCK) & (col < 2 * BLOCK) & (col - BLOCK > q))
            if nn >= 1:
                ok = ok | ((col >= 2 * BLOCK) & (col - 2 * BLOCK <= q))
            bucket = _t5_bucket(d)

            def per_head(h, carry):
                acc = jnp.zeros((BLOCK, 3 * BLOCK), F32)
                for bk in range(REL_BUCKETS):
                    acc = jnp.where(bucket == bk, rb_ref[bk, h], acc)
                tbl_ref[nn, h] = jnp.where(ok, acc, MASK_VALUE)
                return carry

            lax.fori_loop(0, HEADS, per_head, 0)

    ones = _head_ones(128, BF16)
    lane_lo = lax.broadcasted_iota(jnp.int32, (BLOCK, 128), 1) < HEAD_DIM
    lane_hi = jnp.logical_not(lane_lo)
    rr = lax.broadcasted_iota(jnp.int32, (128, 128), 0)
    cc = lax.broadcasted_iota(jnp.int32, (128, 128), 1)
    dup = [((rr // HEAD_DIM == g) & (rr % HEAD_DIM == cc % HEAD_DIM)).astype(BF16) for g in range(2)]

    def qk_norm(x, w):
        ms = _dot(_bf(x * x), ones) * (1.0 / HEAD_DIM)
        return x * lax.rsqrt(ms + NORM_EPS) * w

    nbk = q_ref.shape[0] // BLOCK
    grp = HEADS // 2
    qw = qw_ref[...] * HEAD_DIM ** -0.5

    kvb = [kv0_ref[...], kvp_ref[...]] + [kvc_ref[BLOCK * jb:BLOCK * (jb + 1), :] for jb in range(nbk)]
    kn = [_bf(qk_norm(x[:, 0:128], kw_ref[...])) for x in kvb]
    kd = [[_bf(_dot(x, dup[g])) for x in kn] for g in range(2)]
    vd = [[_bf(_dot(_bf(x[:, 128:256]), dup[g])) for x in kvb] for g in range(2)]
    qn = [[qk_norm(q_ref[BLOCK * jb:BLOCK * (jb + 1), 128 * c4:128 * (c4 + 1)], qw)
           for c4 in range(4)] for jb in range(nbk)]
    sinks = [jnp.concatenate([jnp.full((BLOCK, 128), sink_ref[0, grp * g + i], F32)
                              for i in range(grp)], axis=0) for g in range(2)]
    ones_cols = jnp.ones((3 * BLOCK, 128), BF16)

    def scores(c):
        jb, g = c["jb"], c["g"]
        lhs = jnp.concatenate(
            [jnp.where(lane_lo if e == 0 else lane_hi, qn[jb][2 * g + jj], 0.0)
             for jj in range(2) for e in range(2)], axis=0).astype(BF16)
        keys = jnp.concatenate([kd[g][0], kd[g][1 + jb], kd[g][2 + jb]], axis=0)
        tsel = jnp.clip(n * nbk + jb - META_BLOCK, 0, 2)
        tb = tbl_ref[tsel, pl.ds(grp * g, grp)].reshape(grp * BLOCK, 3 * BLOCK)
        c["s"] = _dot_nt(lhs, keys) + tb

    def row_max(c):
        x = c["s"]
        m3 = jnp.maximum(jnp.maximum(x[:, 0:128], x[:, 128:256]), x[:, 256:384])
        m = jnp.maximum(jnp.max(m3, axis=-1, keepdims=True), sinks[c["g"]][:, 0:1])
        c["mb"] = jnp.broadcast_to(m, (grp * BLOCK, 128))

    def exps(c):
        x, mm = c.pop("s"), c["mb"]
        c["ex"] = jnp.concatenate(
            [_bf(jnp.exp(x[:, 128 * i:128 * (i + 1)] - mm)) for i in range(3)], axis=1)

    def values(c):
        jb, g = c["jb"], c["g"]
        vals = jnp.concatenate([vd[g][0], vd[g][1 + jb], vd[g][2 + jb]], axis=0)
        od = _dot(c.pop("ex"), jnp.concatenate([vals, ones_cols], axis=1))
        den = od[:, 128:256] + jnp.exp(sinks[g] - c.pop("mb"))
        o = od[:, 0:128] * (1.0 / den)
        for jj in range(2):
            col = 2 * g + jj
            o_ref[BLOCK * jb:BLOCK * (jb + 1), 128 * col:128 * (col + 1)] = jnp.where(
                lane_lo, o[256 * jj:256 * jj + 128], o[256 * jj + 128:256 * jj + 256])

    stages = (scores, row_max, exps, values)
    chains = [dict(jb=jb, g=g) for jb in range(nbk) for g in range(2)]
    for w in range(len(chains) + len(stages) - 1):
        for ci, c in enumerate(chains):
            if 0 <= w - ci < len(stages):
                stages[w - ci](c)


def _attention(q, kv, rel_bias, sinks, q_norm_w2, k_norm_w2, batch, lp, tm):
    nb = lp // BLOCK
    nt = lp // tm
    nbk = tm // BLOCK
    return pl.pallas_call(
        _attn_kernel,
        grid=(batch, nt),
        in_specs=[
            pl.BlockSpec(memory_space=pltpu.SMEM),
            pl.BlockSpec(memory_space=pltpu.SMEM),
            pl.BlockSpec((tm, ATTN_DIM), lambda b, n: (b * nt + n, 0)),
            pl.BlockSpec((BLOCK, 2 * KV_DIM), lambda b, n: (b * nb + META_BLOCK, 0)),
            pl.BlockSpec((BLOCK, 2 * KV_DIM),
                         lambda b, n: (b * nb + jnp.maximum(n * nbk - 1, 0), 0)),
            pl.BlockSpec((tm, 2 * KV_DIM), lambda b, n: (b * nt + n, 0)),
            pl.BlockSpec((1, 128), lambda b, n: (0, 0)),
            pl.BlockSpec((1, 128), lambda b, n: (0, 0)),
        ],
        out_specs=pl.BlockSpec((tm, ATTN_DIM), lambda b, n: (b * nt + n, 0)),
        out_shape=jax.ShapeDtypeStruct(q.shape, F32),
        scratch_shapes=[pltpu.VMEM((3, HEADS, BLOCK, 3 * BLOCK), F32)],
        compiler_params=pltpu.CompilerParams(
            dimension_semantics=("arbitrary", "arbitrary"), vmem_limit_bytes=VMEM_LIMIT),
    )(rel_bias, sinks, q, kv, kv, kv, q_norm_w2, k_norm_w2)


def _ffn_kernel(x_ref, meta_ref, ya_ref, yb_ref, wo_ref, nw_ref, wup_ref, cw_ref, cb_ref, wdn_ref,
                o_ref, carry_ref, hbuf_ref):
    tm = x_ref.shape[0]
    pos = lax.broadcasted_iota(jnp.int32, (tm, 1), 0) + pl.program_id(1) * tm

    @pl.when(pl.program_id(1) == 0)
    def _():
        carry_ref[...] = jnp.zeros_like(carry_ref)

    h = (_lead_or_x(pl.program_id(1) == 0, x_ref, meta_ref) + _dot(ya_ref[...].astype(BF16), wo_ref[0:RWKV_DIM, :])
         + _dot(yb_ref[...].astype(BF16), wo_ref[RWKV_DIM:, :]))
    h = jnp.where(pos >= PAD, h, 0.0)
    ms = jnp.mean(h * h, axis=-1, keepdims=True)
    u = (h * lax.rsqrt(ms + NORM_EPS) * nw_ref[...]).astype(BF16)
    acc = h
    cw = 256
    nj = D_FF // cw
    nslot = hbuf_ref.shape[0]

    def up(j):
        return [_dot(u, wup_ref[:, base + cw * j:base + cw * (j + 1)]) for base in (0, D_FF)]

    hids = up(0)
    for j in range(nj):
        nxt = up(j + 1) if j + 1 < nj else None
        halves = []
        for half, (base, hid) in enumerate(zip((0, D_FF), hids)):
            lo = base + cw * j
            buf = hbuf_ref.at[(2 * j + half) % nslot]
            buf[0:8, :] = carry_ref[:, lo:lo + cw]
            buf[8:8 + tm, :] = hid
            carry_ref[:, lo:lo + cw] = hid[tm - 8:tm, :]
            halves.append(hid * cw_ref[0:1, lo:lo + cw]
                          + buf[7:7 + tm, :] * cw_ref[1:2, lo:lo + cw]
                          + buf[6:6 + tm, :] * cw_ref[2:3, lo:lo + cw]
                          + cb_ref[:, lo:lo + cw])
        gate, val = halves
        act = (gate * jax.nn.sigmoid(gate) * val).astype(BF16)
        acc = acc + _dot(act, wdn_ref[cw * j:cw * (j + 1), :])
        hids = nxt
    o_ref[...] = acc


def _ffn(x2, meta, ya, yb, w_out, norm_w, w_up, conv_w, conv_b, w_down, batch, seq):
    tm = LEAD
    nt = (LEAD + seq) // tm
    nx = seq // tm
    row = lambda b, i: (b * nt + i, 0)
    const = lambda b, i: (0, 0)
    return pl.pallas_call(
        _ffn_kernel,
        grid=(batch, nt),
        in_specs=[
            _x_tile_spec(seq),
            pl.BlockSpec((N_META, D_MODEL), const),
            pl.BlockSpec((tm, RWKV_DIM), row),
            pl.BlockSpec((tm, ATTN_DIM), row),
            pl.BlockSpec((D_MODEL, D_MODEL), const),
            pl.BlockSpec((1, D_MODEL), const),
            pl.BlockSpec((D_MODEL, 2 * D_FF), const),
            pl.BlockSpec((3, 2 * D_FF), const),
            pl.BlockSpec((1, 2 * D_FF), const),
            pl.BlockSpec((D_FF, D_MODEL), const),
        ],
        out_specs=pl.BlockSpec((tm, D_MODEL), lambda b, i: (b * nx + jnp.maximum(i - 1, 0), 0)),
        out_shape=jax.ShapeDtypeStruct(x2.shape, F32),
        scratch_shapes=[pltpu.VMEM((8, 2 * D_FF), F32), pltpu.VMEM((4, tm + 8, 256), F32)],
        compiler_params=pltpu.CompilerParams(
            dimension_semantics=("arbitrary", "arbitrary"), vmem_limit_bytes=VMEM_LIMIT),
    )(x2, meta, ya, yb, w_out, norm_w, w_up, conv_w, conv_b, w_down)


def _row_tile(lp, cap, unit=BLOCK):
    n = lp // unit
    best = 1
    for f in range(1, n + 1):
        if n % f == 0 and f * unit <= cap:
            best = f
    return best * unit


def kernel(x, meta_tokens, rel_bias, norm1_w, w_in, shift_mu, decay_w0, decay_up, aaa_a0, aaa_up, gate_up, k_k, k_a, r_k, lnx_w, lnx_b, q_norm_w, k_norm_w, sinks, w_out, norm2_w, w_up, conv_w, conv_b, w_down):
    batch, seq, _ = x.shape
    assert norm1_w.shape[0] == 1 and seq % LEAD == 0
    lp = LEAD + seq
    layer = 0
    x2 = x.reshape(batch * seq, D_MODEL)
    meta = meta_tokens.astype(x.dtype)
    row2 = lambda t: t.reshape(1, -1)
    zeros64 = jnp.zeros((64, RWKV_DIM), F32)
    dec = jnp.concatenate([decay_up[layer], zeros64], axis=0)
    dec_hi = dec.astype(BF16)
    dec_lo = (dec - dec_hi.astype(F32)).astype(BF16)
    pr, q, kv = _inproj(x2, meta, row2(norm1_w[layer]), w_in[layer].astype(BF16), batch, seq)
    y_rwkv = _rwkv(
        pr.reshape(batch, lp, RWKV_IN), row2(shift_mu[layer]), row2(decay_w0[layer]),
        jnp.concatenate([dec_hi, dec_hi, dec_lo], axis=0), row2(aaa_a0[layer]),
        jnp.concatenate([zeros64, aaa_up[layer]], axis=0).astype(BF16),
        gate_up[layer].astype(BF16), row2(k_k[layer]), row2(k_a[layer]),
        row2(lnx_w[layer]), row2(lnx_b[layer]), row2(r_k[layer]))
    y_attn = _attention(q, kv, rel_bias, sinks[layer].reshape(1, HEADS),
                        jnp.tile(q_norm_w[layer], 2).reshape(1, 128),
                        jnp.tile(k_norm_w[layer], 2).reshape(1, 128), batch, lp,
                        _row_tile(lp, 768))
    out = _ffn(x2, meta, y_rwkv.reshape(batch * lp, RWKV_DIM), y_attn, w_out[layer].astype(BF16),
               row2(norm2_w[layer]), w_up[layer].astype(BF16), conv_w[layer],
               row2(conv_b[layer]), w_down[layer].astype(BF16), batch, seq)
    return out.reshape(batch, seq, D_MODEL)
```

```python
import math

import jax
import jax.numpy as jnp
from jax import lax
from jax.experimental import pallas as pl
from jax.experimental.pallas import tpu as pltpu

F32 = jnp.float32
BF16 = jnp.bfloat16

D_MODEL = 1024
N_META = 16
BLOCK = 128
LEAD = 2 * BLOCK
PAD = LEAD - N_META
META_BLOCK = 1
HEADS = 8
HEAD_DIM = 64
RWKV_DIM = 512
RWKV_IN = 1792
ATTN_DIM = 512
KV_DIM = 128
IN_DIM = 2560
D_FF = 2816
CHUNK = 64
CHUNKS_PER_STEP = 4
NORM_EPS = 1e-6
GN_EPS = HEAD_DIM * 1e-5
MASK_VALUE = -1e30
REL_BUCKETS = 32
REL_MAX_EXACT = 16
REL_MAX_DIST = 128
VMEM_LIMIT = 56 * 1024 * 1024


def _dot(a, b, precision=None):
    return lax.dot_general(a, b, (((1,), (0,)), ((), ())), precision=precision,
                           preferred_element_type=F32)


def _dot_nt(a, b, precision=None):
    return lax.dot_general(a, b, (((1,), (1,)), ((), ())), precision=precision,
                           preferred_element_type=F32)


def _bf(x):
    return x.astype(BF16)


def _dot_split(x, w_bf16):
    hi = _bf(x)
    lo = _bf(x - hi.astype(F32))
    return _dot(hi, w_bf16) + _dot(lo, w_bf16)


def _head_ones(n, dtype=F32):
    r = lax.broadcasted_iota(jnp.int32, (n, n), 0) // HEAD_DIM
    c = lax.broadcasted_iota(jnp.int32, (n, n), 1) // HEAD_DIM
    return (r == c).astype(dtype)


def _lead_or_x(is_lead, x_ref, meta_ref):
    lead = jnp.concatenate([jnp.zeros((PAD, D_MODEL), F32), meta_ref[...]], axis=0)
    return jnp.where(is_lead, lead, x_ref[...])


def _inproj_kernel(*refs):
    ncast = 3
    nsub = len(refs) - 6 - 2 * ncast
    x_refs, (meta_ref, nw_ref, w_ref) = refs[:nsub], refs[nsub:nsub + 3]
    cast_in = refs[nsub + 3:nsub + 3 + ncast]
    pr_ref, q_ref, kv_ref = refs[nsub + 3 + ncast:nsub + 6 + ncast]
    cast_out = refs[nsub + 6 + ncast:]
    for src, dst in zip(cast_in, cast_out):
        dst[...] = src[...].astype(BF16)
    first = pl.program_id(1) == 0
    x = jnp.concatenate([_lead_or_x(first if k == 0 else False, x_refs[k], meta_ref)
                         for k in range(nsub)], axis=0)
    ms = jnp.mean(x * x, axis=-1, keepdims=True)
    u = (x * lax.rsqrt(ms + NORM_EPS) * nw_ref[...]).astype(BF16)
    for j in range(IN_DIM // 256):
        c = _dot(u, w_ref[:, 256 * j:256 * (j + 1)])
        lo = 256 * j
        if lo < RWKV_IN:
            pr_ref[:, lo:lo + 256] = c
        elif lo < RWKV_IN + ATTN_DIM:
            q_ref[:, lo - RWKV_IN:lo - RWKV_IN + 256] = c
        else:
            kv_ref[...] = c


def _x_tile_spec(seq, nsub=1, k=0):
    nx = seq // LEAD
    return pl.BlockSpec((LEAD, D_MODEL),
                        lambda b, i: (b * nx + jnp.maximum(nsub * i + k - 1, 0), 0))


def _slice_spec(shape, axis, unit, nsteps, nt):
    size = shape[axis]
    blk = min(b for b in range(unit, size + 1, unit) if size % b == 0 and size // b <= nsteps)
    last = size // blk - 1
    block = tuple(blk if a == axis else s for a, s in enumerate(shape))
    step = lambda b, i: jnp.minimum(b * nt + i, last)
    return pl.BlockSpec(block, lambda b, i: tuple(step(b, i) if a == axis else 0
                                                  for a in range(len(shape))))


def _inproj(x2, meta, norm_w, w_in_bf16, later_weights, batch, seq):
    nlead = (LEAD + seq) // LEAD
    nsub = max(m for m in (3, 2, 1) if nlead % m == 0)
    tm = nsub * LEAD
    nt = nlead // nsub
    tp = batch * nlead * LEAD
    row = lambda b, i: (b * nt + i, 0)
    const = lambda b, i: (0, 0)
    cast_specs = [_slice_spec(w.shape, axis, 16 if axis == 0 else 128, batch * nt, nt)
                  for w, axis in later_weights]
    outs = pl.pallas_call(
        _inproj_kernel,
        grid=(batch, nt),
        in_specs=[_x_tile_spec(seq, nsub, k) for k in range(nsub)] + [
            pl.BlockSpec((N_META, D_MODEL), const),
            pl.BlockSpec((1, D_MODEL), const),
            pl.BlockSpec((D_MODEL, IN_DIM), const),
        ] + cast_specs,
        out_specs=[
            pl.BlockSpec((tm, RWKV_IN), row),
            pl.BlockSpec((tm, ATTN_DIM), row),
            pl.BlockSpec((tm, 2 * KV_DIM), row),
        ] + cast_specs,
        out_shape=[
            jax.ShapeDtypeStruct((tp, RWKV_IN), F32),
            jax.ShapeDtypeStruct((tp, ATTN_DIM), F32),
            jax.ShapeDtypeStruct((tp, 2 * KV_DIM), F32),
        ] + [jax.ShapeDtypeStruct(w.shape, BF16) for w, _ in later_weights],
        compiler_params=pltpu.CompilerParams(
            dimension_semantics=("arbitrary", "arbitrary"), vmem_limit_bytes=VMEM_LIMIT),
    )(*([x2] * nsub), meta, norm_w, w_in_bf16, *[w for w, _ in later_weights])
    return outs[:3], outs[3:]


def _blockdiag(z, lane_lo):
    zero = jnp.zeros_like(z)
    return jnp.concatenate([jnp.where(lane_lo, z, zero), jnp.where(lane_lo, zero, z)], axis=0)


def _rwkv_kernel(p_ref, halo_ref, mu_ref, w0_ref, decup_ref, a0_ref, aup_ref, gup_ref,
                 kk_ref, ka_ref, lnw_ref, lnb_ref, rk_ref, o_ref, s_ref):
    step = pl.program_id(0)
    nb, rows, _ = p_ref.shape
    c = CHUNK

    @pl.when(step == 0)
    def _():
        s_ref[...] = jnp.zeros_like(s_ref)

    ti = lax.broadcasted_iota(jnp.int32, (c, 128), 0)
    lane = lax.broadcasted_iota(jnp.int32, (c, 128), 1)
    si = lane % HEAD_DIM
    lane_lo = lane < HEAD_DIM
    strict = si < ti
    incl = si <= ti
    eye = si == ti
    eye_f = eye.astype(F32)
    blk_masks = [(ti // 8) == (si // 8)]
    size = 8
    while size < c:
        blk_masks.append(((ti // (2 * size)) == (si // (2 * size))) & ((ti // size) != (si // size)))
        size *= 2
    ones = _head_ones(128, BF16)
    row = lax.broadcasted_iota(jnp.int32, (rows, 1), 0)
    bd = lambda z: _blockdiag(z, lane_lo)

    def pair_t(z):
        zt = bd(z).T
        return zt[0:c, :] + zt[c:2 * c, :]

    npair = HEADS // 2

    def head_sums(x, split=False):
        n = x.shape[0]
        xs = jnp.concatenate([x[:, 128 * j:128 * (j + 1)] for j in range(npair)], axis=0)
        ys = _dot_split(xs, ones) if split else _dot(_bf(xs), ones)
        return jnp.concatenate([ys[n * j:n * (j + 1)] for j in range(npair)], axis=1)

    prevs = []
    for b in range(nb):
        prev = pltpu.roll(p_ref[b], shift=1, axis=0)
        prev = jnp.where(row == 0, halo_ref[b, 7:8, :], prev)
        prevs.append(jnp.where((row == 0) & (step == 0), 0.0, prev))
    p = jnp.concatenate([p_ref[b] for b in range(nb)], axis=0)
    ps = p + (jnp.concatenate(prevs, axis=0) - p) * mu_ref[...]
    r = ps[:, 0:512]
    k = ps[:, 512:1024]
    v = ps[:, 1024:1536]
    wa = ps[:, 1536:1664]
    gd = ps[:, 1664:1792]
    th = jnp.tanh(wa)
    th_hi = _bf(th)
    th_lo = _bf(th - th_hi.astype(F32))
    z = w0_ref[...] + _dot(jnp.concatenate([th_hi, th_lo, th_hi], axis=1), decup_ref[...])
    lw = -math.exp(-0.5) * jax.nn.sigmoid(z)
    a = jax.nn.sigmoid(a0_ref[...] + _dot(_bf(wa), aup_ref[...]))
    g = _dot(_bf(jax.nn.sigmoid(gd)), gup_ref[...])
    kk = k * kk_ref[...]
    kn = kk * lax.rsqrt(jnp.maximum(head_sums(kk * kk), 1e-24))
    bb = kn * a
    k = k * (1.0 + (a - 1.0) * ka_ref[...])
    bonus = head_sums(r * k * rk_ref[...]) * v

    rowc = lax.broadcasted_iota(jnp.int32, (c, 1), 0)
    chains = []
    for b in range(nb):
        for ch in range(rows // c):
            rs = slice(c * ch, c * (ch + 1))
            ra = slice(rows * b + c * ch, rows * b + c * (ch + 1))
            lwc = lw[ra]
            cum = lwc
            for sh in (1, 2, 4, 8, 16, 32):
                cum = cum + jnp.where(rowc >= sh, pltpu.roll(cum, shift=sh, axis=0), 0.0)
            cum_last = cum[c - 1:c, :]
            e_neg = jnp.exp(-cum)
            e_end = jnp.exp(cum_last - cum)
            rt = r[ra] * jnp.exp(cum)
            at = -kn[ra] * jnp.exp(cum - lwc)
            kt, bt = k[ra] * e_neg, bb[ra] * e_neg
            kh, bh = k[ra] * e_end, bb[ra] * e_end
            wc = jnp.exp(cum_last)
            for j in range(npair):
                sl = slice(128 * j, 128 * (j + 1))
                chains.append(dict(
                    b=b, ch=ch, j=j, rs=rs, sl=sl, rt=rt[:, sl], at=_bf(at[:, sl]),
                    rhs=jnp.concatenate([bd(_bf(bt[:, sl])), bd(_bf(kt[:, sl]))], axis=0),
                    vbd=bd(_bf(v[ra, sl])),
                    bht=_bf(pair_t(bh[:, sl])), kht=_bf(pair_t(kh[:, sl])),
                    wc=wc[:, sl], bonus=bonus[ra, sl], g=g[ra, sl]))

    def s_products(cd):
        cd["pm"] = _dot_nt(jnp.concatenate([cd["at"], _bf(cd["rt"])], axis=0), cd.pop("rhs"))

    def s_masks(cd):
        pm = cd.pop("pm")
        cd["a_ab"] = jnp.where(strict, pm[0:c, 0:128], 0.0)
        cd["p_rb"] = _bf(jnp.where(incl, pm[c:2 * c, 0:128], 0.0))
        a_ak = _bf(jnp.where(strict, pm[0:c, 128:256], 0.0))
        p_rk = _bf(jnp.where(incl, pm[c:2 * c, 128:256], 0.0))
        cd["xv"] = _dot(jnp.concatenate([a_ak, p_rk, cd.pop("kht")], axis=0), cd.pop("vbd"))

    def s_d2(cd):
        cd["d"] = jnp.where(blk_masks[0], cd["a_ab"], 0.0)
        db = _bf(cd["d"])
        cd["bdd"] = bd(db)
        cd["d2"] = _dot(db, cd["bdd"])

    def s_d34(cd):
        d2b = _bf(cd["d2"])
        cd["d34"] = _dot(d2b, jnp.concatenate([cd.pop("bdd"), bd(d2b)], axis=1))

    def s_t0(cd):
        d34 = cd.pop("d34")
        s3 = eye_f + cd.pop("d") + cd.pop("d2") + d34[:, 0:128]
        cd["t"] = s3 + _dot(_bf(d34[:, 128:256]), bd(_bf(s3)))

    def s_inner(level):
        def run(cd):
            cd["tb"] = _bf(cd["t"])
            cd["inner"] = _bf(_dot(_bf(jnp.where(blk_masks[level], cd["a_ab"], 0.0)), bd(cd["tb"])))
        return run

    def s_merge(cd):
        cd["t"] = cd["t"] + _dot(cd.pop("tb"), bd(cd.pop("inner")))

    def s_gu(cd):
        cd.pop("a_ab")
        gu = _bf(_dot(_bf(cd.pop("t")),
                      jnp.concatenate([bd(cd.pop("at")), bd(_bf(cd["xv"][0:c]))], axis=1)))
        cd["gu"] = jnp.concatenate([bd(gu[:, 0:128]), bd(gu[:, 128:256])], axis=1)

    def s_maps(cd):
        r = _dot(jnp.concatenate([cd.pop("p_rb"), cd.pop("bht")], axis=0), cd.pop("gu"))
        xv = cd.pop("xv")
        cd["y0"] = r[0:c, 128:256] + xv[c:2 * c]
        cd["n_add"] = r[c:2 * c, 128:256] + xv[2 * c:3 * c]
        cd["mq_lhs"] = _bf(jnp.concatenate([r[c:2 * c, 0:128], cd.pop("rt") + r[0:c, 0:128]], axis=0))

    stages = [s_products, s_masks, s_d2, s_d34, s_t0]
    for level in range(1, len(blk_masks)):
        stages += [s_inner(level), s_merge]
    stages += [s_gu, s_maps]
    for stage in stages:
        for cd in chains:
            stage(cd)
    wcc = _dot_split(jnp.concatenate(
        [jnp.where(eye, jnp.broadcast_to(cd["wc"], (c, 128)), 0.0) for cd in chains], axis=0), ones)
    for i, cd in enumerate(chains):
        cd["wc_col"] = wcc[c * i:c * (i + 1)]

    states = {(b, j): s_ref[b, j] for b in range(nb) for j in range(HEADS // 2)}
    for ch in range(rows // c):
        for cd in chains:
            if cd["ch"] != ch:
                continue
            st = states[(cd["b"], cd["j"])]
            mq = _dot(cd["mq_lhs"], bd(_bf(st)))
            states[(cd["b"], cd["j"])] = cd["wc_col"] * st + mq[0:c] + cd["n_add"]
            cd["y"] = mq[c:2 * c] + cd["y0"]
    for (b, j), st in states.items():
        s_ref[b, j] = st

    ys = jnp.concatenate([cd["y"] for cd in chains], axis=0)
    dy = ys - _dot_split(ys, ones) * (1.0 / HEAD_DIM)
    var = _dot(_bf(dy * dy), ones) * (1.0 / HEAD_DIM)
    zn = dy * lax.rsqrt(var + GN_EPS)
    for i, cd in enumerate(chains):
        sl = cd["sl"]
        yn = zn[c * i:c * (i + 1)] * lnw_ref[:, sl] + lnb_ref[:, sl]
        o_ref[cd["b"], cd["rs"], sl] = (yn + cd["bonus"]) * cd["g"]


def _rwkv(pr3, mu, w0, decup_pad, a0, aup_pad, gup, k_k, k_a, lnx_w, lnx_b, r_k):
    batch, lp, _ = pr3.shape
    rows = CHUNK * CHUNKS_PER_STEP
    const = lambda i: (0, 0)
    vec = pl.BlockSpec((1, RWKV_DIM), const)
    lora = pl.BlockSpec((128, RWKV_DIM), const)
    return pl.pallas_call(
        _rwkv_kernel,
        grid=(lp // rows,),
        in_specs=[
            pl.BlockSpec((batch, rows, RWKV_IN), lambda i: (0, i, 0)),
            pl.BlockSpec((batch, 8, RWKV_IN), lambda i: (0, jnp.maximum(i * (rows // 8) - 1, 0), 0)),
            pl.BlockSpec((1, RWKV_IN), const),
            vec, pl.BlockSpec((3 * 128, RWKV_DIM), const), vec, lora, lora, vec, vec, vec, vec, vec,
        ],
        out_specs=pl.BlockSpec((batch, rows, RWKV_DIM), lambda i: (0, i, 0)),
        out_shape=jax.ShapeDtypeStruct((batch, lp, RWKV_DIM), F32),
        scratch_shapes=[pltpu.VMEM((batch, HEADS // 2, CHUNK, 128), F32)],
        compiler_params=pltpu.CompilerParams(
            dimension_semantics=("arbitrary",), vmem_limit_bytes=VMEM_LIMIT),
    )(pr3, pr3, mu, w0, decup_pad, a0, aup_pad, gup, k_k, k_a, lnx_w, lnx_b, r_k)


def _t5_thresholds():
    n_log = REL_BUCKETS - REL_MAX_EXACT
    out = []
    for k in range(1, n_log):
        x = REL_MAX_EXACT * (REL_MAX_DIST / REL_MAX_EXACT) ** (k / n_log)
        assert min(x - math.floor(x), math.ceil(x) - x) > 1e-3
        out.append(math.ceil(x))
    return out


def _t5_bucket(d):
    d = jnp.maximum(d, 0)
    large = jnp.full(d.shape, REL_MAX_EXACT, jnp.int32)
    for t in _t5_thresholds():
        large = large + (d >= t).astype(jnp.int32)
    return jnp.where(d < REL_MAX_EXACT, d, large)


def _attn_kernel(rb_ref, sink_ref, q_ref, kv0_ref, kvp_ref, kvc_ref, qw_ref, kw_ref, o_ref, tbl_ref):
    n = pl.program_id(1)

    @pl.when((pl.program_id(0) == 0) & (n == 0))
    def _():
        q = lax.broadcasted_iota(jnp.int32, (BLOCK, 3 * BLOCK), 0)
        col = lax.broadcasted_iota(jnp.int32, (BLOCK, 3 * BLOCK), 1)
        is_meta = col < BLOCK
        for nn in range(3):
            d_meta = nn * BLOCK + q - col
            d = jnp.where(is_meta, d_meta, q + 2 * BLOCK - col)
            ok = is_meta & (col >= BLOCK - N_META) & (d_meta >= 0)
            if nn >= 2:
                ok = ok | ((col >= BLOCK) & (col < 2 * BLOCK) & (col - BLOCK > q))
            if nn >= 1:
                ok = ok | ((col >= 2 * BLOCK) & (col - 2 * BLOCK <= q))
            bucket = _t5_bucket(d)

            def per_head(h, carry):
                acc = jnp.zeros((BLOCK, 3 * BLOCK), F32)
                for bk in range(REL_BUCKETS):
                    acc = jnp.where(bucket == bk, rb_ref[bk, h], acc)
                tbl_ref[nn, h] = jnp.where(ok, acc, MASK_VALUE)
                return carry

            lax.fori_loop(0, HEADS, per_head, 0)

    ones = _head_ones(128, BF16)
    lane_lo = lax.broadcasted_iota(jnp.int32, (BLOCK, 128), 1) < HEAD_DIM
    lane_hi = jnp.logical_not(lane_lo)
    rr = lax.broadcasted_iota(jnp.int32, (128, 128), 0)
    cc = lax.broadcasted_iota(jnp.int32, (128, 128), 1)
    dup = [((rr // HEAD_DIM == g) & (rr % HEAD_DIM == cc % HEAD_DIM)).astype(BF16) for g in range(2)]

    def qk_norm(x, w):
        ms = _dot(_bf(x * x), ones) * (1.0 / HEAD_DIM)
        return x * lax.rsqrt(ms + NORM_EPS) * w

    nbk = q_ref.shape[0] // BLOCK
    grp = HEADS // 2
    qw = qw_ref[...] * HEAD_DIM ** -0.5

    kvb = [kv0_ref[...], kvp_ref[...]] + [kvc_ref[BLOCK * jb:BLOCK * (jb + 1), :] for jb in range(nbk)]
    kn = [_bf(qk_norm(x[:, 0:128], kw_ref[...])) for x in kvb]
    kd = [[_bf(_dot(x, dup[g])) for x in kn] for g in range(2)]
    vd = [[_bf(_dot(_bf(x[:, 128:256]), dup[g])) for x in kvb] for g in range(2)]
    qn = [[qk_norm(q_ref[BLOCK * jb:BLOCK * (jb + 1), 128 * c4:128 * (c4 + 1)], qw)
           for c4 in range(4)] for jb in range(nbk)]
    sinks = [jnp.concatenate([jnp.full((BLOCK, 128), sink_ref[0, grp * g + i], F32)
                              for i in range(grp)], axis=0) for g in range(2)]
    ones_cols = jnp.ones((3 * BLOCK, 128), BF16)

    def scores(c):
        jb, g = c["jb"], c["g"]
        lhs = jnp.concatenate(
            [jnp.where(lane_lo if e == 0 else lane_hi, qn[jb][2 * g + jj], 0.0)
             for jj in range(2) for e in range(2)], axis=0).astype(BF16)
        keys = jnp.concatenate([kd[g][0], kd[g][1 + jb], kd[g][2 + jb]], axis=0)
        tsel = jnp.clip(n * nbk + jb - META_BLOCK, 0, 2)
        tb = tbl_ref[tsel, pl.ds(grp * g, grp)].reshape(grp * BLOCK, 3 * BLOCK)
        c["s"] = _dot_nt(lhs, keys) + tb

    def row_max(c):
        x = c["s"]
        m3 = jnp.maximum(jnp.maximum(x[:, 0:128], x[:, 128:256]), x[:, 256:384])
        m = jnp.maximum(jnp.max(m3, axis=-1, keepdims=True), sinks[c["g"]][:, 0:1])
        c["mb"] = jnp.broadcast_to(m, (grp * BLOCK, 128))

    def exps(c):
        x, mm = c.pop("s"), c["mb"]
        c["ex"] = jnp.concatenate(
            [_bf(jnp.exp(x[:, 128 * i:128 * (i + 1)] - mm)) for i in range(3)], axis=1)

    def values(c):
        jb, g = c["jb"], c["g"]
        vals = jnp.concatenate([vd[g][0], vd[g][1 + jb], vd[g][2 + jb]], axis=0)
        od = _dot(c.pop("ex"), jnp.concatenate([vals, ones_cols], axis=1))
        den = od[:, 128:256] + jnp.exp(sinks[g] - c.pop("mb"))
        o = od[:, 0:128] * (1.0 / den)
        for jj in range(2):
            col = 2 * g + jj
            o_ref[BLOCK * jb:BLOCK * (jb + 1), 128 * col:128 * (col + 1)] = jnp.where(
                lane_lo, o[256 * jj:256 * jj + 128], o[256 * jj + 128:256 * jj + 256])

    stages = (scores, row_max, exps, values)
    chains = [dict(jb=jb, g=g) for jb in range(nbk) for g in range(2)]
    for w in range(len(chains) + len(stages) - 1):
        for ci, c in enumerate(chains):
            if 0 <= w - ci < len(stages):
                stages[w - ci](c)


def _attention(q, kv, rel_bias, sinks, q_norm_w2, k_norm_w2, batch, lp, tm):
    nb = lp // BLOCK
    nt = lp // tm
    nbk = tm // BLOCK
    return pl.pallas_call(
        _attn_kernel,
        grid=(batch, nt),
        in_specs=[
            pl.BlockSpec(memory_space=pltpu.SMEM),
            pl.BlockSpec(memory_space=pltpu.SMEM),
            pl.BlockSpec((tm, ATTN_DIM), lambda b, n: (b * nt + n, 0)),
            pl.BlockSpec((BLOCK, 2 * KV_DIM), lambda b, n: (b * nb + META_BLOCK, 0)),
            pl.BlockSpec((BLOCK, 2 * KV_DIM),
                         lambda b, n: (b * nb + jnp.maximum(n * nbk - 1, 0), 0)),
            pl.BlockSpec((tm, 2 * KV_DIM), lambda b, n: (b * nt + n, 0)),
            pl.BlockSpec((1, 128), lambda b, n: (0, 0)),
            pl.BlockSpec((1, 128), lambda b, n: (0, 0)),
        ],
        out_specs=pl.BlockSpec((tm, ATTN_DIM), lambda b, n: (b * nt + n, 0)),
        out_shape=jax.ShapeDtypeStruct(q.shape, F32),
        scratch_shapes=[pltpu.VMEM((3, HEADS, BLOCK, 3 * BLOCK), F32)],
        compiler_params=pltpu.CompilerParams(
            dimension_semantics=("arbitrary", "arbitrary"), vmem_limit_bytes=VMEM_LIMIT),
    )(rel_bias, sinks, q, kv, kv, kv, q_norm_w2, k_norm_w2)


def _ffn_kernel(x_ref, meta_ref, ya_ref, yb_ref, wo_ref, nw_ref, wup_ref, cw_ref, cb_ref, wdn_ref,
                o_ref, carry_ref, hbuf_ref):
    tm = x_ref.shape[0]
    pos = lax.broadcasted_iota(jnp.int32, (tm, 1), 0) + pl.program_id(1) * tm

    @pl.when(pl.program_id(1) == 0)
    def _():
        carry_ref[...] = jnp.zeros_like(carry_ref)

    h = (_lead_or_x(pl.program_id(1) == 0, x_ref, meta_ref) + _dot(ya_ref[...].astype(BF16), wo_ref[0:RWKV_DIM, :])
         + _dot(yb_ref[...].astype(BF16), wo_ref[RWKV_DIM:, :]))
    h = jnp.where(pos >= PAD, h, 0.0)
    ms = jnp.mean(h * h, axis=-1, keepdims=True)
    u = (h * lax.rsqrt(ms + NORM_EPS) * nw_ref[...]).astype(BF16)
    acc = h
    cw = 256
    nj = D_FF // cw
    nslot = hbuf_ref.shape[0]

    def up(j):
        return [_dot(u, wup_ref[:, base + cw * j:base + cw * (j + 1)]) for base in (0, D_FF)]

    hids = up(0)
    for j in range(nj):
        nxt = up(j + 1) if j + 1 < nj else None
        halves = []
        for half, (base, hid) in enumerate(zip((0, D_FF), hids)):
            lo = base + cw * j
            buf = hbuf_ref.at[(2 * j + half) % nslot]
            buf[0:8, :] = carry_ref[:, lo:lo + cw]
            buf[8:8 + tm, :] = hid
            carry_ref[:, lo:lo + cw] = hid[tm - 8:tm, :]
            halves.append(hid * cw_ref[0:1, lo:lo + cw]
                          + buf[7:7 + tm, :] * cw_ref[1:2, lo:lo + cw]
                          + buf[6:6 + tm, :] * cw_ref[2:3, lo:lo + cw]
                          + cb_ref[:, lo:lo + cw])
        gate, val = halves
        act = (gate * jax.nn.sigmoid(gate) * val).astype(BF16)
        acc = acc + _dot(act, wdn_ref[cw * j:cw * (j + 1), :])
        hids = nxt
    o_ref[...] = acc


def _ffn(x2, meta, ya, yb, w_out, norm_w, w_up, conv_w, conv_b, w_down, batch, seq):
    tm = LEAD
    nt = (LEAD + seq) // tm
    nx = seq // tm
    row = lambda b, i: (b * nt + i, 0)
    const = lambda b, i: (0, 0)
    return pl.pallas_call(
        _ffn_kernel,
        grid=(batch, nt),
        in_specs=[
            _x_tile_spec(seq),
            pl.BlockSpec((N_META, D_MODEL), const),
            pl.BlockSpec((tm, RWKV_DIM), row),
            pl.BlockSpec((tm, ATTN_DIM), row),
            pl.BlockSpec((D_MODEL, D_MODEL), const),
            pl.BlockSpec((1, D_MODEL), const),
            pl.BlockSpec((D_MODEL, 2 * D_FF), const),
            pl.BlockSpec((3, 2 * D_FF), const),
            pl.BlockSpec((1, 2 * D_FF), const),
            pl.BlockSpec((D_FF, D_MODEL), const),
        ],
        out_specs=pl.BlockSpec((tm, D_MODEL), lambda b, i: (b * nx + jnp.maximum(i - 1, 0), 0)),
        out_shape=jax.ShapeDtypeStruct(x2.shape, F32),
        scratch_shapes=[pltpu.VMEM((8, 2 * D_FF), F32), pltpu.VMEM((4, tm + 8, 256), F32)],
        compiler_params=pltpu.CompilerParams(
            dimension_semantics=("arbitrary", "arbitrary"), vmem_limit_bytes=VMEM_LIMIT),
    )(x2, meta, ya, yb, w_out, norm_w, w_up, conv_w, conv_b, w_down)


def _row_tile(lp, cap, unit=BLOCK):
    n = lp // unit
    best = 1
    for f in range(1, n + 1):
        if n % f == 0 and f * unit <= cap:
            best = f
    return best * unit


def kernel(x, meta_tokens, rel_bias, norm1_w, w_in, shift_mu, decay_w0, decay_up, aaa_a0, aaa_up, gate_up, k_k, k_a, r_k, lnx_w, lnx_b, q_norm_w, k_norm_w, sinks, w_out, norm2_w, w_up, conv_w, conv_b, w_down):
    batch, seq, _ = x.shape
    assert norm1_w.shape[0] == 1 and seq % LEAD == 0
    lp = LEAD + seq
    layer = 0
    x2 = x.reshape(batch * seq, D_MODEL)
    meta = meta_tokens.astype(x.dtype)
    row2 = lambda t: t.reshape(1, -1)
    zeros64 = jnp.zeros((64, RWKV_DIM), F32)
    dec = jnp.concatenate([decay_up[layer], zeros64], axis=0)
    dec_hi = dec.astype(BF16)
    dec_lo = (dec - dec_hi.astype(F32)).astype(BF16)
    (pr, q, kv), (w_out_b, w_up_b, w_down_b) = _inproj(
        x2, meta, row2(norm1_w[layer]), w_in[layer].astype(BF16),
        ((w_out[layer], 0), (w_up[layer], 1), (w_down[layer], 0)), batch, seq)
    y_rwkv = _rwkv(
        pr.reshape(batch, lp, RWKV_IN), row2(shift_mu[layer]), row2(decay_w0[layer]),
        jnp.concatenate([dec_hi, dec_hi, dec_lo], axis=0), row2(aaa_a0[layer]),
        jnp.concatenate([zeros64, aaa_up[layer]], axis=0).astype(BF16),
        gate_up[layer].astype(BF16), row2(k_k[layer]), row2(k_a[layer]),
        row2(lnx_w[layer]), row2(lnx_b[layer]), row2(r_k[layer]))
    y_attn = _attention(q, kv, rel_bias, sinks[layer].reshape(1, HEADS),
                        jnp.tile(q_norm_w[layer], 2).reshape(1, 128),
                        jnp.tile(k_norm_w[layer], 2).reshape(1, 128), batch, lp,
                        _row_tile(lp, 768))
    out = _ffn(x2, meta, y_rwkv.reshape(batch * lp, RWKV_DIM), y_attn, w_out_b,
               row2(norm2_w[layer]), w_up_b, conv_w[layer], row2(conv_b[layer]), w_down_b,
               batch, seq)
    return out.reshape(batch, seq, D_MODEL)
```

```python
import math

import jax
import jax.numpy as jnp
from jax import lax
from jax.experimental import pallas as pl
from jax.experimental.pallas import tpu as pltpu

F32 = jnp.float32
BF16 = jnp.bfloat16

D_MODEL = 1024
N_META = 16
BLOCK = 128
LEAD = 2 * BLOCK
PAD = LEAD - N_META
META_BLOCK = 1
HEADS = 8
HEAD_DIM = 64
RWKV_DIM = 512
RWKV_IN = 1792
ATTN_DIM = 512
KV_DIM = 128
IN_DIM = 2560
D_FF = 2816
CHUNK = 64
CHUNKS_PER_STEP = 4
NORM_EPS = 1e-6
GN_EPS = HEAD_DIM * 1e-5
MASK_VALUE = -1e30
REL_BUCKETS = 32
REL_MAX_EXACT = 16
REL_MAX_DIST = 128
VMEM_LIMIT = 56 * 1024 * 1024


def _dot(a, b, precision=None):
    return lax.dot_general(a, b, (((1,), (0,)), ((), ())), precision=precision,
                           preferred_element_type=F32)


def _dot_nt(a, b, precision=None):
    return lax.dot_general(a, b, (((1,), (1,)), ((), ())), precision=precision,
                           preferred_element_type=F32)


def _bf(x):
    return x.astype(BF16)


def _dot_split(x, w_bf16):
    hi = _bf(x)
    lo = _bf(x - hi.astype(F32))
    return _dot(hi, w_bf16) + _dot(lo, w_bf16)


def _head_ones(n, dtype=F32):
    r = lax.broadcasted_iota(jnp.int32, (n, n), 0) // HEAD_DIM
    c = lax.broadcasted_iota(jnp.int32, (n, n), 1) // HEAD_DIM
    return (r == c).astype(dtype)


def _lead_or_x(is_lead, x_ref, meta_ref):
    lead = jnp.concatenate([jnp.zeros((PAD, D_MODEL), F32), meta_ref[...]], axis=0)
    return jnp.where(is_lead, lead, x_ref[...])


def _inproj_kernel(*refs):
    ncast = 3
    nsub = len(refs) - 6 - 2 * ncast
    x_refs, (meta_ref, nw_ref, w_ref) = refs[:nsub], refs[nsub:nsub + 3]
    cast_in = refs[nsub + 3:nsub + 3 + ncast]
    pr_ref, q_ref, kv_ref = refs[nsub + 3 + ncast:nsub + 6 + ncast]
    cast_out = refs[nsub + 6 + ncast:]
    for src, dst in zip(cast_in, cast_out):
        dst[...] = src[...].astype(BF16)
    first = pl.program_id(1) == 0
    x = jnp.concatenate([_lead_or_x(first if k == 0 else False, x_refs[k], meta_ref)
                         for k in range(nsub)], axis=0)
    ms = jnp.mean(x * x, axis=-1, keepdims=True)
    u = (x * lax.rsqrt(ms + NORM_EPS) * nw_ref[...]).astype(BF16)
    for j in range(IN_DIM // 256):
        c = _dot(u, w_ref[:, 256 * j:256 * (j + 1)])
        lo = 256 * j
        if lo < RWKV_IN:
            pr_ref[:, lo:lo + 256] = c
        elif lo < RWKV_IN + ATTN_DIM:
            q_ref[:, lo - RWKV_IN:lo - RWKV_IN + 256] = c
        else:
            kv_ref[...] = c


def _x_tile_spec(seq, nsub=1, k=0):
    nx = seq // LEAD
    return pl.BlockSpec((LEAD, D_MODEL),
                        lambda b, i: (b * nx + jnp.maximum(nsub * i + k - 1, 0), 0))


def _slice_spec(shape, axis, unit, nsteps, nt):
    size = shape[axis]
    blk = min(b for b in range(unit, size + 1, unit) if size % b == 0 and size // b <= nsteps)
    last = size // blk - 1
    block = tuple(blk if a == axis else s for a, s in enumerate(shape))
    step = lambda b, i: jnp.minimum(b * nt + i, last)
    return pl.BlockSpec(block, lambda b, i: tuple(step(b, i) if a == axis else 0
                                                  for a in range(len(shape))))


def _inproj(x2, meta, norm_w, w_in_bf16, later_weights, batch, seq):
    nlead = (LEAD + seq) // LEAD
    nsub = max(m for m in (3, 2, 1) if nlead % m == 0)
    tm = nsub * LEAD
    nt = nlead // nsub
    tp = batch * nlead * LEAD
    row = lambda b, i: (b * nt + i, 0)
    const = lambda b, i: (0, 0)
    cast_specs = [_slice_spec(w.shape, axis, 16 if axis == 0 else 128, batch * nt, nt)
                  for w, axis in later_weights]
    outs = pl.pallas_call(
        _inproj_kernel,
        grid=(batch, nt),
        in_specs=[_x_tile_spec(seq, nsub, k) for k in range(nsub)] + [
            pl.BlockSpec((N_META, D_MODEL), const),
            pl.BlockSpec((1, D_MODEL), const),
            pl.BlockSpec((D_MODEL, IN_DIM), const),
        ] + cast_specs,
        out_specs=[
            pl.BlockSpec((tm, RWKV_IN), row),
            pl.BlockSpec((tm, ATTN_DIM), row),
            pl.BlockSpec((tm, 2 * KV_DIM), row),
        ] + cast_specs,
        out_shape=[
            jax.ShapeDtypeStruct((tp, RWKV_IN), F32),
            jax.ShapeDtypeStruct((tp, ATTN_DIM), F32),
            jax.ShapeDtypeStruct((tp, 2 * KV_DIM), F32),
        ] + [jax.ShapeDtypeStruct(w.shape, BF16) for w, _ in later_weights],
        compiler_params=pltpu.CompilerParams(
            dimension_semantics=("arbitrary", "arbitrary"), vmem_limit_bytes=VMEM_LIMIT),
    )(*([x2] * nsub), meta, norm_w, w_in_bf16, *[w for w, _ in later_weights])
    return outs[:3], outs[3:]


def _blockdiag(z, lane_lo):
    zero = jnp.zeros_like(z)
    return jnp.concatenate([jnp.where(lane_lo, z, zero), jnp.where(lane_lo, zero, z)], axis=0)


def _rwkv_kernel(p_ref, halo_ref, mu_ref, w0_ref, decup_ref, a0_ref, aup_ref, gup_ref,
                 kk_ref, ka_ref, lnw_ref, lnb_ref, rk_ref, o_ref, s_ref):
    step = pl.program_id(0)
    nb, rows, _ = p_ref.shape
    c = CHUNK

    @pl.when(step == 0)
    def _():
        s_ref[...] = jnp.zeros_like(s_ref)

    ti = lax.broadcasted_iota(jnp.int32, (c, 128), 0)
    lane = lax.broadcasted_iota(jnp.int32, (c, 128), 1)
    si = lane % HEAD_DIM
    lane_lo = lane < HEAD_DIM
    strict = si < ti
    incl = si <= ti
    eye = si == ti
    eye_f = eye.astype(F32)
    blk_masks = [(ti // 8) == (si // 8)]
    size = 8
    while size < c:
        blk_masks.append(((ti // (2 * size)) == (si // (2 * size))) & ((ti // size) != (si // size)))
        size *= 2
    ones = _head_ones(128, BF16)
    row = lax.broadcasted_iota(jnp.int32, (rows, 1), 0)
    bd = lambda z: _blockdiag(z, lane_lo)

    def pair_t(z):
        zt = bd(z).T
        return zt[0:c, :] + zt[c:2 * c, :]

    npair = HEADS // 2

    def head_sums(x, split=False):
        n = x.shape[0]
        xs = jnp.concatenate([x[:, 128 * j:128 * (j + 1)] for j in range(npair)], axis=0)
        ys = _dot_split(xs, ones) if split else _dot(_bf(xs), ones)
        return jnp.concatenate([ys[n * j:n * (j + 1)] for j in range(npair)], axis=1)

    prevs = []
    for b in range(nb):
        prev = pltpu.roll(p_ref[b], shift=1, axis=0)
        prev = jnp.where(row == 0, halo_ref[b, 7:8, :], prev)
        prevs.append(jnp.where((row == 0) & (step == 0), 0.0, prev))
    p = jnp.concatenate([p_ref[b] for b in range(nb)], axis=0)
    ps = p + (jnp.concatenate(prevs, axis=0) - p) * mu_ref[...]
    r = ps[:, 0:512]
    k = ps[:, 512:1024]
    v = ps[:, 1024:1536]
    wa = ps[:, 1536:1664]
    gd = ps[:, 1664:1792]
    th = jnp.tanh(wa)
    th_hi = _bf(th)
    th_lo = _bf(th - th_hi.astype(F32))
    z = w0_ref[...] + _dot(jnp.concatenate([th_hi, th_lo, th_hi], axis=1), decup_ref[...])
    lw = -math.exp(-0.5) * jax.nn.sigmoid(z)
    a = jax.nn.sigmoid(a0_ref[...] + _dot(_bf(wa), aup_ref[...]))
    g = _dot(_bf(jax.nn.sigmoid(gd)), gup_ref[...])
    kk = k * kk_ref[...]
    kn = kk * lax.rsqrt(jnp.maximum(head_sums(kk * kk), 1e-24))
    bb = kn * a
    k = k * (1.0 + (a - 1.0) * ka_ref[...])
    bonus = head_sums(r * k * rk_ref[...]) * v

    rowc = lax.broadcasted_iota(jnp.int32, (c, 1), 0)
    chains = []
    for b in range(nb):
        for ch in range(rows // c):
            rs = slice(c * ch, c * (ch + 1))
            ra = slice(rows * b + c * ch, rows * b + c * (ch + 1))
            lwc = lw[ra]
            cum = lwc
            for sh in (1, 2, 4, 8, 16, 32):
                cum = cum + jnp.where(rowc >= sh, pltpu.roll(cum, shift=sh, axis=0), 0.0)
            cum_last = cum[c - 1:c, :]
            e_neg = jnp.exp(-cum)
            e_end = jnp.exp(cum_last - cum)
            rt = r[ra] * jnp.exp(cum)
            at = -kn[ra] * jnp.exp(cum - lwc)
            kt, bt = k[ra] * e_neg, bb[ra] * e_neg
            kh, bh = k[ra] * e_end, bb[ra] * e_end
            wc = jnp.exp(cum_last)
            for j in range(npair):
                sl = slice(128 * j, 128 * (j + 1))
                chains.append(dict(
                    b=b, ch=ch, j=j, rs=rs, sl=sl, rt=rt[:, sl], at=_bf(at[:, sl]),
                    rhs=jnp.concatenate([bd(_bf(bt[:, sl])), bd(_bf(kt[:, sl]))], axis=0),
                    vbd=bd(_bf(v[ra, sl])),
                    bht=_bf(pair_t(bh[:, sl])), kht=_bf(pair_t(kh[:, sl])),
                    wc=wc[:, sl], bonus=bonus[ra, sl], g=g[ra, sl]))

    def s_products(cd):
        cd["pm"] = _dot_nt(jnp.concatenate([cd["at"], _bf(cd["rt"])], axis=0), cd.pop("rhs"))

    def s_masks(cd):
        pm = cd.pop("pm")
        cd["a_ab"] = jnp.where(strict, pm[0:c, 0:128], 0.0)
        cd["p_rb"] = _bf(jnp.where(incl, pm[c:2 * c, 0:128], 0.0))
        a_ak = _bf(jnp.where(strict, pm[0:c, 128:256], 0.0))
        p_rk = _bf(jnp.where(incl, pm[c:2 * c, 128:256], 0.0))
        cd["xv"] = _dot(jnp.concatenate([a_ak, p_rk, cd.pop("kht")], axis=0), cd.pop("vbd"))

    def s_d2(cd):
        cd["d"] = jnp.where(blk_masks[0], cd["a_ab"], 0.0)
        db = _bf(cd["d"])
        cd["bdd"] = bd(db)
        cd["d2"] = _dot(db, cd["bdd"])

    def s_d34(cd):
        d2b = _bf(cd["d2"])
        cd["d34"] = _dot(d2b, jnp.concatenate([cd.pop("bdd"), bd(d2b)], axis=1))

    def s_t0(cd):
        d34 = cd.pop("d34")
        s3 = eye_f + cd.pop("d") + cd.pop("d2") + d34[:, 0:128]
        cd["t"] = s3 + _dot(_bf(d34[:, 128:256]), bd(_bf(s3)))

    def s_inner(level):
        def run(cd):
            cd["tb"] = _bf(cd["t"])
            cd["inner"] = _bf(_dot(_bf(jnp.where(blk_masks[level], cd["a_ab"], 0.0)), bd(cd["tb"])))
        return run

    def s_merge(cd):
        cd["t"] = cd["t"] + _dot(cd.pop("tb"), bd(cd.pop("inner")))

    def s_gu(cd):
        cd.pop("a_ab")
        gu = _bf(_dot(_bf(cd.pop("t")),
                      jnp.concatenate([bd(cd.pop("at")), bd(_bf(cd["xv"][0:c]))], axis=1)))
        cd["gu"] = jnp.concatenate([bd(gu[:, 0:128]), bd(gu[:, 128:256])], axis=1)

    def s_maps(cd):
        r = _dot(jnp.concatenate([cd.pop("p_rb"), cd.pop("bht")], axis=0), cd.pop("gu"))
        xv = cd.pop("xv")
        cd["y0"] = r[0:c, 128:256] + xv[c:2 * c]
        cd["n_add"] = r[c:2 * c, 128:256] + xv[2 * c:3 * c]
        cd["mq_lhs"] = _bf(jnp.concatenate([r[c:2 * c, 0:128], cd.pop("rt") + r[0:c, 0:128]], axis=0))

    stages = [s_products, s_masks, s_d2, s_d34, s_t0]
    for level in range(1, len(blk_masks)):
        stages += [s_inner(level), s_merge]
    stages += [s_gu, s_maps]
    for stage in stages:
        for cd in chains:
            stage(cd)
    wcc = _dot_split(jnp.concatenate(
        [jnp.where(eye, jnp.broadcast_to(cd["wc"], (c, 128)), 0.0) for cd in chains], axis=0), ones)
    for i, cd in enumerate(chains):
        cd["wc_col"] = wcc[c * i:c * (i + 1)]

    states = {(b, j): s_ref[b, j] for b in range(nb) for j in range(HEADS // 2)}
    for ch in range(rows // c):
        for cd in chains:
            if cd["ch"] != ch:
                continue
            st = states[(cd["b"], cd["j"])]
            mq = _dot(cd["mq_lhs"], bd(_bf(st)))
            states[(cd["b"], cd["j"])] = cd["wc_col"] * st + mq[0:c] + cd["n_add"]
            cd["y"] = mq[c:2 * c] + cd["y0"]
    for (b, j), st in states.items():
        s_ref[b, j] = st

    ys = jnp.concatenate([cd["y"] for cd in chains], axis=0)
    dy = ys - _dot_split(ys, ones) * (1.0 / HEAD_DIM)
    var = _dot(_bf(dy * dy), ones) * (1.0 / HEAD_DIM)
    zn = dy * lax.rsqrt(var + GN_EPS)
    for i, cd in enumerate(chains):
        sl = cd["sl"]
        yn = zn[c * i:c * (i + 1)] * lnw_ref[:, sl] + lnb_ref[:, sl]
        o_ref[cd["b"], cd["rs"], sl] = (yn + cd["bonus"]) * cd["g"]


def _rwkv(pr3, mu, w0, decup_pad, a0, aup_pad, gup, k_k, k_a, lnx_w, lnx_b, r_k):
    batch, lp, _ = pr3.shape
    rows = CHUNK * CHUNKS_PER_STEP
    const = lambda i: (0, 0)
    vec = pl.BlockSpec((1, RWKV_DIM), const)
    lora = pl.BlockSpec((128, RWKV_DIM), const)
    return pl.pallas_call(
        _rwkv_kernel,
        grid=(lp // rows,),
        in_specs=[
            pl.BlockSpec((batch, rows, RWKV_IN), lambda i: (0, i, 0)),
            pl.BlockSpec((batch, 8, RWKV_IN), lambda i: (0, jnp.maximum(i * (rows // 8) - 1, 0), 0)),
            pl.BlockSpec((1, RWKV_IN), const),
            vec, pl.BlockSpec((3 * 128, RWKV_DIM), const), vec, lora, lora, vec, vec, vec, vec, vec,
        ],
        out_specs=pl.BlockSpec((batch, rows, RWKV_DIM), lambda i: (0, i, 0)),
        out_shape=jax.ShapeDtypeStruct((batch, lp, RWKV_DIM), F32),
        scratch_shapes=[pltpu.VMEM((batch, HEADS // 2, CHUNK, 128), F32)],
        compiler_params=pltpu.CompilerParams(
            dimension_semantics=("arbitrary",), vmem_limit_bytes=VMEM_LIMIT),
    )(pr3, pr3, mu, w0, decup_pad, a0, aup_pad, gup, k_k, k_a, lnx_w, lnx_b, r_k)


def _t5_thresholds():
    n_log = REL_BUCKETS - REL_MAX_EXACT
    out = []
    for k in range(1, n_log):
        x = REL_MAX_EXACT * (REL_MAX_DIST / REL_MAX_EXACT) ** (k / n_log)
        assert min(x - math.floor(x), math.ceil(x) - x) > 1e-3
        out.append(math.ceil(x))
    return out


def _t5_bucket(d):
    d = jnp.maximum(d, 0)
    large = jnp.full(d.shape, REL_MAX_EXACT, jnp.int32)
    for t in _t5_thresholds():
        large = large + (d >= t).astype(jnp.int32)
    return jnp.where(d < REL_MAX_EXACT, d, large)


def _attn_kernel(rb_ref, sink_ref, q_ref, kv0_ref, kvp_ref, kvc_ref, qw_ref, kw_ref, o_ref, tbl_ref):
    n = pl.program_id(1)

    @pl.when((pl.program_id(0) == 0) & (n == 0))
    def _():
        q = lax.broadcasted_iota(jnp.int32, (BLOCK, 3 * BLOCK), 0)
        col = lax.broadcasted_iota(jnp.int32, (BLOCK, 3 * BLOCK), 1)
        is_meta = col < BLOCK
        for nn in range(3):
            d_meta = nn * BLOCK + q - col
            d = jnp.where(is_meta, d_meta, q + 2 * BLOCK - col)
            ok = is_meta & (col >= BLOCK - N_META) & (d_meta >= 0)
            if nn >= 2:
                ok = ok | ((col >= BLOCK) & (col < 2 * BLOCK) & (col - BLOCK > q))
            if nn >= 1:
                ok = ok | ((col >= 2 * BLOCK) & (col - 2 * BLOCK <= q))
            bucket = _t5_bucket(d)

            def per_head(h, carry):
                acc = jnp.zeros((BLOCK, 3 * BLOCK), F32)
                for bk in range(REL_BUCKETS):
                    acc = jnp.where(bucket == bk, rb_ref[bk, h], acc)
                tbl_ref[nn, h] = jnp.where(ok, acc, MASK_VALUE)
                return carry

            lax.fori_loop(0, HEADS, per_head, 0)

    ones = _head_ones(128, BF16)
    lane_lo = lax.broadcasted_iota(jnp.int32, (BLOCK, 128), 1) < HEAD_DIM
    lane_hi = jnp.logical_not(lane_lo)
    rr = lax.broadcasted_iota(jnp.int32, (128, 128), 0)
    cc = lax.broadcasted_iota(jnp.int32, (128, 128), 1)
    dup = [((rr // HEAD_DIM == g) & (rr % HEAD_DIM == cc % HEAD_DIM)).astype(BF16) for g in range(2)]

    def qk_norm(x, w):
        ms = _dot(_bf(x * x), ones) * (1.0 / HEAD_DIM)
        return x * lax.rsqrt(ms + NORM_EPS) * w

    nbk = q_ref.shape[0] // BLOCK
    grp = HEADS // 2
    qw = qw_ref[...] * HEAD_DIM ** -0.5

    kvb = [kv0_ref[...], kvp_ref[...]] + [kvc_ref[BLOCK * jb:BLOCK * (jb + 1), :] for jb in range(nbk)]
    kn = [_bf(qk_norm(x[:, 0:128], kw_ref[...])) for x in kvb]
    kd = [[_bf(_dot(x, dup[g])) for x in kn] for g in range(2)]
    vd = [[_bf(_dot(_bf(x[:, 128:256]), dup[g])) for x in kvb] for g in range(2)]
    qn = [[qk_norm(q_ref[BLOCK * jb:BLOCK * (jb + 1), 128 * c4:128 * (c4 + 1)], qw)
           for c4 in range(4)] for jb in range(nbk)]
    sinks = [jnp.concatenate([jnp.full((BLOCK, 128), sink_ref[0, grp * g + i], F32)
                              for i in range(grp)], axis=0) for g in range(2)]
    ones_cols = jnp.ones((3 * BLOCK, 128), BF16)

    def scores(c):
        jb, g = c["jb"], c["g"]
        lhs = jnp.concatenate(
            [jnp.where(lane_lo if e == 0 else lane_hi, qn[jb][2 * g + jj], 0.0)
             for jj in range(2) for e in range(2)], axis=0).astype(BF16)
        keys = jnp.concatenate([kd[g][0], kd[g][1 + jb], kd[g][2 + jb]], axis=0)
        tsel = jnp.clip(n * nbk + jb - META_BLOCK, 0, 2)
        tb = tbl_ref[tsel, pl.ds(grp * g, grp)].reshape(grp * BLOCK, 3 * BLOCK)
        c["s"] = _dot_nt(lhs, keys) + tb

    def row_max(c):
        x = c["s"]
        m3 = jnp.maximum(jnp.maximum(x[:, 0:128], x[:, 128:256]), x[:, 256:384])
        m = jnp.maximum(jnp.max(m3, axis=-1, keepdims=True), sinks[c["g"]][:, 0:1])
        c["mb"] = jnp.broadcast_to(m, (grp * BLOCK, 128))

    def exps(c):
        x, mm = c.pop("s"), c["mb"]
        c["ex"] = jnp.concatenate(
            [_bf(jnp.exp(x[:, 128 * i:128 * (i + 1)] - mm)) for i in range(3)], axis=1)

    def values(c):
        jb, g = c["jb"], c["g"]
        vals = jnp.concatenate([vd[g][0], vd[g][1 + jb], vd[g][2 + jb]], axis=0)
        od = _dot(c.pop("ex"), jnp.concatenate([vals, ones_cols], axis=1))
        den = od[:, 128:256] + jnp.exp(sinks[g] - c.pop("mb"))
        o = od[:, 0:128] * (1.0 / den)
        for jj in range(2):
            col = 2 * g + jj
            o_ref[BLOCK * jb:BLOCK * (jb + 1), 128 * col:128 * (col + 1)] = jnp.where(
                lane_lo, o[256 * jj:256 * jj + 128], o[256 * jj + 128:256 * jj + 256])

    stages = (scores, row_max, exps, values)
    chains = [dict(jb=jb, g=g) for jb in range(nbk) for g in range(2)]
    for w in range(len(chains) + len(stages) - 1):
        for ci, c in enumerate(chains):
            if 0 <= w - ci < len(stages):
                stages[w - ci](c)


def _attention(q, kv, rel_bias, sinks, q_norm_w2, k_norm_w2, batch, lp, tm):
    nb = lp // BLOCK
    nt = lp // tm
    nbk = tm // BLOCK
    return pl.pallas_call(
        _attn_kernel,
        grid=(batch, nt),
        in_specs=[
            pl.BlockSpec(memory_space=pltpu.SMEM),
            pl.BlockSpec(memory_space=pltpu.SMEM),
            pl.BlockSpec((tm, ATTN_DIM), lambda b, n: (b * nt + n, 0)),
            pl.BlockSpec((BLOCK, 2 * KV_DIM), lambda b, n: (b * nb + META_BLOCK, 0)),
            pl.BlockSpec((BLOCK, 2 * KV_DIM),
                         lambda b, n: (b * nb + jnp.maximum(n * nbk - 1, 0), 0)),
            pl.BlockSpec((tm, 2 * KV_DIM), lambda b, n: (b * nt + n, 0)),
            pl.BlockSpec((1, 128), lambda b, n: (0, 0)),
            pl.BlockSpec((1, 128), lambda b, n: (0, 0)),
        ],
        out_specs=pl.BlockSpec((tm, ATTN_DIM), lambda b, n: (b * nt + n, 0)),
        out_shape=jax.ShapeDtypeStruct(q.shape, F32),
        scratch_shapes=[pltpu.VMEM((3, HEADS, BLOCK, 3 * BLOCK), F32)],
        compiler_params=pltpu.CompilerParams(
            dimension_semantics=("arbitrary", "arbitrary"), vmem_limit_bytes=VMEM_LIMIT),
    )(rel_bias, sinks, q, kv, kv, kv, q_norm_w2, k_norm_w2)


def _mixed(h_in, yr, ya, wo_ref):
    return (h_in + _dot(_bf(yr), wo_ref[0:RWKV_DIM, :]) + _dot(_bf(ya), wo_ref[RWKV_DIM:, :]))


def _normed(h, nw_ref):
    ms = jnp.mean(h * h, axis=-1, keepdims=True)
    return _bf(h * lax.rsqrt(ms + NORM_EPS) * nw_ref[...])


def _ffn_lead_kernel(meta_ref, yr_ref, ya_ref, wo_ref, nw_ref, wup_ref, o_ref):
    h = _mixed(meta_ref[...], yr_ref[...], ya_ref[...], wo_ref)
    o_ref[0] = _dot(_normed(h, nw_ref), wup_ref[...])[N_META - 8:N_META, :]


FFN_SUB = 2


def _ffn_kernel(x_ref, *refs):
    yr_refs, ya_refs = refs[0:FFN_SUB], refs[FFN_SUB:2 * FFN_SUB]
    (lead_ref, wo_ref, nw_ref, wup_ref, cw_ref, cb_ref, wdn_ref,
     o_ref, carry_ref, hbuf_ref) = refs[2 * FFN_SUB:]
    tm = LEAD
    cw = 256
    nj = D_FF // cw
    nslot = hbuf_ref.shape[0]

    @pl.when(pl.program_id(1) == 0)
    def _():
        carry_ref[...] = lead_ref[0]

    subs = range(FFN_SUB)
    hres = [_mixed(x_ref[tm * s:tm * (s + 1), :], yr_refs[s][...], ya_refs[s][...], wo_ref)
            for s in subs]
    u = [_normed(h, nw_ref) for h in hres]
    acc = [None for _ in subs]

    def up(s, j):
        return [_dot(u[s], wup_ref[:, base + cw * j:base + cw * (j + 1)]) for base in (0, D_FF)]

    items = [(s, j) for s in subs for j in range(nj)]
    hids = up(*items[0])
    for n, (s, j) in enumerate(items):
        nxt = up(*items[n + 1]) if n + 1 < len(items) else None
        halves = []
        for half, (base, hid) in enumerate(zip((0, D_FF), hids)):
            lo = base + cw * j
            buf = hbuf_ref.at[(2 * n + half) % nslot]
            buf[0:8, :] = carry_ref[:, lo:lo + cw]
            buf[8:8 + tm, :] = hid
            carry_ref[:, lo:lo + cw] = hid[tm - 8:tm, :]
            halves.append(hid * cw_ref[0:1, lo:lo + cw]
                          + buf[7:7 + tm, :] * cw_ref[1:2, lo:lo + cw]
                          + buf[6:6 + tm, :] * cw_ref[2:3, lo:lo + cw]
                          + cb_ref[:, lo:lo + cw])
        gate, val = halves
        act = (gate * jax.nn.sigmoid(gate) * val).astype(BF16)
        down = _dot(act, wdn_ref[cw * j:cw * (j + 1), :])
        acc[s] = down if acc[s] is None else acc[s] + down
        hids = nxt
        if j == nj - 1:
            o_ref[tm * s:tm * (s + 1), :] = hres[s] + acc[s]


def _ffn_lead(meta, yr, ya, w_out, norm_w, w_up, batch, seq):
    nlead = (LEAD + seq) // LEAD
    const2 = lambda b: (0, 0)
    lead_rows = lambda b: ((b * nlead + 1) * (LEAD // N_META) - 1, 0)
    return pl.pallas_call(
        _ffn_lead_kernel,
        grid=(batch,),
        in_specs=[
            pl.BlockSpec((N_META, D_MODEL), const2),
            pl.BlockSpec((N_META, RWKV_DIM), lead_rows),
            pl.BlockSpec((N_META, ATTN_DIM), lead_rows),
            pl.BlockSpec((D_MODEL, D_MODEL), const2),
            pl.BlockSpec((1, D_MODEL), const2),
            pl.BlockSpec((D_MODEL, 2 * D_FF), const2),
        ],
        out_specs=pl.BlockSpec((1, 8, 2 * D_FF), lambda b: (b, 0, 0)),
        out_shape=jax.ShapeDtypeStruct((batch, 8, 2 * D_FF), F32),
        compiler_params=pltpu.CompilerParams(
            dimension_semantics=("arbitrary",), vmem_limit_bytes=VMEM_LIMIT),
    )(meta, yr, ya, w_out, norm_w, w_up)


def _ffn(x2, lead, yr, ya, w_out, norm_w, w_up, conv_w, conv_b, w_down, batch, seq):
    nlead = (LEAD + seq) // LEAD
    assert (nlead - 1) % FFN_SUB == 0
    tm = FFN_SUB * LEAD
    nt = seq // tm
    row = lambda b, i: (b * nt + i, 0)
    const = lambda b, i: (0, 0)
    sub = lambda width: [pl.BlockSpec((LEAD, width),
                                      lambda b, i, s=s: (b * nlead + 1 + FFN_SUB * i + s, 0))
                         for s in range(FFN_SUB)]
    return pl.pallas_call(
        _ffn_kernel,
        grid=(batch, nt),
        in_specs=[pl.BlockSpec((tm, D_MODEL), row)] + sub(RWKV_DIM) + sub(ATTN_DIM) + [
            pl.BlockSpec((1, 8, 2 * D_FF), lambda b, i: (b, 0, 0)),
            pl.BlockSpec((D_MODEL, D_MODEL), const),
            pl.BlockSpec((1, D_MODEL), const),
            pl.BlockSpec((D_MODEL, 2 * D_FF), const),
            pl.BlockSpec((3, 2 * D_FF), const),
            pl.BlockSpec((1, 2 * D_FF), const),
            pl.BlockSpec((D_FF, D_MODEL), const),
        ],
        out_specs=pl.BlockSpec((tm, D_MODEL), row),
        out_shape=jax.ShapeDtypeStruct(x2.shape, F32),
        scratch_shapes=[pltpu.VMEM((8, 2 * D_FF), F32),
                        pltpu.VMEM((4 * FFN_SUB, LEAD + 8, 256), F32)],
        compiler_params=pltpu.CompilerParams(
            dimension_semantics=("arbitrary", "arbitrary"), vmem_limit_bytes=VMEM_LIMIT),
    )(x2, *([yr] * FFN_SUB), *([ya] * FFN_SUB), lead, w_out, norm_w, w_up, conv_w, conv_b, w_down)


def _row_tile(lp, cap, unit=BLOCK):
    n = lp // unit
    best = 1
    for f in range(1, n + 1):
        if n % f == 0 and f * unit <= cap:
            best = f
    return best * unit


def kernel(x, meta_tokens, rel_bias, norm1_w, w_in, shift_mu, decay_w0, decay_up, aaa_a0, aaa_up, gate_up, k_k, k_a, r_k, lnx_w, lnx_b, q_norm_w, k_norm_w, sinks, w_out, norm2_w, w_up, conv_w, conv_b, w_down):
    batch, seq, _ = x.shape
    assert norm1_w.shape[0] == 1 and seq % LEAD == 0
    lp = LEAD + seq
    layer = 0
    x2 = x.reshape(batch * seq, D_MODEL)
    meta = meta_tokens.astype(x.dtype)
    row2 = lambda t: t.reshape(1, -1)
    zeros64 = jnp.zeros((64, RWKV_DIM), F32)
    dec = jnp.concatenate([decay_up[layer], zeros64], axis=0)
    dec_hi = dec.astype(BF16)
    dec_lo = (dec - dec_hi.astype(F32)).astype(BF16)
    (pr, q, kv), (w_out_b, w_up_b, w_down_b) = _inproj(
        x2, meta, row2(norm1_w[layer]), w_in[layer].astype(BF16),
        ((w_out[layer], 0), (w_up[layer], 1), (w_down[layer], 0)), batch, seq)
    y_rwkv = _rwkv(
        pr.reshape(batch, lp, RWKV_IN), row2(shift_mu[layer]), row2(decay_w0[layer]),
        jnp.concatenate([dec_hi, dec_hi, dec_lo], axis=0), row2(aaa_a0[layer]),
        jnp.concatenate([zeros64, aaa_up[layer]], axis=0).astype(BF16),
        gate_up[layer].astype(BF16), row2(k_k[layer]), row2(k_a[layer]),
        row2(lnx_w[layer]), row2(lnx_b[layer]), row2(r_k[layer]))
    y_attn = _attention(q, kv, rel_bias, sinks[layer].reshape(1, HEADS),
                        jnp.tile(q_norm_w[layer], 2).reshape(1, 128),
                        jnp.tile(k_norm_w[layer], 2).reshape(1, 128), batch, lp,
                        _row_tile(lp, 768))
    y_rwkv = y_rwkv.reshape(batch * lp, RWKV_DIM)
    lead = _ffn_lead(meta, y_rwkv, y_attn, w_out_b, row2(norm2_w[layer]), w_up_b, batch, seq)
    out = _ffn(x2, lead, y_rwkv, y_attn, w_out_b, row2(norm2_w[layer]), w_up_b, conv_w[layer],
               row2(conv_b[layer]), w_down_b, batch, seq)
    return out.reshape(batch, seq, D_MODEL)
```

```python
import math

import jax
import jax.numpy as jnp
from jax import lax
from jax.experimental import pallas as pl
from jax.experimental.pallas import tpu as pltpu

F32 = jnp.float32
BF16 = jnp.bfloat16

D_MODEL = 1024
N_META = 16
BLOCK = 128
LEAD = 2 * BLOCK
PAD = LEAD - N_META
META_BLOCK = 1
HEADS = 8
HEAD_DIM = 64
RWKV_DIM = 512
RWKV_IN = 1792
ATTN_DIM = 512
KV_DIM = 128
IN_DIM = 2560
D_FF = 2816
CHUNK = 64
CHUNKS_PER_STEP = 4
NORM_EPS = 1e-6
GN_EPS = HEAD_DIM * 1e-5
MASK_VALUE = -1e30
REL_BUCKETS = 32
REL_MAX_EXACT = 16
REL_MAX_DIST = 128
VMEM_LIMIT = 56 * 1024 * 1024


def _dot(a, b, precision=None):
    return lax.dot_general(a, b, (((1,), (0,)), ((), ())), precision=precision,
                           preferred_element_type=F32)


def _dot_nt(a, b, precision=None):
    return lax.dot_general(a, b, (((1,), (1,)), ((), ())), precision=precision,
                           preferred_element_type=F32)


def _bf(x):
    return x.astype(BF16)


def _dot_split(x, w_bf16):
    hi = _bf(x)
    lo = _bf(x - hi.astype(F32))
    return _dot(hi, w_bf16) + _dot(lo, w_bf16)


def _head_ones(n, dtype=F32):
    r = lax.broadcasted_iota(jnp.int32, (n, n), 0) // HEAD_DIM
    c = lax.broadcasted_iota(jnp.int32, (n, n), 1) // HEAD_DIM
    return (r == c).astype(dtype)


def _lead_or_x(is_lead, x_ref, meta_ref):
    lead = jnp.concatenate([jnp.zeros((PAD, D_MODEL), F32), meta_ref[...]], axis=0)
    return jnp.where(is_lead, lead, x_ref[...])


def _inproj_kernel(*refs):
    ncast = 3
    nsub = len(refs) - 6 - 2 * ncast
    x_refs, (meta_ref, nw_ref, w_ref) = refs[:nsub], refs[nsub:nsub + 3]
    cast_in = refs[nsub + 3:nsub + 3 + ncast]
    pr_ref, q_ref, kv_ref = refs[nsub + 3 + ncast:nsub + 6 + ncast]
    cast_out = refs[nsub + 6 + ncast:]
    for src, dst in zip(cast_in, cast_out):
        dst[...] = src[...].astype(BF16)
    first = pl.program_id(1) == 0
    x = jnp.concatenate([_lead_or_x(first if k == 0 else False, x_refs[k], meta_ref)
                         for k in range(nsub)], axis=0)
    ms = jnp.mean(x * x, axis=-1, keepdims=True)
    u = (x * lax.rsqrt(ms + NORM_EPS) * nw_ref[...]).astype(BF16)
    for j in range(IN_DIM // 256):
        c = _dot(u, w_ref[:, 256 * j:256 * (j + 1)])
        lo = 256 * j
        if lo < RWKV_IN:
            pr_ref[:, lo:lo + 256] = c
        elif lo < RWKV_IN + ATTN_DIM:
            q_ref[:, lo - RWKV_IN:lo - RWKV_IN + 256] = c
        else:
            kv_ref[...] = c


def _x_tile_spec(seq, nsub=1, k=0):
    nx = seq // LEAD
    return pl.BlockSpec((LEAD, D_MODEL),
                        lambda b, i: (b * nx + jnp.maximum(nsub * i + k - 1, 0), 0))


def _slice_spec(shape, axis, unit, nsteps, nt):
    size = shape[axis]
    blk = min(b for b in range(unit, size + 1, unit) if size % b == 0 and size // b <= nsteps)
    last = size // blk - 1
    block = tuple(blk if a == axis else s for a, s in enumerate(shape))
    step = lambda b, i: jnp.minimum(b * nt + i, last)
    return pl.BlockSpec(block, lambda b, i: tuple(step(b, i) if a == axis else 0
                                                  for a in range(len(shape))))


def _inproj(x2, meta, norm_w, w_in_bf16, later_weights, batch, seq):
    nlead = (LEAD + seq) // LEAD
    nsub = max(m for m in (3, 2, 1) if nlead % m == 0)
    tm = nsub * LEAD
    nt = nlead // nsub
    tp = batch * nlead * LEAD
    row = lambda b, i: (b * nt + i, 0)
    const = lambda b, i: (0, 0)
    cast_specs = [_slice_spec(w.shape, axis, 16 if axis == 0 else 128, batch * nt, nt)
                  for w, axis in later_weights]
    outs = pl.pallas_call(
        _inproj_kernel,
        grid=(batch, nt),
        in_specs=[_x_tile_spec(seq, nsub, k) for k in range(nsub)] + [
            pl.BlockSpec((N_META, D_MODEL), const),
            pl.BlockSpec((1, D_MODEL), const),
            pl.BlockSpec((D_MODEL, IN_DIM), const),
        ] + cast_specs,
        out_specs=[
            pl.BlockSpec((tm, RWKV_IN), row),
            pl.BlockSpec((tm, ATTN_DIM), row),
            pl.BlockSpec((tm, 2 * KV_DIM), row),
        ] + cast_specs,
        out_shape=[
            jax.ShapeDtypeStruct((tp, RWKV_IN), F32),
            jax.ShapeDtypeStruct((tp, ATTN_DIM), F32),
            jax.ShapeDtypeStruct((tp, 2 * KV_DIM), F32),
        ] + [jax.ShapeDtypeStruct(w.shape, BF16) for w, _ in later_weights],
        compiler_params=pltpu.CompilerParams(
            dimension_semantics=("arbitrary", "arbitrary"), vmem_limit_bytes=VMEM_LIMIT),
    )(*([x2] * nsub), meta, norm_w, w_in_bf16, *[w for w, _ in later_weights])
    return outs[:3], outs[3:]


def _blockdiag(z, lane_lo):
    zero = jnp.zeros_like(z)
    return jnp.concatenate([jnp.where(lane_lo, z, zero), jnp.where(lane_lo, zero, z)], axis=0)


def _rwkv_kernel(p_ref, halo_ref, mu_ref, w0_ref, decup_ref, a0_ref, aup_ref, gup_ref,
                 kk_ref, ka_ref, lnw_ref, lnb_ref, rk_ref, o_ref, s_ref):
    step = pl.program_id(0)
    nb, rows, _ = p_ref.shape
    c = CHUNK

    @pl.when(step == 0)
    def _():
        s_ref[...] = jnp.zeros_like(s_ref)

    ti = lax.broadcasted_iota(jnp.int32, (c, 128), 0)
    lane = lax.broadcasted_iota(jnp.int32, (c, 128), 1)
    si = lane % HEAD_DIM
    lane_lo = lane < HEAD_DIM
    strict = si < ti
    incl = si <= ti
    eye = si == ti
    eye_f = eye.astype(F32)
    blk_masks = [(ti // 8) == (si // 8)]
    size = 8
    while size < c:
        blk_masks.append(((ti // (2 * size)) == (si // (2 * size))) & ((ti // size) != (si // size)))
        size *= 2
    ones = _head_ones(128, BF16)
    row = lax.broadcasted_iota(jnp.int32, (rows, 1), 0)
    bd = lambda z: _blockdiag(z, lane_lo)

    def pair_t(z):
        zt = bd(z).T
        return zt[0:c, :] + zt[c:2 * c, :]

    npair = HEADS // 2

    def head_sums(x, split=False):
        n = x.shape[0]
        xs = jnp.concatenate([x[:, 128 * j:128 * (j + 1)] for j in range(npair)], axis=0)
        ys = _dot_split(xs, ones) if split else _dot(_bf(xs), ones)
        return jnp.concatenate([ys[n * j:n * (j + 1)] for j in range(npair)], axis=1)

    prevs = []
    for b in range(nb):
        prev = pltpu.roll(p_ref[b], shift=1, axis=0)
        prev = jnp.where(row == 0, halo_ref[b, 7:8, :], prev)
        prevs.append(jnp.where((row == 0) & (step == 0), 0.0, prev))
    p = jnp.concatenate([p_ref[b] for b in range(nb)], axis=0)
    ps = p + (jnp.concatenate(prevs, axis=0) - p) * mu_ref[...]
    r = ps[:, 0:512]
    k = ps[:, 512:1024]
    v = ps[:, 1024:1536]
    wa = ps[:, 1536:1664]
    gd = ps[:, 1664:1792]
    th = jnp.tanh(wa)
    th_hi = _bf(th)
    th_lo = _bf(th - th_hi.astype(F32))
    z = w0_ref[...] + _dot(jnp.concatenate([th_hi, th_lo, th_hi], axis=1), decup_ref[...])
    lw = -math.exp(-0.5) * jax.nn.sigmoid(z)
    a = jax.nn.sigmoid(a0_ref[...] + _dot(_bf(wa), aup_ref[...]))
    g = _dot(_bf(jax.nn.sigmoid(gd)), gup_ref[...])
    kk = k * kk_ref[...]
    kn = kk * lax.rsqrt(jnp.maximum(head_sums(kk * kk), 1e-24))
    bb = kn * a
    k = k * (1.0 + (a - 1.0) * ka_ref[...])
    bonus = head_sums(r * k * rk_ref[...]) * v

    rowc = lax.broadcasted_iota(jnp.int32, (c, 1), 0)
    chains = []
    for b in range(nb):
        for ch in range(rows // c):
            rs = slice(c * ch, c * (ch + 1))
            ra = slice(rows * b + c * ch, rows * b + c * (ch + 1))
            lwc = lw[ra]
            cum = lwc
            for sh in (1, 2, 4, 8, 16, 32):
                cum = cum + jnp.where(rowc >= sh, pltpu.roll(cum, shift=sh, axis=0), 0.0)
            cum_last = cum[c - 1:c, :]
            e_neg = jnp.exp(-cum)
            e_end = jnp.exp(cum_last - cum)
            rt = r[ra] * jnp.exp(cum)
            at = -kn[ra] * jnp.exp(cum - lwc)
            kt, bt = k[ra] * e_neg, bb[ra] * e_neg
            kh, bh = k[ra] * e_end, bb[ra] * e_end
            wc = jnp.exp(cum_last)
            for j in range(npair):
                sl = slice(128 * j, 128 * (j + 1))
                chains.append(dict(
                    b=b, ch=ch, j=j, rs=rs, sl=sl, rt=rt[:, sl], at=_bf(at[:, sl]),
                    rhs=jnp.concatenate([bd(_bf(bt[:, sl])), bd(_bf(kt[:, sl]))], axis=0),
                    vbd=bd(_bf(v[ra, sl])),
                    bht=_bf(pair_t(bh[:, sl])), kht=_bf(pair_t(kh[:, sl])),
                    wc=wc[:, sl], bonus=bonus[ra, sl], g=g[ra, sl]))

    def s_products(cd):
        cd["pm"] = _dot_nt(jnp.concatenate([cd["at"], _bf(cd["rt"])], axis=0), cd.pop("rhs"))

    def s_masks(cd):
        pm = cd.pop("pm")
        cd["a_ab"] = jnp.where(strict, pm[0:c, 0:128], 0.0)
        cd["p_rb"] = _bf(jnp.where(incl, pm[c:2 * c, 0:128], 0.0))
        a_ak = _bf(jnp.where(strict, pm[0:c, 128:256], 0.0))
        p_rk = _bf(jnp.where(incl, pm[c:2 * c, 128:256], 0.0))
        cd["xv"] = _dot(jnp.concatenate([a_ak, p_rk, cd.pop("kht")], axis=0), cd.pop("vbd"))

    def s_d2(cd):
        cd["d"] = jnp.where(blk_masks[0], cd["a_ab"], 0.0)
        db = _bf(cd["d"])
        cd["bdd"] = bd(db)
        cd["d2"] = _dot(db, cd["bdd"])

    def s_d34(cd):
        d2b = _bf(cd["d2"])
        cd["d34"] = _dot(d2b, jnp.concatenate([cd.pop("bdd"), bd(d2b)], axis=1))

    def s_t0(cd):
        d34 = cd.pop("d34")
        s3 = eye_f + cd.pop("d") + cd.pop("d2") + d34[:, 0:128]
        cd["t"] = s3 + _dot(_bf(d34[:, 128:256]), bd(_bf(s3)))

    def s_inner(level):
        def run(cd):
            cd["tb"] = _bf(cd["t"])
            cd["inner"] = _bf(_dot(_bf(jnp.where(blk_masks[level], cd["a_ab"], 0.0)), bd(cd["tb"])))
        return run

    def s_merge(cd):
        cd["t"] = cd["t"] + _dot(cd.pop("tb"), bd(cd.pop("inner")))

    def s_gu(cd):
        cd.pop("a_ab")
        gu = _bf(_dot(_bf(cd.pop("t")),
                      jnp.concatenate([bd(cd.pop("at")), bd(_bf(cd["xv"][0:c]))], axis=1)))
        cd["gu"] = jnp.concatenate([bd(gu[:, 0:128]), bd(gu[:, 128:256])], axis=1)

    def s_maps(cd):
        r = _dot(jnp.concatenate([cd.pop("p_rb"), cd.pop("bht")], axis=0), cd.pop("gu"))
        xv = cd.pop("xv")
        cd["y0"] = r[0:c, 128:256] + xv[c:2 * c]
        cd["n_add"] = r[c:2 * c, 128:256] + xv[2 * c:3 * c]
        cd["mq_lhs"] = _bf(jnp.concatenate([r[c:2 * c, 0:128], cd.pop("rt") + r[0:c, 0:128]], axis=0))

    stages = [s_products, s_masks, s_d2, s_d34, s_t0]
    for level in range(1, len(blk_masks)):
        stages += [s_inner(level), s_merge]
    stages += [s_gu, s_maps]
    for stage in stages:
        for cd in chains:
            stage(cd)
    wcc = _dot_split(jnp.concatenate(
        [jnp.where(eye, jnp.broadcast_to(cd["wc"], (c, 128)), 0.0) for cd in chains], axis=0), ones)
    for i, cd in enumerate(chains):
        cd["wc_col"] = wcc[c * i:c * (i + 1)]

    states = {(b, j): s_ref[b, j] for b in range(nb) for j in range(HEADS // 2)}
    for ch in range(rows // c):
        for cd in chains:
            if cd["ch"] != ch:
                continue
            st = states[(cd["b"], cd["j"])]
            mq = _dot(cd["mq_lhs"], bd(_bf(st)))
            states[(cd["b"], cd["j"])] = cd["wc_col"] * st + mq[0:c] + cd["n_add"]
            cd["y"] = mq[c:2 * c] + cd["y0"]
    for (b, j), st in states.items():
        s_ref[b, j] = st

    ys = jnp.concatenate([cd["y"] for cd in chains], axis=0)
    dy = ys - _dot_split(ys, ones) * (1.0 / HEAD_DIM)
    var = _dot(_bf(dy * dy), ones) * (1.0 / HEAD_DIM)
    zn = dy * lax.rsqrt(var + GN_EPS)
    for i, cd in enumerate(chains):
        sl = cd["sl"]
        yn = zn[c * i:c * (i + 1)] * lnw_ref[:, sl] + lnb_ref[:, sl]
        o_ref[cd["b"], cd["rs"], sl] = (yn + cd["bonus"]) * cd["g"]


def _rwkv(pr3, mu, w0, decup_pad, a0, aup_pad, gup, k_k, k_a, lnx_w, lnx_b, r_k):
    batch, lp, _ = pr3.shape
    rows = CHUNK * CHUNKS_PER_STEP
    const = lambda i: (0, 0)
    vec = pl.BlockSpec((1, RWKV_DIM), const)
    lora = pl.BlockSpec((128, RWKV_DIM), const)
    return pl.pallas_call(
        _rwkv_kernel,
        grid=(lp // rows,),
        in_specs=[
            pl.BlockSpec((batch, rows, RWKV_IN), lambda i: (0, i, 0)),
            pl.BlockSpec((batch, 8, RWKV_IN), lambda i: (0, jnp.maximum(i * (rows // 8) - 1, 0), 0)),
            pl.BlockSpec((1, RWKV_IN), const),
            vec, pl.BlockSpec((3 * 128, RWKV_DIM), const), vec, lora, lora, vec, vec, vec, vec, vec,
        ],
        out_specs=pl.BlockSpec((batch, rows, RWKV_DIM), lambda i: (0, i, 0)),
        out_shape=jax.ShapeDtypeStruct((batch, lp, RWKV_DIM), F32),
        scratch_shapes=[pltpu.VMEM((batch, HEADS // 2, CHUNK, 128), F32)],
        compiler_params=pltpu.CompilerParams(
            dimension_semantics=("arbitrary",), vmem_limit_bytes=VMEM_LIMIT),
    )(pr3, pr3, mu, w0, decup_pad, a0, aup_pad, gup, k_k, k_a, lnx_w, lnx_b, r_k)


def _t5_thresholds():
    n_log = REL_BUCKETS - REL_MAX_EXACT
    out = []
    for k in range(1, n_log):
        x = REL_MAX_EXACT * (REL_MAX_DIST / REL_MAX_EXACT) ** (k / n_log)
        assert min(x - math.floor(x), math.ceil(x) - x) > 1e-3
        out.append(math.ceil(x))
    return out


def _t5_bucket(d):
    d = jnp.maximum(d, 0)
    large = jnp.full(d.shape, REL_MAX_EXACT, jnp.int32)
    for t in _t5_thresholds():
        large = large + (d >= t).astype(jnp.int32)
    return jnp.where(d < REL_MAX_EXACT, d, large)


def _attn_kernel(rb_ref, sink_ref, q_ref, kv0_ref, kvp_ref, kvc_ref, qw_ref, kw_ref, o_ref, tbl_ref):
    n = pl.program_id(1)

    @pl.when((pl.program_id(0) == 0) & (n == 0))
    def _():
        q = lax.broadcasted_iota(jnp.int32, (BLOCK, 3 * BLOCK), 0)
        col = lax.broadcasted_iota(jnp.int32, (BLOCK, 3 * BLOCK), 1)
        is_meta = col < BLOCK
        for nn in range(3):
            d_meta = nn * BLOCK + q - col
            d = jnp.where(is_meta, d_meta, q + 2 * BLOCK - col)
            ok = is_meta & (col >= BLOCK - N_META) & (d_meta >= 0)
            if nn >= 2:
                ok = ok | ((col >= BLOCK) & (col < 2 * BLOCK) & (col - BLOCK > q))
            if nn >= 1:
                ok = ok | ((col >= 2 * BLOCK) & (col - 2 * BLOCK <= q))
            bucket = _t5_bucket(d)

            def per_head(h, carry):
                acc = jnp.zeros((BLOCK, 3 * BLOCK), F32)
                for bk in range(REL_BUCKETS):
                    acc = jnp.where(bucket == bk, rb_ref[bk, h], acc)
                tbl_ref[nn, h] = jnp.where(ok, acc, MASK_VALUE)
                return carry

            lax.fori_loop(0, HEADS, per_head, 0)

    ones = _head_ones(128, BF16)
    lane_lo = lax.broadcasted_iota(jnp.int32, (BLOCK, 128), 1) < HEAD_DIM
    lane_hi = jnp.logical_not(lane_lo)
    rr = lax.broadcasted_iota(jnp.int32, (128, 128), 0)
    cc = lax.broadcasted_iota(jnp.int32, (128, 128), 1)
    dup = [((rr // HEAD_DIM == g) & (rr % HEAD_DIM == cc % HEAD_DIM)).astype(BF16) for g in range(2)]

    def qk_norm(x, w):
        ms = _dot(_bf(x * x), ones) * (1.0 / HEAD_DIM)
        return x * lax.rsqrt(ms + NORM_EPS) * w

    nbk = q_ref.shape[0] // BLOCK
    grp = HEADS // 2
    qw = qw_ref[...] * HEAD_DIM ** -0.5

    kvb = [kv0_ref[...], kvp_ref[...]] + [kvc_ref[BLOCK * jb:BLOCK * (jb + 1), :] for jb in range(nbk)]
    kn = [_bf(qk_norm(x[:, 0:128], kw_ref[...])) for x in kvb]
    kd = [[_bf(_dot(x, dup[g])) for x in kn] for g in range(2)]
    vd = [[_bf(_dot(_bf(x[:, 128:256]), dup[g])) for x in kvb] for g in range(2)]
    qn = [[qk_norm(q_ref[BLOCK * jb:BLOCK * (jb + 1), 128 * c4:128 * (c4 + 1)], qw)
           for c4 in range(4)] for jb in range(nbk)]
    sinks = [jnp.concatenate([jnp.full((BLOCK, 128), sink_ref[0, grp * g + i], F32)
                              for i in range(grp)], axis=0) for g in range(2)]
    ones_cols = jnp.ones((3 * BLOCK, 128), BF16)

    def scores(c):
        jb, g = c["jb"], c["g"]
        lhs = jnp.concatenate(
            [jnp.where(lane_lo if e == 0 else lane_hi, qn[jb][2 * g + jj], 0.0)
             for jj in range(2) for e in range(2)], axis=0).astype(BF16)
        keys = jnp.concatenate([kd[g][0], kd[g][1 + jb], kd[g][2 + jb]], axis=0)
        tsel = jnp.clip(n * nbk + jb - META_BLOCK, 0, 2)
        tb = tbl_ref[tsel, pl.ds(grp * g, grp)].reshape(grp * BLOCK, 3 * BLOCK)
        c["s"] = _dot_nt(lhs, keys) + tb

    def row_max(c):
        x = c["s"]
        m3 = jnp.maximum(jnp.maximum(x[:, 0:128], x[:, 128:256]), x[:, 256:384])
        m = jnp.maximum(jnp.max(m3, axis=-1, keepdims=True), sinks[c["g"]][:, 0:1])
        c["mb"] = jnp.broadcast_to(m, (grp * BLOCK, 128))

    def exps(c):
        x, mm = c.pop("s"), c["mb"]
        c["ex"] = jnp.concatenate(
            [_bf(jnp.exp(x[:, 128 * i:128 * (i + 1)] - mm)) for i in range(3)], axis=1)

    def values(c):
        jb, g = c["jb"], c["g"]
        vals = jnp.concatenate([vd[g][0], vd[g][1 + jb], vd[g][2 + jb]], axis=0)
        od = _dot(c.pop("ex"), jnp.concatenate([vals, ones_cols], axis=1))
        den = od[:, 128:256] + jnp.exp(sinks[g] - c.pop("mb"))
        o = od[:, 0:128] * (1.0 / den)
        for jj in range(2):
            col = 2 * g + jj
            o_ref[BLOCK * jb:BLOCK * (jb + 1), 128 * col:128 * (col + 1)] = jnp.where(
                lane_lo, o[256 * jj:256 * jj + 128], o[256 * jj + 128:256 * jj + 256])

    stages = (scores, row_max, exps, values)
    chains = [dict(jb=jb, g=g) for jb in range(nbk) for g in range(2)]
    for w in range(len(chains) + len(stages) - 1):
        for ci, c in enumerate(chains):
            if 0 <= w - ci < len(stages):
                stages[w - ci](c)


def _attention(q, kv, rel_bias, sinks, q_norm_w2, k_norm_w2, batch, lp, tm):
    nb = lp // BLOCK
    nt = lp // tm
    nbk = tm // BLOCK
    return pl.pallas_call(
        _attn_kernel,
        grid=(batch, nt),
        in_specs=[
            pl.BlockSpec(memory_space=pltpu.SMEM),
            pl.BlockSpec(memory_space=pltpu.SMEM),
            pl.BlockSpec((tm, ATTN_DIM), lambda b, n: (b * nt + n, 0)),
            pl.BlockSpec((BLOCK, 2 * KV_DIM), lambda b, n: (b * nb + META_BLOCK, 0)),
            pl.BlockSpec((BLOCK, 2 * KV_DIM),
                         lambda b, n: (b * nb + jnp.maximum(n * nbk - 1, 0), 0)),
            pl.BlockSpec((tm, 2 * KV_DIM), lambda b, n: (b * nt + n, 0)),
            pl.BlockSpec((1, 128), lambda b, n: (0, 0)),
            pl.BlockSpec((1, 128), lambda b, n: (0, 0)),
        ],
        out_specs=pl.BlockSpec((tm, ATTN_DIM), lambda b, n: (b * nt + n, 0)),
        out_shape=jax.ShapeDtypeStruct(q.shape, F32),
        scratch_shapes=[pltpu.VMEM((3, HEADS, BLOCK, 3 * BLOCK), F32)],
        compiler_params=pltpu.CompilerParams(
            dimension_semantics=("arbitrary", "arbitrary"), vmem_limit_bytes=VMEM_LIMIT),
    )(rel_bias, sinks, q, kv, kv, kv, q_norm_w2, k_norm_w2)


def _mixed(h_in, yr, ya, wo_ref):
    return (h_in + _dot(_bf(yr), wo_ref[0:RWKV_DIM, :]) + _dot(_bf(ya), wo_ref[RWKV_DIM:, :]))


def _normed(h, nw_ref):
    ms = jnp.mean(h * h, axis=-1, keepdims=True)
    return _bf(h * lax.rsqrt(ms + NORM_EPS) * nw_ref[...])


FFN_SUB = 2


def _ffn_kernel(x_ref, *refs):
    yr_refs, ya_refs = refs[0:FFN_SUB], refs[FFN_SUB:2 * FFN_SUB]
    (meta_ref, yrm_ref, yam_ref, wo_ref, nw_ref, wup_ref, cw_ref, cb_ref, wdn_ref,
     o_ref, carry_ref, hbuf_ref) = refs[2 * FFN_SUB:]
    tm = LEAD
    cw = 256
    nj = D_FF // cw
    nslot = hbuf_ref.shape[0]

    @pl.when(pl.program_id(1) == 0)
    def _():
        h = _mixed(meta_ref[...], yrm_ref[...], yam_ref[...], wo_ref)
        carry_ref[...] = _dot(_normed(h, nw_ref), wup_ref[...])[N_META - 8:N_META, :]

    subs = range(FFN_SUB)
    hres = [_mixed(x_ref[tm * s:tm * (s + 1), :], yr_refs[s][...], ya_refs[s][...], wo_ref)
            for s in subs]
    u = [_normed(h, nw_ref) for h in hres]
    acc = [None for _ in subs]

    def up(s, j):
        return [_dot(u[s], wup_ref[:, base + cw * j:base + cw * (j + 1)]) for base in (0, D_FF)]

    items = [(s, j) for s in subs for j in range(nj)]
    hids = up(*items[0])
    for n, (s, j) in enumerate(items):
        nxt = up(*items[n + 1]) if n + 1 < len(items) else None
        halves = []
        for half, (base, hid) in enumerate(zip((0, D_FF), hids)):
            lo = base + cw * j
            buf = hbuf_ref.at[(2 * n + half) % nslot]
            buf[0:8, :] = carry_ref[:, lo:lo + cw]
            buf[8:8 + tm, :] = hid
            carry_ref[:, lo:lo + cw] = hid[tm - 8:tm, :]
            halves.append(hid * cw_ref[0:1, lo:lo + cw]
                          + buf[7:7 + tm, :] * cw_ref[1:2, lo:lo + cw]
                          + buf[6:6 + tm, :] * cw_ref[2:3, lo:lo + cw]
                          + cb_ref[:, lo:lo + cw])
        gate, val = halves
        act = (gate * jax.nn.sigmoid(gate) * val).astype(BF16)
        down = _dot(act, wdn_ref[cw * j:cw * (j + 1), :])
        acc[s] = down if acc[s] is None else acc[s] + down
        hids = nxt
        if j == nj - 1:
            o_ref[tm * s:tm * (s + 1), :] = hres[s] + acc[s]


def _ffn(x2, meta, yr, ya, w_out, norm_w, w_up, conv_w, conv_b, w_down, batch, seq):
    nlead = (LEAD + seq) // LEAD
    assert (nlead - 1) % FFN_SUB == 0
    meta_rows = lambda b, i: ((b * nlead + 1) * (LEAD // N_META) - 1, 0)
    tm = FFN_SUB * LEAD
    nt = seq // tm
    row = lambda b, i: (b * nt + i, 0)
    const = lambda b, i: (0, 0)
    sub = lambda width: [pl.BlockSpec((LEAD, width),
                                      lambda b, i, s=s: (b * nlead + 1 + FFN_SUB * i + s, 0))
                         for s in range(FFN_SUB)]
    return pl.pallas_call(
        _ffn_kernel,
        grid=(batch, nt),
        in_specs=[pl.BlockSpec((tm, D_MODEL), row)] + sub(RWKV_DIM) + sub(ATTN_DIM) + [
            pl.BlockSpec((N_META, D_MODEL), const),
            pl.BlockSpec((N_META, RWKV_DIM), meta_rows),
            pl.BlockSpec((N_META, ATTN_DIM), meta_rows),
            pl.BlockSpec((D_MODEL, D_MODEL), const),
            pl.BlockSpec((1, D_MODEL), const),
            pl.BlockSpec((D_MODEL, 2 * D_FF), const),
            pl.BlockSpec((3, 2 * D_FF), const),
            pl.BlockSpec((1, 2 * D_FF), const),
            pl.BlockSpec((D_FF, D_MODEL), const),
        ],
        out_specs=pl.BlockSpec((tm, D_MODEL), row),
        out_shape=jax.ShapeDtypeStruct(x2.shape, F32),
        scratch_shapes=[pltpu.VMEM((8, 2 * D_FF), F32),
                        pltpu.VMEM((4 * FFN_SUB, LEAD + 8, 256), F32)],
        compiler_params=pltpu.CompilerParams(
            dimension_semantics=("arbitrary", "arbitrary"), vmem_limit_bytes=VMEM_LIMIT),
    )(x2, *([yr] * FFN_SUB), *([ya] * FFN_SUB), meta, yr, ya, w_out, norm_w, w_up, conv_w, conv_b,
      w_down)


def _row_tile(lp, cap, unit=BLOCK):
    n = lp // unit
    best = 1
    for f in range(1, n + 1):
        if n % f == 0 and f * unit <= cap:
            best = f
    return best * unit


def kernel(x, meta_tokens, rel_bias, norm1_w, w_in, shift_mu, decay_w0, decay_up, aaa_a0, aaa_up, gate_up, k_k, k_a, r_k, lnx_w, lnx_b, q_norm_w, k_norm_w, sinks, w_out, norm2_w, w_up, conv_w, conv_b, w_down):
    batch, seq, _ = x.shape
    assert norm1_w.shape[0] == 1 and seq % LEAD == 0
    lp = LEAD + seq
    layer = 0
    x2 = x.reshape(batch * seq, D_MODEL)
    meta = meta_tokens.astype(x.dtype)
    row2 = lambda t: t.reshape(1, -1)
    zeros64 = jnp.zeros((64, RWKV_DIM), F32)
    dec = jnp.concatenate([decay_up[layer], zeros64], axis=0)
    dec_hi = dec.astype(BF16)
    dec_lo = (dec - dec_hi.astype(F32)).astype(BF16)
    (pr, q, kv), (w_out_b, w_up_b, w_down_b) = _inproj(
        x2, meta, row2(norm1_w[layer]), w_in[layer].astype(BF16),
        ((w_out[layer], 0), (w_up[layer], 1), (w_down[layer], 0)), batch, seq)
    y_rwkv = _rwkv(
        pr.reshape(batch, lp, RWKV_IN), row2(shift_mu[layer]), row2(decay_w0[layer]),
        jnp.concatenate([dec_hi, dec_hi, dec_lo], axis=0), row2(aaa_a0[layer]),
        jnp.concatenate([zeros64, aaa_up[layer]], axis=0).astype(BF16),
        gate_up[layer].astype(BF16), row2(k_k[layer]), row2(k_a[layer]),
        row2(lnx_w[layer]), row2(lnx_b[layer]), row2(r_k[layer]))
    y_attn = _attention(q, kv, rel_bias, sinks[layer].reshape(1, HEADS),
                        jnp.tile(q_norm_w[layer], 2).reshape(1, 128),
                        jnp.tile(k_norm_w[layer], 2).reshape(1, 128), batch, lp,
                        _row_tile(lp, 768))
    y_rwkv = y_rwkv.reshape(batch * lp, RWKV_DIM)
    out = _ffn(x2, meta, y_rwkv, y_attn, w_out_b, row2(norm2_w[layer]), w_up_b, conv_w[layer],
               row2(conv_b[layer]), w_down_b, batch, seq)
    return out.reshape(batch, seq, D_MODEL)
```

```python
import math

import jax
import jax.numpy as jnp
from jax import lax
from jax.experimental import pallas as pl
from jax.experimental.pallas import tpu as pltpu

F32 = jnp.float32
BF16 = jnp.bfloat16

D_MODEL = 1024
N_META = 16
BLOCK = 128
LEAD = 2 * BLOCK
PAD = LEAD - N_META
META_BLOCK = 1
HEADS = 8
HEAD_DIM = 64
RWKV_DIM = 512
RWKV_IN = 1792
ATTN_DIM = 512
KV_DIM = 128
IN_DIM = 2560
D_FF = 2816
CHUNK = 64
CHUNKS_PER_STEP = 4
NORM_EPS = 1e-6
GN_EPS = HEAD_DIM * 1e-5
MASK_VALUE = -1e30
REL_BUCKETS = 32
REL_MAX_EXACT = 16
REL_MAX_DIST = 128
VMEM_LIMIT = 56 * 1024 * 1024


def _dot(a, b, precision=None):
    return lax.dot_general(a, b, (((1,), (0,)), ((), ())), precision=precision,
                           preferred_element_type=F32)


def _dot_nt(a, b, precision=None):
    return lax.dot_general(a, b, (((1,), (1,)), ((), ())), precision=precision,
                           preferred_element_type=F32)


def _bf(x):
    return x.astype(BF16)


def _dot_split(x, w_bf16):
    hi = _bf(x)
    lo = _bf(x - hi.astype(F32))
    return _dot(hi, w_bf16) + _dot(lo, w_bf16)


def _head_ones(n, dtype=F32):
    r = lax.broadcasted_iota(jnp.int32, (n, n), 0) // HEAD_DIM
    c = lax.broadcasted_iota(jnp.int32, (n, n), 1) // HEAD_DIM
    return (r == c).astype(dtype)


def _lead_or_x(is_lead, x_ref, meta_ref):
    lead = jnp.concatenate([jnp.zeros((PAD, D_MODEL), F32), meta_ref[...]], axis=0)
    return jnp.where(is_lead, lead, x_ref[...])


def _inproj_kernel(*refs):
    ncast = 3
    nsub = len(refs) - 6 - 2 * ncast
    x_refs, (meta_ref, nw_ref, w_ref) = refs[:nsub], refs[nsub:nsub + 3]
    cast_in = refs[nsub + 3:nsub + 3 + ncast]
    pr_ref, q_ref, kv_ref = refs[nsub + 3 + ncast:nsub + 6 + ncast]
    cast_out = refs[nsub + 6 + ncast:]
    for src, dst in zip(cast_in, cast_out):
        dst[...] = src[...].astype(BF16)
    first = pl.program_id(1) == 0
    x = jnp.concatenate([_lead_or_x(first if k == 0 else False, x_refs[k], meta_ref)
                         for k in range(nsub)], axis=0)
    ms = jnp.mean(x * x, axis=-1, keepdims=True)
    u = (x * lax.rsqrt(ms + NORM_EPS) * nw_ref[...]).astype(BF16)
    for j in range(IN_DIM // 256):
        c = _dot(u, w_ref[:, 256 * j:256 * (j + 1)])
        lo = 256 * j
        if lo < RWKV_IN:
            pr_ref[:, lo:lo + 256] = c.astype(pr_ref.dtype)
        elif lo < RWKV_IN + ATTN_DIM:
            q_ref[:, lo - RWKV_IN:lo - RWKV_IN + 256] = c.astype(q_ref.dtype)
        else:
            kv_ref[...] = c.astype(kv_ref.dtype)


def _x_tile_spec(seq, nsub=1, k=0):
    nx = seq // LEAD
    return pl.BlockSpec((LEAD, D_MODEL),
                        lambda b, i: (b * nx + jnp.maximum(nsub * i + k - 1, 0), 0))


def _slice_spec(shape, axis, unit, nsteps, nt):
    size = shape[axis]
    blk = min(b for b in range(unit, size + 1, unit) if size % b == 0 and size // b <= nsteps)
    last = size // blk - 1
    block = tuple(blk if a == axis else s for a, s in enumerate(shape))
    step = lambda b, i: jnp.minimum(b * nt + i, last)
    return pl.BlockSpec(block, lambda b, i: tuple(step(b, i) if a == axis else 0
                                                  for a in range(len(shape))))


def _inproj(x2, meta, norm_w, w_in_bf16, later_weights, batch, seq):
    nlead = (LEAD + seq) // LEAD
    nsub = max(m for m in (3, 2, 1) if nlead % m == 0)
    tm = nsub * LEAD
    nt = nlead // nsub
    tp = batch * nlead * LEAD
    row = lambda b, i: (b * nt + i, 0)
    const = lambda b, i: (0, 0)
    cast_specs = [_slice_spec(w.shape, axis, 16 if axis == 0 else 128, batch * nt, nt)
                  for w, axis in later_weights]
    outs = pl.pallas_call(
        _inproj_kernel,
        grid=(batch, nt),
        in_specs=[_x_tile_spec(seq, nsub, k) for k in range(nsub)] + [
            pl.BlockSpec((N_META, D_MODEL), const),
            pl.BlockSpec((1, D_MODEL), const),
            pl.BlockSpec((D_MODEL, IN_DIM), const),
        ] + cast_specs,
        out_specs=[
            pl.BlockSpec((tm, RWKV_IN), row),
            pl.BlockSpec((tm, ATTN_DIM), row),
            pl.BlockSpec((tm, 2 * KV_DIM), row),
        ] + cast_specs,
        out_shape=[
            jax.ShapeDtypeStruct((tp, RWKV_IN), BF16),
            jax.ShapeDtypeStruct((tp, ATTN_DIM), BF16),
            jax.ShapeDtypeStruct((tp, 2 * KV_DIM), BF16),
        ] + [jax.ShapeDtypeStruct(w.shape, BF16) for w, _ in later_weights],
        compiler_params=pltpu.CompilerParams(
            dimension_semantics=("arbitrary", "arbitrary"), vmem_limit_bytes=VMEM_LIMIT),
    )(*([x2] * nsub), meta, norm_w, w_in_bf16, *[w for w, _ in later_weights])
    return outs[:3], outs[3:]


def _blockdiag(z, lane_lo):
    zero = jnp.zeros_like(z)
    return jnp.concatenate([jnp.where(lane_lo, z, zero), jnp.where(lane_lo, zero, z)], axis=0)


def _rwkv_kernel(p_ref, halo_ref, mu_ref, w0_ref, decup_ref, a0_ref, aup_ref, gup_ref,
                 kk_ref, ka_ref, lnw_ref, lnb_ref, rk_ref, o_ref, s_ref):
    step = pl.program_id(0)
    nb, rows, _ = p_ref.shape
    c = CHUNK

    @pl.when(step == 0)
    def _():
        s_ref[...] = jnp.zeros_like(s_ref)

    ti = lax.broadcasted_iota(jnp.int32, (c, 128), 0)
    lane = lax.broadcasted_iota(jnp.int32, (c, 128), 1)
    si = lane % HEAD_DIM
    lane_lo = lane < HEAD_DIM
    strict = si < ti
    incl = si <= ti
    eye = si == ti
    eye_f = eye.astype(F32)
    blk_masks = [(ti // 8) == (si // 8)]
    size = 8
    while size < c:
        blk_masks.append(((ti // (2 * size)) == (si // (2 * size))) & ((ti // size) != (si // size)))
        size *= 2
    ones = _head_ones(128, BF16)
    row = lax.broadcasted_iota(jnp.int32, (rows, 1), 0)
    bd = lambda z: _blockdiag(z, lane_lo)

    def pair_t(z):
        zt = bd(z).T
        return zt[0:c, :] + zt[c:2 * c, :]

    npair = HEADS // 2

    def head_sums(x, split=False):
        n = x.shape[0]
        xs = jnp.concatenate([x[:, 128 * j:128 * (j + 1)] for j in range(npair)], axis=0)
        ys = _dot_split(xs, ones) if split else _dot(_bf(xs), ones)
        return jnp.concatenate([ys[n * j:n * (j + 1)] for j in range(npair)], axis=1)

    prevs = []
    for b in range(nb):
        prev = pltpu.roll(p_ref[b].astype(F32), shift=1, axis=0)
        last = halo_ref.shape[1] - 1
        prev = jnp.where(row == 0, halo_ref[b, last:last + 1, :].astype(F32), prev)
        prevs.append(jnp.where((row == 0) & (step == 0), 0.0, prev))
    p = jnp.concatenate([p_ref[b].astype(F32) for b in range(nb)], axis=0)
    ps = p + (jnp.concatenate(prevs, axis=0) - p) * mu_ref[...]
    r = ps[:, 0:512]
    k = ps[:, 512:1024]
    v = ps[:, 1024:1536]
    wa = ps[:, 1536:1664]
    gd = ps[:, 1664:1792]
    th = jnp.tanh(wa)
    th_hi = _bf(th)
    th_lo = _bf(th - th_hi.astype(F32))
    z = w0_ref[...] + _dot(jnp.concatenate([th_hi, th_lo, th_hi], axis=1), decup_ref[...])
    lw = -math.exp(-0.5) * jax.nn.sigmoid(z)
    a = jax.nn.sigmoid(a0_ref[...] + _dot(_bf(wa), aup_ref[...]))
    g = _dot(_bf(jax.nn.sigmoid(gd)), gup_ref[...])
    kk = k * kk_ref[...]
    kn = kk * lax.rsqrt(jnp.maximum(head_sums(kk * kk), 1e-24))
    bb = kn * a
    k = k * (1.0 + (a - 1.0) * ka_ref[...])
    bonus = head_sums(r * k * rk_ref[...]) * v

    rowc = lax.broadcasted_iota(jnp.int32, (c, 1), 0)
    chains = []
    for b in range(nb):
        for ch in range(rows // c):
            rs = slice(c * ch, c * (ch + 1))
            ra = slice(rows * b + c * ch, rows * b + c * (ch + 1))
            lwc = lw[ra]
            cum = lwc
            for sh in (1, 2, 4, 8, 16, 32):
                cum = cum + jnp.where(rowc >= sh, pltpu.roll(cum, shift=sh, axis=0), 0.0)
            cum_last = cum[c - 1:c, :]
            e_neg = jnp.exp(-cum)
            e_end = jnp.exp(cum_last - cum)
            rt = r[ra] * jnp.exp(cum)
            at = -kn[ra] * jnp.exp(cum - lwc)
            kt, bt = k[ra] * e_neg, bb[ra] * e_neg
            kh, bh = k[ra] * e_end, bb[ra] * e_end
            wc = jnp.exp(cum_last)
            for j in range(npair):
                sl = slice(128 * j, 128 * (j + 1))
                chains.append(dict(
                    b=b, ch=ch, j=j, rs=rs, sl=sl, rt=rt[:, sl], at=_bf(at[:, sl]),
                    rhs=jnp.concatenate([bd(_bf(bt[:, sl])), bd(_bf(kt[:, sl]))], axis=0),
                    vbd=bd(_bf(v[ra, sl])),
                    bht=_bf(pair_t(bh[:, sl])), kht=_bf(pair_t(kh[:, sl])),
                    wc=wc[:, sl], bonus=bonus[ra, sl], g=g[ra, sl]))

    def s_products(cd):
        cd["pm"] = _dot_nt(jnp.concatenate([cd["at"], _bf(cd["rt"])], axis=0), cd.pop("rhs"))

    def s_masks(cd):
        pm = cd.pop("pm")
        cd["a_ab"] = jnp.where(strict, pm[0:c, 0:128], 0.0)
        cd["p_rb"] = _bf(jnp.where(incl, pm[c:2 * c, 0:128], 0.0))
        a_ak = _bf(jnp.where(strict, pm[0:c, 128:256], 0.0))
        p_rk = _bf(jnp.where(incl, pm[c:2 * c, 128:256], 0.0))
        cd["xv"] = _dot(jnp.concatenate([a_ak, p_rk, cd.pop("kht")], axis=0), cd.pop("vbd"))

    def s_d2(cd):
        cd["d"] = jnp.where(blk_masks[0], cd["a_ab"], 0.0)
        db = _bf(cd["d"])
        cd["bdd"] = bd(db)
        cd["d2"] = _dot(db, cd["bdd"])

    def s_d34(cd):
        d2b = _bf(cd["d2"])
        cd["d34"] = _dot(d2b, jnp.concatenate([cd.pop("bdd"), bd(d2b)], axis=1))

    def s_t0(cd):
        d34 = cd.pop("d34")
        s3 = eye_f + cd.pop("d") + cd.pop("d2") + d34[:, 0:128]
        cd["t"] = s3 + _dot(_bf(d34[:, 128:256]), bd(_bf(s3)))

    def s_inner(level):
        def run(cd):
            cd["tb"] = _bf(cd["t"])
            cd["inner"] = _bf(_dot(_bf(jnp.where(blk_masks[level], cd["a_ab"], 0.0)), bd(cd["tb"])))
        return run

    def s_merge(cd):
        cd["t"] = cd["t"] + _dot(cd.pop("tb"), bd(cd.pop("inner")))

    def s_gu(cd):
        cd.pop("a_ab")
        gu = _bf(_dot(_bf(cd.pop("t")),
                      jnp.concatenate([bd(cd.pop("at")), bd(_bf(cd["xv"][0:c]))], axis=1)))
        cd["gu"] = jnp.concatenate([bd(gu[:, 0:128]), bd(gu[:, 128:256])], axis=1)

    def s_maps(cd):
        r = _dot(jnp.concatenate([cd.pop("p_rb"), cd.pop("bht")], axis=0), cd.pop("gu"))
        xv = cd.pop("xv")
        cd["y0"] = r[0:c, 128:256] + xv[c:2 * c]
        cd["n_add"] = r[c:2 * c, 128:256] + xv[2 * c:3 * c]
        cd["mq_lhs"] = _bf(jnp.concatenate([r[c:2 * c, 0:128], cd.pop("rt") + r[0:c, 0:128]], axis=0))

    stages = [s_products, s_masks, s_d2, s_d34, s_t0]
    for level in range(1, len(blk_masks)):
        stages += [s_inner(level), s_merge]
    stages += [s_gu, s_maps]
    for stage in stages:
        for cd in chains:
            stage(cd)
    wcc = _dot_split(jnp.concatenate(
        [jnp.where(eye, jnp.broadcast_to(cd["wc"], (c, 128)), 0.0) for cd in chains], axis=0), ones)
    for i, cd in enumerate(chains):
        cd["wc_col"] = wcc[c * i:c * (i + 1)]

    states = {(b, j): s_ref[b, j] for b in range(nb) for j in range(HEADS // 2)}
    for ch in range(rows // c):
        for cd in chains:
            if cd["ch"] != ch:
                continue
            st = states[(cd["b"], cd["j"])]
            mq = _dot(cd["mq_lhs"], bd(_bf(st)))
            states[(cd["b"], cd["j"])] = cd["wc_col"] * st + mq[0:c] + cd["n_add"]
            cd["y"] = mq[c:2 * c] + cd["y0"]
    for (b, j), st in states.items():
        s_ref[b, j] = st

    ys = jnp.concatenate([cd["y"] for cd in chains], axis=0)
    dy = ys - _dot_split(ys, ones) * (1.0 / HEAD_DIM)
    var = _dot(_bf(dy * dy), ones) * (1.0 / HEAD_DIM)
    zn = dy * lax.rsqrt(var + GN_EPS)
    for i, cd in enumerate(chains):
        sl = cd["sl"]
        yn = zn[c * i:c * (i + 1)] * lnw_ref[:, sl] + lnb_ref[:, sl]
        o_ref[cd["b"], cd["rs"], sl] = ((yn + cd["bonus"]) * cd["g"]).astype(o_ref.dtype)


def _rwkv(pr3, mu, w0, decup_pad, a0, aup_pad, gup, k_k, k_a, lnx_w, lnx_b, r_k):
    batch, lp, _ = pr3.shape
    rows = CHUNK * CHUNKS_PER_STEP
    const = lambda i: (0, 0)
    vec = pl.BlockSpec((1, RWKV_DIM), const)
    lora = pl.BlockSpec((128, RWKV_DIM), const)
    return pl.pallas_call(
        _rwkv_kernel,
        grid=(lp // rows,),
        in_specs=[
            pl.BlockSpec((batch, rows, RWKV_IN), lambda i: (0, i, 0)),
            pl.BlockSpec((batch, 16, RWKV_IN),
                         lambda i: (0, jnp.maximum(i * (rows // 16) - 1, 0), 0)),
            pl.BlockSpec((1, RWKV_IN), const),
            vec, pl.BlockSpec((3 * 128, RWKV_DIM), const), vec, lora, lora, vec, vec, vec, vec, vec,
        ],
        out_specs=pl.BlockSpec((batch, rows, RWKV_DIM), lambda i: (0, i, 0)),
        out_shape=jax.ShapeDtypeStruct((batch, lp, RWKV_DIM), BF16),
        scratch_shapes=[pltpu.VMEM((batch, HEADS // 2, CHUNK, 128), F32)],
        compiler_params=pltpu.CompilerParams(
            dimension_semantics=("arbitrary",), vmem_limit_bytes=VMEM_LIMIT),
    )(pr3, pr3, mu, w0, decup_pad, a0, aup_pad, gup, k_k, k_a, lnx_w, lnx_b, r_k)


def _t5_thresholds():
    n_log = REL_BUCKETS - REL_MAX_EXACT
    out = []
    for k in range(1, n_log):
        x = REL_MAX_EXACT * (REL_MAX_DIST / REL_MAX_EXACT) ** (k / n_log)
        assert min(x - math.floor(x), math.ceil(x) - x) > 1e-3
        out.append(math.ceil(x))
    return out


def _t5_bucket(d):
    d = jnp.maximum(d, 0)
    large = jnp.full(d.shape, REL_MAX_EXACT, jnp.int32)
    for t in _t5_thresholds():
        large = large + (d >= t).astype(jnp.int32)
    return jnp.where(d < REL_MAX_EXACT, d, large)


def _attn_kernel(rb_ref, sink_ref, q_ref, kv0_ref, kvp_ref, kvc_ref, qw_ref, kw_ref, o_ref, tbl_ref):
    n = pl.program_id(1)

    @pl.when((pl.program_id(0) == 0) & (n == 0))
    def _():
        q = lax.broadcasted_iota(jnp.int32, (BLOCK, 3 * BLOCK), 0)
        col = lax.broadcasted_iota(jnp.int32, (BLOCK, 3 * BLOCK), 1)
        is_meta = col < BLOCK
        for nn in range(3):
            d_meta = nn * BLOCK + q - col
            d = jnp.where(is_meta, d_meta, q + 2 * BLOCK - col)
            ok = is_meta & (col >= BLOCK - N_META) & (d_meta >= 0)
            if nn >= 2:
                ok = ok | ((col >= BLOCK) & (col < 2 * BLOCK) & (col - BLOCK > q))
            if nn >= 1:
                ok = ok | ((col >= 2 * BLOCK) & (col - 2 * BLOCK <= q))
            bucket = _t5_bucket(d)

            def per_head(h, carry):
                acc = jnp.zeros((BLOCK, 3 * BLOCK), F32)
                for bk in range(REL_BUCKETS):
                    acc = jnp.where(bucket == bk, rb_ref[bk, h], acc)
                tbl_ref[nn, h] = jnp.where(ok, acc, MASK_VALUE)
                return carry

            lax.fori_loop(0, HEADS, per_head, 0)

    ones = _head_ones(128, BF16)
    lane_lo = lax.broadcasted_iota(jnp.int32, (BLOCK, 128), 1) < HEAD_DIM
    lane_hi = jnp.logical_not(lane_lo)
    rr = lax.broadcasted_iota(jnp.int32, (128, 128), 0)
    cc = lax.broadcasted_iota(jnp.int32, (128, 128), 1)
    dup = [((rr // HEAD_DIM == g) & (rr % HEAD_DIM == cc % HEAD_DIM)).astype(BF16) for g in range(2)]

    def qk_norm(x, w):
        ms = _dot(_bf(x * x), ones) * (1.0 / HEAD_DIM)
        return x * lax.rsqrt(ms + NORM_EPS) * w

    nbk = q_ref.shape[0] // BLOCK
    grp = HEADS // 2
    qw = qw_ref[...] * HEAD_DIM ** -0.5

    kvb = [kv0_ref[...], kvp_ref[...]] + [kvc_ref[BLOCK * jb:BLOCK * (jb + 1), :] for jb in range(nbk)]
    kvb = [x.astype(F32) for x in kvb]
    kn = [_bf(qk_norm(x[:, 0:128], kw_ref[...])) for x in kvb]
    kd = [[_bf(_dot(x, dup[g])) for x in kn] for g in range(2)]
    vd = [[_bf(_dot(_bf(x[:, 128:256]), dup[g])) for x in kvb] for g in range(2)]
    qn = [[qk_norm(q_ref[BLOCK * jb:BLOCK * (jb + 1), 128 * c4:128 * (c4 + 1)].astype(F32), qw)
           for c4 in range(4)] for jb in range(nbk)]
    sinks = [jnp.concatenate([jnp.full((BLOCK, 128), sink_ref[0, grp * g + i], F32)
                              for i in range(grp)], axis=0) for g in range(2)]
    ones_cols = jnp.ones((3 * BLOCK, 128), BF16)

    def scores(c):
        jb, g = c["jb"], c["g"]
        lhs = jnp.concatenate(
            [jnp.where(lane_lo if e == 0 else lane_hi, qn[jb][2 * g + jj], 0.0)
             for jj in range(2) for e in range(2)], axis=0).astype(BF16)
        keys = jnp.concatenate([kd[g][0], kd[g][1 + jb], kd[g][2 + jb]], axis=0)
        tsel = jnp.clip(n * nbk + jb - META_BLOCK, 0, 2)
        tb = tbl_ref[tsel, pl.ds(grp * g, grp)].reshape(grp * BLOCK, 3 * BLOCK)
        c["s"] = _dot_nt(lhs, keys) + tb

    def row_max(c):
        x = c["s"]
        m3 = jnp.maximum(jnp.maximum(x[:, 0:128], x[:, 128:256]), x[:, 256:384])
        m = jnp.maximum(jnp.max(m3, axis=-1, keepdims=True), sinks[c["g"]][:, 0:1])
        c["mb"] = jnp.broadcast_to(m, (grp * BLOCK, 128))

    def exps(c):
        x, mm = c.pop("s"), c["mb"]
        c["ex"] = jnp.concatenate(
            [_bf(jnp.exp(x[:, 128 * i:128 * (i + 1)] - mm)) for i in range(3)], axis=1)

    def values(c):
        jb, g = c["jb"], c["g"]
        vals = jnp.concatenate([vd[g][0], vd[g][1 + jb], vd[g][2 + jb]], axis=0)
        od = _dot(c.pop("ex"), jnp.concatenate([vals, ones_cols], axis=1))
        den = od[:, 128:256] + jnp.exp(sinks[g] - c.pop("mb"))
        o = od[:, 0:128] * (1.0 / den)
        for jj in range(2):
            col = 2 * g + jj
            o_ref[BLOCK * jb:BLOCK * (jb + 1), 128 * col:128 * (col + 1)] = jnp.where(
                lane_lo, o[256 * jj:256 * jj + 128], o[256 * jj + 128:256 * jj + 256]
            ).astype(o_ref.dtype)

    stages = (scores, row_max, exps, values)
    chains = [dict(jb=jb, g=g) for jb in range(nbk) for g in range(2)]
    for w in range(len(chains) + len(stages) - 1):
        for ci, c in enumerate(chains):
            if 0 <= w - ci < len(stages):
                stages[w - ci](c)


def _attention(q, kv, rel_bias, sinks, q_norm_w2, k_norm_w2, batch, lp, tm):
    nb = lp // BLOCK
    nt = lp // tm
    nbk = tm // BLOCK
    return pl.pallas_call(
        _attn_kernel,
        grid=(batch, nt),
        in_specs=[
            pl.BlockSpec(memory_space=pltpu.SMEM),
            pl.BlockSpec(memory_space=pltpu.SMEM),
            pl.BlockSpec((tm, ATTN_DIM), lambda b, n: (b * nt + n, 0)),
            pl.BlockSpec((BLOCK, 2 * KV_DIM), lambda b, n: (b * nb + META_BLOCK, 0)),
            pl.BlockSpec((BLOCK, 2 * KV_DIM),
                         lambda b, n: (b * nb + jnp.maximum(n * nbk - 1, 0), 0)),
            pl.BlockSpec((tm, 2 * KV_DIM), lambda b, n: (b * nt + n, 0)),
            pl.BlockSpec((1, 128), lambda b, n: (0, 0)),
            pl.BlockSpec((1, 128), lambda b, n: (0, 0)),
        ],
        out_specs=pl.BlockSpec((tm, ATTN_DIM), lambda b, n: (b * nt + n, 0)),
        out_shape=jax.ShapeDtypeStruct(q.shape, BF16),
        scratch_shapes=[pltpu.VMEM((3, HEADS, BLOCK, 3 * BLOCK), F32)],
        compiler_params=pltpu.CompilerParams(
            dimension_semantics=("arbitrary", "arbitrary"), vmem_limit_bytes=VMEM_LIMIT),
    )(rel_bias, sinks, q, kv, kv, kv, q_norm_w2, k_norm_w2)


def _mixed(h_in, yr, ya, wo_ref):
    return (h_in + _dot(_bf(yr), wo_ref[0:RWKV_DIM, :]) + _dot(_bf(ya), wo_ref[RWKV_DIM:, :]))


def _normed(h, nw_ref):
    ms = jnp.mean(h * h, axis=-1, keepdims=True)
    return _bf(h * lax.rsqrt(ms + NORM_EPS) * nw_ref[...])


FFN_SUB = 2


def _ffn_kernel(x_ref, *refs):
    yr_refs, ya_refs = refs[0:FFN_SUB], refs[FFN_SUB:2 * FFN_SUB]
    (meta_ref, yrm_ref, yam_ref, wo_ref, nw_ref, wup_ref, cw_ref, cb_ref, wdn_ref,
     o_ref, carry_ref, hbuf_ref) = refs[2 * FFN_SUB:]
    tm = LEAD
    cw = 256
    nj = D_FF // cw
    nslot = hbuf_ref.shape[0]

    @pl.when(pl.program_id(1) == 0)
    def _():
        h = _mixed(meta_ref[...], yrm_ref[...], yam_ref[...], wo_ref)
        carry_ref[...] = _dot(_normed(h, nw_ref), wup_ref[...])[N_META - 8:N_META, :]

    subs = range(FFN_SUB)
    hres = [_mixed(x_ref[tm * s:tm * (s + 1), :], yr_refs[s][...], ya_refs[s][...], wo_ref)
            for s in subs]
    u = [_normed(h, nw_ref) for h in hres]
    acc = [None for _ in subs]

    def up(s, j):
        return [_dot(u[s], wup_ref[:, base + cw * j:base + cw * (j + 1)]) for base in (0, D_FF)]

    items = [(s, j) for s in subs for j in range(nj)]
    hids = up(*items[0])
    for n, (s, j) in enumerate(items):
        nxt = up(*items[n + 1]) if n + 1 < len(items) else None
        halves = []
        for half, (base, hid) in enumerate(zip((0, D_FF), hids)):
            lo = base + cw * j
            buf = hbuf_ref.at[(2 * n + half) % nslot]
            buf[0:8, :] = carry_ref[:, lo:lo + cw]
            buf[8:8 + tm, :] = hid
            carry_ref[:, lo:lo + cw] = hid[tm - 8:tm, :]
            halves.append(hid * cw_ref[0:1, lo:lo + cw]
                          + buf[7:7 + tm, :] * cw_ref[1:2, lo:lo + cw]
                          + buf[6:6 + tm, :] * cw_ref[2:3, lo:lo + cw]
                          + cb_ref[:, lo:lo + cw])
        gate, val = halves
        act = (gate * jax.nn.sigmoid(gate) * val).astype(BF16)
        down = _dot(act, wdn_ref[cw * j:cw * (j + 1), :])
        acc[s] = down if acc[s] is None else acc[s] + down
        hids = nxt
        if j == nj - 1:
            o_ref[tm * s:tm * (s + 1), :] = hres[s] + acc[s]


def _ffn(x2, meta, yr, ya, w_out, norm_w, w_up, conv_w, conv_b, w_down, batch, seq):
    nlead = (LEAD + seq) // LEAD
    assert (nlead - 1) % FFN_SUB == 0
    meta_rows = lambda b, i: ((b * nlead + 1) * (LEAD // N_META) - 1, 0)
    tm = FFN_SUB * LEAD
    nt = seq // tm
    row = lambda b, i: (b * nt + i, 0)
    const = lambda b, i: (0, 0)
    sub = lambda width: [pl.BlockSpec((LEAD, width),
                                      lambda b, i, s=s: (b * nlead + 1 + FFN_SUB * i + s, 0))
                         for s in range(FFN_SUB)]
    return pl.pallas_call(
        _ffn_kernel,
        grid=(batch, nt),
        in_specs=[pl.BlockSpec((tm, D_MODEL), row)] + sub(RWKV_DIM) + sub(ATTN_DIM) + [
            pl.BlockSpec((N_META, D_MODEL), const),
            pl.BlockSpec((N_META, RWKV_DIM), meta_rows),
            pl.BlockSpec((N_META, ATTN_DIM), meta_rows),
            pl.BlockSpec((D_MODEL, D_MODEL), const),
            pl.BlockSpec((1, D_MODEL), const),
            pl.BlockSpec((D_MODEL, 2 * D_FF), const),
            pl.BlockSpec((3, 2 * D_FF), const),
            pl.BlockSpec((1, 2 * D_FF), const),
            pl.BlockSpec((D_FF, D_MODEL), const),
        ],
        out_specs=pl.BlockSpec((tm, D_MODEL), row),
        out_shape=jax.ShapeDtypeStruct(x2.shape, F32),
        scratch_shapes=[pltpu.VMEM((8, 2 * D_FF), F32),
                        pltpu.VMEM((4 * FFN_SUB, LEAD + 8, 256), F32)],
        compiler_params=pltpu.CompilerParams(
            dimension_semantics=("arbitrary", "arbitrary"), vmem_limit_bytes=VMEM_LIMIT),
    )(x2, *([yr] * FFN_SUB), *([ya] * FFN_SUB), meta, yr, ya, w_out, norm_w, w_up, conv_w, conv_b,
      w_down)


def _row_tile(lp, cap, unit=BLOCK):
    n = lp // unit
    best = 1
    for f in range(1, n + 1):
        if n % f == 0 and f * unit <= cap:
            best = f
    return best * unit


def kernel(x, meta_tokens, rel_bias, norm1_w, w_in, shift_mu, decay_w0, decay_up, aaa_a0, aaa_up, gate_up, k_k, k_a, r_k, lnx_w, lnx_b, q_norm_w, k_norm_w, sinks, w_out, norm2_w, w_up, conv_w, conv_b, w_down):
    batch, seq, _ = x.shape
    assert norm1_w.shape[0] == 1 and seq % LEAD == 0
    lp = LEAD + seq
    layer = 0
    x2 = x.reshape(batch * seq, D_MODEL)
    meta = meta_tokens.astype(x.dtype)
    row2 = lambda t: t.reshape(1, -1)
    zeros64 = jnp.zeros((64, RWKV_DIM), F32)
    dec = jnp.concatenate([decay_up[layer], zeros64], axis=0)
    dec_hi = dec.astype(BF16)
    dec_lo = (dec - dec_hi.astype(F32)).astype(BF16)
    (pr, q, kv), (w_out_b, w_up_b, w_down_b) = _inproj(
        x2, meta, row2(norm1_w[layer]), w_in[layer].astype(BF16),
        ((w_out[layer], 0), (w_up[layer], 1), (w_down[layer], 0)), batch, seq)
    y_rwkv = _rwkv(
        pr.reshape(batch, lp, RWKV_IN), row2(shift_mu[layer]), row2(decay_w0[layer]),
        jnp.concatenate([dec_hi, dec_hi, dec_lo], axis=0), row2(aaa_a0[layer]),
        jnp.concatenate([zeros64, aaa_up[layer]], axis=0).astype(BF16),
        gate_up[layer].astype(BF16), row2(k_k[layer]), row2(k_a[layer]),
        row2(lnx_w[layer]), row2(lnx_b[layer]), row2(r_k[layer]))
    y_attn = _attention(q, kv, rel_bias, sinks[layer].reshape(1, HEADS),
                        jnp.tile(q_norm_w[layer], 2).reshape(1, 128),
                        jnp.tile(k_norm_w[layer], 2).reshape(1, 128), batch, lp,
                        _row_tile(lp, 768))
    y_rwkv = y_rwkv.reshape(batch * lp, RWKV_DIM)
    out = _ffn(x2, meta, y_rwkv, y_attn, w_out_b, row2(norm2_w[layer]), w_up_b, conv_w[layer],
               row2(conv_b[layer]), w_down_b, batch, seq)
    return out.reshape(batch, seq, D_MODEL)
```

```python
import math

import jax
import jax.numpy as jnp
from jax import lax
from jax.experimental import pallas as pl
from jax.experimental.pallas import tpu as pltpu

F32 = jnp.float32
BF16 = jnp.bfloat16

D_MODEL = 1024
N_META = 16
BLOCK = 128
LEAD = 2 * BLOCK
PAD = LEAD - N_META
META_BLOCK = 1
HEADS = 8
HEAD_DIM = 64
RWKV_DIM = 512
RWKV_IN = 1792
ATTN_DIM = 512
KV_DIM = 128
IN_DIM = 2560
D_FF = 2816
CHUNK = 64
CHUNKS_PER_STEP = 4
NORM_EPS = 1e-6
GN_EPS = HEAD_DIM * 1e-5
MASK_VALUE = -1e30
REL_BUCKETS = 32
REL_MAX_EXACT = 16
REL_MAX_DIST = 128
VMEM_LIMIT = 56 * 1024 * 1024


def _dot(a, b, precision=None):
    return lax.dot_general(a, b, (((1,), (0,)), ((), ())), precision=precision,
                           preferred_element_type=F32)


def _dot_nt(a, b, precision=None):
    return lax.dot_general(a, b, (((1,), (1,)), ((), ())), precision=precision,
                           preferred_element_type=F32)


def _bf(x):
    return x.astype(BF16)


def _dot_split(x, w_bf16):
    hi = _bf(x)
    lo = _bf(x - hi.astype(F32))
    return _dot(hi, w_bf16) + _dot(lo, w_bf16)


def _head_ones(n, dtype=F32):
    r = lax.broadcasted_iota(jnp.int32, (n, n), 0) // HEAD_DIM
    c = lax.broadcasted_iota(jnp.int32, (n, n), 1) // HEAD_DIM
    return (r == c).astype(dtype)


def _lead_or_x(is_lead, x_ref, meta_ref):
    lead = jnp.concatenate([jnp.zeros((PAD, D_MODEL), F32), meta_ref[...]], axis=0)
    return jnp.where(is_lead, lead, x_ref[...])


def _inproj_kernel(*refs):
    ncast = 3
    nsub = len(refs) - 6 - 2 * ncast
    x_refs, (meta_ref, nw_ref, w_ref) = refs[:nsub], refs[nsub:nsub + 3]
    cast_in = refs[nsub + 3:nsub + 3 + ncast]
    pr_ref, q_ref, kv_ref = refs[nsub + 3 + ncast:nsub + 6 + ncast]
    cast_out = refs[nsub + 6 + ncast:]
    for src, dst in zip(cast_in, cast_out):
        dst[...] = src[...].astype(BF16)
    first = pl.program_id(1) == 0
    x = jnp.concatenate([_lead_or_x(first if k == 0 else False, x_refs[k], meta_ref)
                         for k in range(nsub)], axis=0)
    ms = jnp.mean(x * x, axis=-1, keepdims=True)
    u = (x * lax.rsqrt(ms + NORM_EPS) * nw_ref[...]).astype(BF16)
    for j in range(IN_DIM // 256):
        c = _dot(u, w_ref[:, 256 * j:256 * (j + 1)])
        lo = 256 * j
        if lo < RWKV_IN:
            pr_ref[:, lo:lo + 256] = c.astype(pr_ref.dtype)
        elif lo < RWKV_IN + ATTN_DIM:
            q_ref[:, lo - RWKV_IN:lo - RWKV_IN + 256] = c.astype(q_ref.dtype)
        else:
            kv_ref[...] = c.astype(kv_ref.dtype)


def _x_tile_spec(seq, nsub=1, k=0):
    nx = seq // LEAD
    return pl.BlockSpec((LEAD, D_MODEL),
                        lambda b, i: (b * nx + jnp.maximum(nsub * i + k - 1, 0), 0))


def _slice_spec(shape, axis, unit, nsteps, nt):
    size = shape[axis]
    blk = min(b for b in range(unit, size + 1, unit) if size % b == 0 and size // b <= nsteps)
    last = size // blk - 1
    block = tuple(blk if a == axis else s for a, s in enumerate(shape))
    step = lambda b, i: jnp.minimum(b * nt + i, last)
    return pl.BlockSpec(block, lambda b, i: tuple(step(b, i) if a == axis else 0
                                                  for a in range(len(shape))))


def _inproj(x2, meta, norm_w, w_in_bf16, later_weights, batch, seq):
    nlead = (LEAD + seq) // LEAD
    nsub = max(m for m in (3, 2, 1) if nlead % m == 0)
    tm = nsub * LEAD
    nt = nlead // nsub
    tp = batch * nlead * LEAD
    row = lambda b, i: (b * nt + i, 0)
    const = lambda b, i: (0, 0)
    cast_specs = [_slice_spec(w.shape, axis, 16 if axis == 0 else 128, batch * nt, nt)
                  for w, axis in later_weights]
    outs = pl.pallas_call(
        _inproj_kernel,
        grid=(batch, nt),
        in_specs=[_x_tile_spec(seq, nsub, k) for k in range(nsub)] + [
            pl.BlockSpec((N_META, D_MODEL), const),
            pl.BlockSpec((1, D_MODEL), const),
            pl.BlockSpec((D_MODEL, IN_DIM), const),
        ] + cast_specs,
        out_specs=[
            pl.BlockSpec((tm, RWKV_IN), row),
            pl.BlockSpec((tm, ATTN_DIM), row),
            pl.BlockSpec((tm, 2 * KV_DIM), row),
        ] + cast_specs,
        out_shape=[
            jax.ShapeDtypeStruct((tp, RWKV_IN), BF16),
            jax.ShapeDtypeStruct((tp, ATTN_DIM), BF16),
            jax.ShapeDtypeStruct((tp, 2 * KV_DIM), BF16),
        ] + [jax.ShapeDtypeStruct(w.shape, BF16) for w, _ in later_weights],
        compiler_params=pltpu.CompilerParams(
            dimension_semantics=("arbitrary", "arbitrary"), vmem_limit_bytes=VMEM_LIMIT),
    )(*([x2] * nsub), meta, norm_w, w_in_bf16, *[w for w, _ in later_weights])
    return outs[:3], outs[3:]


def _blockdiag(z, lane_lo):
    zero = jnp.zeros_like(z)
    return jnp.concatenate([jnp.where(lane_lo, z, zero), jnp.where(lane_lo, zero, z)], axis=0)


def _rwkv_kernel(p_ref, halo_ref, mu_ref, w0_ref, decup_ref, a0_ref, aup_ref, gup_ref,
                 kk_ref, ka_ref, lnw_ref, lnb_ref, rk_ref, o_ref, s_ref):
    step = pl.program_id(0)
    nb, rows, _ = p_ref.shape
    c = CHUNK

    @pl.when(step == 0)
    def _():
        s_ref[...] = jnp.zeros_like(s_ref)

    ti = lax.broadcasted_iota(jnp.int32, (c, 128), 0)
    lane = lax.broadcasted_iota(jnp.int32, (c, 128), 1)
    si = lane % HEAD_DIM
    lane_lo = lane < HEAD_DIM
    strict = si < ti
    incl = si <= ti
    eye = si == ti
    eye_f = eye.astype(F32)
    blk_masks = [(ti // 8) == (si // 8)]
    size = 8
    while size < c:
        blk_masks.append(((ti // (2 * size)) == (si // (2 * size))) & ((ti // size) != (si // size)))
        size *= 2
    ones = _head_ones(128, BF16)
    row = lax.broadcasted_iota(jnp.int32, (rows, 1), 0)
    bd = lambda z: _blockdiag(z, lane_lo)

    def pair_t(z):
        zt = bd(z).T
        return zt[0:c, :] + zt[c:2 * c, :]

    npair = HEADS // 2

    def head_sums(x, split=False):
        n = x.shape[0]
        xs = jnp.concatenate([x[:, 128 * j:128 * (j + 1)] for j in range(npair)], axis=0)
        ys = _dot_split(xs, ones) if split else _dot(_bf(xs), ones)
        return jnp.concatenate([ys[n * j:n * (j + 1)] for j in range(npair)], axis=1)

    prevs = []
    for b in range(nb):
        prev = pltpu.roll(p_ref[b].astype(F32), shift=1, axis=0)
        last = halo_ref.shape[1] - 1
        prev = jnp.where(row == 0, halo_ref[b, last:last + 1, :].astype(F32), prev)
        prevs.append(jnp.where((row == 0) & (step == 0), 0.0, prev))
    p = jnp.concatenate([p_ref[b].astype(F32) for b in range(nb)], axis=0)
    ps = p + (jnp.concatenate(prevs, axis=0) - p) * mu_ref[...]
    r = ps[:, 0:512]
    k = ps[:, 512:1024]
    v = ps[:, 1024:1536]
    wa = ps[:, 1536:1664]
    gd = ps[:, 1664:1792]
    th = jnp.tanh(wa)
    th_hi = _bf(th)
    th_lo = _bf(th - th_hi.astype(F32))
    z = w0_ref[...] + _dot(jnp.concatenate([th_hi, th_lo, th_hi], axis=1), decup_ref[...])
    lw = -math.exp(-0.5) * jax.nn.sigmoid(z)
    a = jax.nn.sigmoid(a0_ref[...] + _dot(_bf(wa), aup_ref[...]))
    g = _dot(_bf(jax.nn.sigmoid(gd)), gup_ref[...])
    kk = k * kk_ref[...]
    kn = kk * lax.rsqrt(jnp.maximum(head_sums(kk * kk), 1e-24))
    bb = kn * a
    k = k * (1.0 + (a - 1.0) * ka_ref[...])
    bonus = head_sums(r * k * rk_ref[...]) * v

    rowc = lax.broadcasted_iota(jnp.int32, (c, 1), 0)
    chains = []
    for b in range(nb):
        for ch in range(rows // c):
            rs = slice(c * ch, c * (ch + 1))
            ra = slice(rows * b + c * ch, rows * b + c * (ch + 1))
            lwc = lw[ra]
            cum = lwc
            for sh in (1, 2, 4, 8, 16, 32):
                cum = cum + jnp.where(rowc >= sh, pltpu.roll(cum, shift=sh, axis=0), 0.0)
            cum_last = cum[c - 1:c, :]
            e_neg = jnp.exp(-cum)
            e_end = jnp.exp(cum_last - cum)
            rt = r[ra] * jnp.exp(cum)
            at = -kn[ra] * jnp.exp(cum - lwc)
            kt, bt = k[ra] * e_neg, bb[ra] * e_neg
            kh, bh = k[ra] * e_end, bb[ra] * e_end
            wc = jnp.exp(cum_last)
            for j in range(npair):
                sl = slice(128 * j, 128 * (j + 1))
                chains.append(dict(
                    b=b, ch=ch, j=j, rs=rs, sl=sl, rt=rt[:, sl], at=_bf(at[:, sl]),
                    rhs=jnp.concatenate([bd(_bf(bt[:, sl])), bd(_bf(kt[:, sl]))], axis=0),
                    vbd=bd(_bf(v[ra, sl])),
                    bht=pair_t(_bf(bh[:, sl])), kht=pair_t(_bf(kh[:, sl])),
                    wc=wc[:, sl], bonus=bonus[ra, sl], g=g[ra, sl]))

    def s_products(cd):
        cd["pm"] = _dot_nt(jnp.concatenate([cd["at"], _bf(cd["rt"])], axis=0), cd.pop("rhs"))

    def s_masks(cd):
        pm = cd.pop("pm")
        cd["a_ab"] = jnp.where(strict, pm[0:c, 0:128], 0.0)
        cd["p_rb"] = _bf(jnp.where(incl, pm[c:2 * c, 0:128], 0.0))
        a_ak = _bf(jnp.where(strict, pm[0:c, 128:256], 0.0))
        p_rk = _bf(jnp.where(incl, pm[c:2 * c, 128:256], 0.0))
        cd["xv"] = _dot(jnp.concatenate([a_ak, p_rk, cd.pop("kht")], axis=0), cd.pop("vbd"))

    def s_d2(cd):
        cd["d"] = jnp.where(blk_masks[0], cd["a_ab"], 0.0)
        db = _bf(cd["d"])
        cd["bdd"] = bd(db)
        cd["d2"] = _dot(db, cd["bdd"])

    def s_d34(cd):
        d2b = _bf(cd["d2"])
        cd["d34"] = _dot(d2b, jnp.concatenate([cd.pop("bdd"), bd(d2b)], axis=1))

    def s_t0(cd):
        d34 = cd.pop("d34")
        s3 = eye_f + cd.pop("d") + cd.pop("d2") + d34[:, 0:128]
        cd["t"] = s3 + _dot(_bf(d34[:, 128:256]), bd(_bf(s3)))

    def s_inner(level):
        def run(cd):
            cd["tb"] = _bf(cd["t"])
            cd["inner"] = _bf(_dot(_bf(jnp.where(blk_masks[level], cd["a_ab"], 0.0)), bd(cd["tb"])))
        return run

    def s_merge(cd):
        cd["t"] = cd["t"] + _dot(cd.pop("tb"), bd(cd.pop("inner")))

    def s_gu(cd):
        cd.pop("a_ab")
        gu = _bf(_dot(_bf(cd.pop("t")),
                      jnp.concatenate([bd(cd.pop("at")), bd(_bf(cd["xv"][0:c]))], axis=1)))
        cd["gu"] = jnp.concatenate([bd(gu[:, 0:128]), bd(gu[:, 128:256])], axis=1)

    def s_maps(cd):
        r = _dot(jnp.concatenate([cd.pop("p_rb"), cd.pop("bht")], axis=0), cd.pop("gu"))
        xv = cd.pop("xv")
        cd["y0"] = r[0:c, 128:256] + xv[c:2 * c]
        cd["n_add"] = r[c:2 * c, 128:256] + xv[2 * c:3 * c]
        cd["mq_lhs"] = _bf(jnp.concatenate([r[c:2 * c, 0:128], cd.pop("rt") + r[0:c, 0:128]], axis=0))

    stages = [s_products, s_masks, s_d2, s_d34, s_t0]
    for level in range(1, len(blk_masks)):
        stages += [s_inner(level), s_merge]
    stages += [s_gu, s_maps]
    for stage in stages:
        for cd in chains:
            stage(cd)
    wcc = _dot_split(jnp.concatenate(
        [jnp.where(eye, jnp.broadcast_to(cd["wc"], (c, 128)), 0.0) for cd in chains], axis=0), ones)
    for i, cd in enumerate(chains):
        cd["wc_col"] = wcc[c * i:c * (i + 1)]

    states = {(b, j): s_ref[b, j] for b in range(nb) for j in range(HEADS // 2)}
    for ch in range(rows // c):
        for cd in chains:
            if cd["ch"] != ch:
                continue
            st = states[(cd["b"], cd["j"])]
            mq = _dot(cd["mq_lhs"], bd(_bf(st)))
            states[(cd["b"], cd["j"])] = cd["wc_col"] * st + mq[0:c] + cd["n_add"]
            cd["y"] = mq[c:2 * c] + cd["y0"]
    for (b, j), st in states.items():
        s_ref[b, j] = st

    ys = jnp.concatenate([cd["y"] for cd in chains], axis=0)
    dy = ys - _dot_split(ys, ones) * (1.0 / HEAD_DIM)
    var = _dot(_bf(dy * dy), ones) * (1.0 / HEAD_DIM)
    zn = dy * lax.rsqrt(var + GN_EPS)
    for i, cd in enumerate(chains):
        sl = cd["sl"]
        yn = zn[c * i:c * (i + 1)] * lnw_ref[:, sl] + lnb_ref[:, sl]
        o_ref[cd["b"], cd["rs"], sl] = ((yn + cd["bonus"]) * cd["g"]).astype(o_ref.dtype)


def _rwkv(pr3, mu, w0, decup_pad, a0, aup_pad, gup, k_k, k_a, lnx_w, lnx_b, r_k):
    batch, lp, _ = pr3.shape
    rows = CHUNK * CHUNKS_PER_STEP
    const = lambda i: (0, 0)
    vec = pl.BlockSpec((1, RWKV_DIM), const)
    lora = pl.BlockSpec((128, RWKV_DIM), const)
    return pl.pallas_call(
        _rwkv_kernel,
        grid=(lp // rows,),
        in_specs=[
            pl.BlockSpec((batch, rows, RWKV_IN), lambda i: (0, i, 0)),
            pl.BlockSpec((batch, 16, RWKV_IN),
                         lambda i: (0, jnp.maximum(i * (rows // 16) - 1, 0), 0)),
            pl.BlockSpec((1, RWKV_IN), const),
            vec, pl.BlockSpec((3 * 128, RWKV_DIM), const), vec, lora, lora, vec, vec, vec, vec, vec,
        ],
        out_specs=pl.BlockSpec((batch, rows, RWKV_DIM), lambda i: (0, i, 0)),
        out_shape=jax.ShapeDtypeStruct((batch, lp, RWKV_DIM), BF16),
        scratch_shapes=[pltpu.VMEM((batch, HEADS // 2, CHUNK, 128), F32)],
        compiler_params=pltpu.CompilerParams(
            dimension_semantics=("arbitrary",), vmem_limit_bytes=VMEM_LIMIT),
    )(pr3, pr3, mu, w0, decup_pad, a0, aup_pad, gup, k_k, k_a, lnx_w, lnx_b, r_k)


def _t5_thresholds():
    n_log = REL_BUCKETS - REL_MAX_EXACT
    out = []
    for k in range(1, n_log):
        x = REL_MAX_EXACT * (REL_MAX_DIST / REL_MAX_EXACT) ** (k / n_log)
        assert min(x - math.floor(x), math.ceil(x) - x) > 1e-3
        out.append(math.ceil(x))
    return out


def _t5_bucket(d):
    d = jnp.maximum(d, 0)
    large = jnp.full(d.shape, REL_MAX_EXACT, jnp.int32)
    for t in _t5_thresholds():
        large = large + (d >= t).astype(jnp.int32)
    return jnp.where(d < REL_MAX_EXACT, d, large)


def _attn_kernel(rb_ref, sink_ref, q_ref, kv0_ref, kvp_ref, kvc_ref, qw_ref, kw_ref, o_ref, tbl_ref):
    n = pl.program_id(1)

    @pl.when((pl.program_id(0) == 0) & (n == 0))
    def _():
        q = lax.broadcasted_iota(jnp.int32, (BLOCK, 3 * BLOCK), 0)
        col = lax.broadcasted_iota(jnp.int32, (BLOCK, 3 * BLOCK), 1)
        is_meta = col < BLOCK
        for nn in range(3):
            d_meta = nn * BLOCK + q - col
            d = jnp.where(is_meta, d_meta, q + 2 * BLOCK - col)
            ok = is_meta & (col >= BLOCK - N_META) & (d_meta >= 0)
            if nn >= 2:
                ok = ok | ((col >= BLOCK) & (col < 2 * BLOCK) & (col - BLOCK > q))
            if nn >= 1:
                ok = ok | ((col >= 2 * BLOCK) & (col - 2 * BLOCK <= q))
            bucket = _t5_bucket(d)

            def per_head(h, carry):
                acc = jnp.zeros((BLOCK, 3 * BLOCK), F32)
                for bk in range(REL_BUCKETS):
                    acc = jnp.where(bucket == bk, rb_ref[bk, h], acc)
                tbl_ref[nn, h] = jnp.where(ok, acc, MASK_VALUE)
                return carry

            lax.fori_loop(0, HEADS, per_head, 0)

    ones = _head_ones(128, BF16)
    lane_lo = lax.broadcasted_iota(jnp.int32, (BLOCK, 128), 1) < HEAD_DIM
    lane_hi = jnp.logical_not(lane_lo)
    rr = lax.broadcasted_iota(jnp.int32, (128, 128), 0)
    cc = lax.broadcasted_iota(jnp.int32, (128, 128), 1)
    dup = [((rr // HEAD_DIM == g) & (rr % HEAD_DIM == cc % HEAD_DIM)).astype(BF16) for g in range(2)]

    def qk_norm(x, w):
        ms = _dot(_bf(x * x), ones) * (1.0 / HEAD_DIM)
        return x * lax.rsqrt(ms + NORM_EPS) * w

    nbk = q_ref.shape[0] // BLOCK
    grp = HEADS // 2
    qw = qw_ref[...] * HEAD_DIM ** -0.5

    kvb = [kv0_ref[...], kvp_ref[...]] + [kvc_ref[BLOCK * jb:BLOCK * (jb + 1), :] for jb in range(nbk)]
    kvb = [x.astype(F32) for x in kvb]
    kn = [_bf(qk_norm(x[:, 0:128], kw_ref[...])) for x in kvb]
    kd = [[_bf(_dot(x, dup[g])) for x in kn] for g in range(2)]
    vd = [[_bf(_dot(_bf(x[:, 128:256]), dup[g])) for x in kvb] for g in range(2)]
    qn = [[qk_norm(q_ref[BLOCK * jb:BLOCK * (jb + 1), 128 * c4:128 * (c4 + 1)].astype(F32), qw)
           for c4 in range(4)] for jb in range(nbk)]
    sinks = [jnp.concatenate([jnp.full((BLOCK, 128), sink_ref[0, grp * g + i], F32)
                              for i in range(grp)], axis=0) for g in range(2)]
    ones_cols = jnp.ones((3 * BLOCK, 128), BF16)

    def scores(c):
        jb, g = c["jb"], c["g"]
        lhs = jnp.concatenate(
            [jnp.where(lane_lo if e == 0 else lane_hi, qn[jb][2 * g + jj], 0.0)
             for jj in range(2) for e in range(2)], axis=0).astype(BF16)
        keys = jnp.concatenate([kd[g][0], kd[g][1 + jb], kd[g][2 + jb]], axis=0)
        tsel = jnp.clip(n * nbk + jb - META_BLOCK, 0, 2)
        tb = tbl_ref[tsel, pl.ds(grp * g, grp)].reshape(grp * BLOCK, 3 * BLOCK)
        c["s"] = _dot_nt(lhs, keys) + tb

    def row_max(c):
        x = c["s"]
        m3 = jnp.maximum(jnp.maximum(x[:, 0:128], x[:, 128:256]), x[:, 256:384])
        m = jnp.maximum(jnp.max(m3, axis=-1, keepdims=True), sinks[c["g"]][:, 0:1])
        c["mb"] = jnp.broadcast_to(m, (grp * BLOCK, 128))

    def exps(c):
        x, mm = c.pop("s"), c["mb"]
        c["ex"] = jnp.concatenate(
            [_bf(jnp.exp(x[:, 128 * i:128 * (i + 1)] - mm)) for i in range(3)], axis=1)

    def values(c):
        jb, g = c["jb"], c["g"]
        vals = jnp.concatenate([vd[g][0], vd[g][1 + jb], vd[g][2 + jb]], axis=0)
        od = _dot(c.pop("ex"), jnp.concatenate([vals, ones_cols], axis=1))
        den = od[:, 128:256] + jnp.exp(sinks[g] - c.pop("mb"))
        o = od[:, 0:128] * (1.0 / den)
        for jj in range(2):
            col = 2 * g + jj
            o_ref[BLOCK * jb:BLOCK * (jb + 1), 128 * col:128 * (col + 1)] = jnp.where(
                lane_lo, o[256 * jj:256 * jj + 128], o[256 * jj + 128:256 * jj + 256]
            ).astype(o_ref.dtype)

    stages = (scores, row_max, exps, values)
    chains = [dict(jb=jb, g=g) for jb in range(nbk) for g in range(2)]
    for w in range(len(chains) + len(stages) - 1):
        for ci, c in enumerate(chains):
            if 0 <= w - ci < len(stages):
                stages[w - ci](c)


def _attention(q, kv, rel_bias, sinks, q_norm_w2, k_norm_w2, batch, lp, tm):
    nb = lp // BLOCK
    nt = lp // tm
    nbk = tm // BLOCK
    return pl.pallas_call(
        _attn_kernel,
        grid=(batch, nt),
        in_specs=[
            pl.BlockSpec(memory_space=pltpu.SMEM),
            pl.BlockSpec(memory_space=pltpu.SMEM),
            pl.BlockSpec((tm, ATTN_DIM), lambda b, n: (b * nt + n, 0)),
            pl.BlockSpec((BLOCK, 2 * KV_DIM), lambda b, n: (b * nb + META_BLOCK, 0)),
            pl.BlockSpec((BLOCK, 2 * KV_DIM),
                         lambda b, n: (b * nb + jnp.maximum(n * nbk - 1, 0), 0)),
            pl.BlockSpec((tm, 2 * KV_DIM), lambda b, n: (b * nt + n, 0)),
            pl.BlockSpec((1, 128), lambda b, n: (0, 0)),
            pl.BlockSpec((1, 128), lambda b, n: (0, 0)),
        ],
        out_specs=pl.BlockSpec((tm, ATTN_DIM), lambda b, n: (b * nt + n, 0)),
        out_shape=jax.ShapeDtypeStruct(q.shape, BF16),
        scratch_shapes=[pltpu.VMEM((3, HEADS, BLOCK, 3 * BLOCK), F32)],
        compiler_params=pltpu.CompilerParams(
            dimension_semantics=("arbitrary", "arbitrary"), vmem_limit_bytes=VMEM_LIMIT),
    )(rel_bias, sinks, q, kv, kv, kv, q_norm_w2, k_norm_w2)


def _mixed(h_in, yr, ya, wo_ref):
    return (h_in + _dot(_bf(yr), wo_ref[0:RWKV_DIM, :]) + _dot(_bf(ya), wo_ref[RWKV_DIM:, :]))


def _normed(h, nw_ref):
    ms = jnp.mean(h * h, axis=-1, keepdims=True)
    return _bf(h * lax.rsqrt(ms + NORM_EPS) * nw_ref[...])


FFN_SUB = 4


def _ffn_kernel(x_ref, *refs):
    yr_refs, ya_refs = refs[0:FFN_SUB], refs[FFN_SUB:2 * FFN_SUB]
    (meta_ref, yrm_ref, yam_ref, wo_ref, nw_ref, wup_ref, cw_ref, cb_ref, wdn_ref,
     o_ref, carry_ref, hbuf_ref) = refs[2 * FFN_SUB:]
    tm = LEAD
    cw = 256
    nj = D_FF // cw
    nslot = hbuf_ref.shape[0]

    @pl.when(pl.program_id(1) == 0)
    def _():
        h = _mixed(meta_ref[...], yrm_ref[...], yam_ref[...], wo_ref)
        carry_ref[...] = _dot(_normed(h, nw_ref), wup_ref[...])[N_META - 8:N_META, :]

    subs = range(FFN_SUB)
    hres = [_mixed(x_ref[tm * s:tm * (s + 1), :], yr_refs[s][...], ya_refs[s][...], wo_ref)
            for s in subs]
    u = [_normed(h, nw_ref) for h in hres]
    acc = [None for _ in subs]

    def up(s, j):
        return [_dot(u[s], wup_ref[:, base + cw * j:base + cw * (j + 1)]) for base in (0, D_FF)]

    items = [(s, j) for s in subs for j in range(nj)]
    hids = up(*items[0])
    for n, (s, j) in enumerate(items):
        nxt = up(*items[n + 1]) if n + 1 < len(items) else None
        halves = []
        for half, (base, hid) in enumerate(zip((0, D_FF), hids)):
            lo = base + cw * j
            buf = hbuf_ref.at[(2 * n + half) % nslot]
            buf[0:8, :] = carry_ref[:, lo:lo + cw]
            buf[8:8 + tm, :] = hid
            carry_ref[:, lo:lo + cw] = hid[tm - 8:tm, :]
            halves.append(hid * cw_ref[0:1, lo:lo + cw]
                          + buf[7:7 + tm, :] * cw_ref[1:2, lo:lo + cw]
                          + buf[6:6 + tm, :] * cw_ref[2:3, lo:lo + cw]
                          + cb_ref[:, lo:lo + cw])
        gate, val = halves
        act = (gate * jax.nn.sigmoid(gate) * val).astype(BF16)
        down = _dot(act, wdn_ref[cw * j:cw * (j + 1), :])
        acc[s] = down if acc[s] is None else acc[s] + down
        hids = nxt
        if j == nj - 1:
            o_ref[tm * s:tm * (s + 1), :] = hres[s] + acc[s]


def _ffn(x2, meta, yr, ya, w_out, norm_w, w_up, conv_w, conv_b, w_down, batch, seq):
    nlead = (LEAD + seq) // LEAD
    assert (nlead - 1) % FFN_SUB == 0
    meta_rows = lambda b, i: ((b * nlead + 1) * (LEAD // N_META) - 1, 0)
    tm = FFN_SUB * LEAD
    nt = seq // tm
    row = lambda b, i: (b * nt + i, 0)
    const = lambda b, i: (0, 0)
    sub = lambda width: [pl.BlockSpec((LEAD, width),
                                      lambda b, i, s=s: (b * nlead + 1 + FFN_SUB * i + s, 0))
                         for s in range(FFN_SUB)]
    return pl.pallas_call(
        _ffn_kernel,
        grid=(batch, nt),
        in_specs=[pl.BlockSpec((tm, D_MODEL), row)] + sub(RWKV_DIM) + sub(ATTN_DIM) + [
            pl.BlockSpec((N_META, D_MODEL), const),
            pl.BlockSpec((N_META, RWKV_DIM), meta_rows),
            pl.BlockSpec((N_META, ATTN_DIM), meta_rows),
            pl.BlockSpec((D_MODEL, D_MODEL), const),
            pl.BlockSpec((1, D_MODEL), const),
            pl.BlockSpec((D_MODEL, 2 * D_FF), const),
            pl.BlockSpec((3, 2 * D_FF), const),
            pl.BlockSpec((1, 2 * D_FF), const),
            pl.BlockSpec((D_FF, D_MODEL), const),
        ],
        out_specs=pl.BlockSpec((tm, D_MODEL), row),
        out_shape=jax.ShapeDtypeStruct(x2.shape, F32),
        scratch_shapes=[pltpu.VMEM((8, 2 * D_FF), F32),
                        pltpu.VMEM((4 * FFN_SUB, LEAD + 8, 256), F32)],
        compiler_params=pltpu.CompilerParams(
            dimension_semantics=("arbitrary", "arbitrary"), vmem_limit_bytes=VMEM_LIMIT),
    )(x2, *([yr] * FFN_SUB), *([ya] * FFN_SUB), meta, yr, ya, w_out, norm_w, w_up, conv_w, conv_b,
      w_down)


def _row_tile(lp, cap, unit=BLOCK):
    n = lp // unit
    best = 1
    for f in range(1, n + 1):
        if n % f == 0 and f * unit <= cap:
            best = f
    return best * unit


def kernel(x, meta_tokens, rel_bias, norm1_w, w_in, shift_mu, decay_w0, decay_up, aaa_a0, aaa_up, gate_up, k_k, k_a, r_k, lnx_w, lnx_b, q_norm_w, k_norm_w, sinks, w_out, norm2_w, w_up, conv_w, conv_b, w_down):
    batch, seq, _ = x.shape
    assert norm1_w.shape[0] == 1 and seq % LEAD == 0
    lp = LEAD + seq
    layer = 0
    x2 = x.reshape(batch * seq, D_MODEL)
    meta = meta_tokens.astype(x.dtype)
    row2 = lambda t: t.reshape(1, -1)
    zeros64 = jnp.zeros((64, RWKV_DIM), F32)
    dec = jnp.concatenate([decay_up[layer], zeros64], axis=0)
    dec_hi = dec.astype(BF16)
    dec_lo = (dec - dec_hi.astype(F32)).astype(BF16)
    (pr, q, kv), (w_out_b, w_up_b, w_down_b) = _inproj(
        x2, meta, row2(norm1_w[layer]), w_in[layer].astype(BF16),
        ((w_out[layer], 0), (w_up[layer], 1), (w_down[layer], 0)), batch, seq)
    y_rwkv = _rwkv(
        pr.reshape(batch, lp, RWKV_IN), row2(shift_mu[layer]), row2(decay_w0[layer]),
        jnp.concatenate([dec_hi, dec_hi, dec_lo], axis=0), row2(aaa_a0[layer]),
        jnp.concatenate([zeros64, aaa_up[layer]], axis=0).astype(BF16),
        gate_up[layer].astype(BF16), row2(k_k[layer]), row2(k_a[layer]),
        row2(lnx_w[layer]), row2(lnx_b[layer]), row2(r_k[layer]))
    y_attn = _attention(q, kv, rel_bias, sinks[layer].reshape(1, HEADS),
                        jnp.tile(q_norm_w[layer], 2).reshape(1, 128),
                        jnp.tile(k_norm_w[layer], 2).reshape(1, 128), batch, lp,
                        _row_tile(lp, 1408))
    y_rwkv = y_rwkv.reshape(batch * lp, RWKV_DIM)
    out = _ffn(x2, meta, y_rwkv, y_attn, w_out_b, row2(norm2_w[layer]), w_up_b, conv_w[layer],
               row2(conv_b[layer]), w_down_b, batch, seq)
    return out.reshape(batch, seq, D_MODEL)
```

```python
import math

import jax
import jax.numpy as jnp
from jax import lax
from jax.experimental import pallas as pl
from jax.experimental.pallas import tpu as pltpu

F32 = jnp.float32
BF16 = jnp.bfloat16

D_MODEL = 1024
N_META = 16
BLOCK = 128
LEAD = 2 * BLOCK
PAD = LEAD - N_META
META_BLOCK = 1
HEADS = 8
HEAD_DIM = 64
RWKV_DIM = 512
RWKV_IN = 1792
ATTN_DIM = 512
KV_DIM = 128
IN_DIM = 2560
D_FF = 2816
CHUNK = 64
CHUNKS_PER_STEP = 4
NORM_EPS = 1e-6
GN_EPS = HEAD_DIM * 1e-5
MASK_VALUE = -1e30
REL_BUCKETS = 32
REL_MAX_EXACT = 16
REL_MAX_DIST = 128
VMEM_LIMIT = 56 * 1024 * 1024


def _dot(a, b, precision=None):
    return lax.dot_general(a, b, (((1,), (0,)), ((), ())), precision=precision,
                           preferred_element_type=F32)


def _dot_nt(a, b, precision=None):
    return lax.dot_general(a, b, (((1,), (1,)), ((), ())), precision=precision,
                           preferred_element_type=F32)


def _bf(x):
    return x.astype(BF16)


def _dot_split(x, w_bf16):
    hi = _bf(x)
    lo = _bf(x - hi.astype(F32))
    return _dot(hi, w_bf16) + _dot(lo, w_bf16)


def _head_ones(n, dtype=F32):
    r = lax.broadcasted_iota(jnp.int32, (n, n), 0) // HEAD_DIM
    c = lax.broadcasted_iota(jnp.int32, (n, n), 1) // HEAD_DIM
    return (r == c).astype(dtype)


def _lead_or_x(is_lead, x_ref, meta_ref):
    lead = jnp.concatenate([jnp.zeros((PAD, D_MODEL), F32), meta_ref[...]], axis=0)
    return jnp.where(is_lead, lead, x_ref[...])


def _inproj_kernel(*refs):
    ncast = 3
    nsub = len(refs) - 7 - 2 * ncast
    x_refs, (meta_ref, nw_ref, w32_ref) = refs[:nsub], refs[nsub:nsub + 3]
    cast_in = refs[nsub + 3:nsub + 3 + ncast]
    pr_ref, q_ref, kv_ref = refs[nsub + 3 + ncast:nsub + 6 + ncast]
    cast_out = refs[nsub + 6 + ncast:nsub + 6 + 2 * ncast]
    w_ref = refs[-1]

    @pl.when((pl.program_id(0) == 0) & (pl.program_id(1) == 0))
    def _():
        w_ref[...] = w32_ref[...].astype(BF16)

    for src, dst in zip(cast_in, cast_out):
        dst[...] = src[...].astype(BF16)
    first = pl.program_id(1) == 0
    x = jnp.concatenate([_lead_or_x(first if k == 0 else False, x_refs[k], meta_ref)
                         for k in range(nsub)], axis=0)
    ms = jnp.mean(x * x, axis=-1, keepdims=True)
    u = (x * lax.rsqrt(ms + NORM_EPS) * nw_ref[...]).astype(BF16)
    for j in range(IN_DIM // 256):
        c = _dot(u, w_ref[:, 256 * j:256 * (j + 1)])
        lo = 256 * j
        if lo < RWKV_IN:
            pr_ref[:, lo:lo + 256] = c.astype(pr_ref.dtype)
        elif lo < RWKV_IN + ATTN_DIM:
            q_ref[:, lo - RWKV_IN:lo - RWKV_IN + 256] = c.astype(q_ref.dtype)
        else:
            kv_ref[...] = c.astype(kv_ref.dtype)


def _x_tile_spec(seq, nsub=1, k=0):
    nx = seq // LEAD
    return pl.BlockSpec((LEAD, D_MODEL),
                        lambda b, i: (b * nx + jnp.maximum(nsub * i + k - 1, 0), 0))


def _slice_spec(shape, axis, unit, nsteps, nt):
    size = shape[axis]
    blk = min(b for b in range(unit, size + 1, unit) if size % b == 0 and size // b <= nsteps)
    last = size // blk - 1
    block = tuple(blk if a == axis else s for a, s in enumerate(shape))
    step = lambda b, i: jnp.minimum(b * nt + i, last)
    return pl.BlockSpec(block, lambda b, i: tuple(step(b, i) if a == axis else 0
                                                  for a in range(len(shape))))


def _inproj(x2, meta, norm_w, w_in, later_weights, batch, seq):
    nlead = (LEAD + seq) // LEAD
    nsub = max(m for m in (3, 2, 1) if nlead % m == 0)
    tm = nsub * LEAD
    nt = nlead // nsub
    tp = batch * nlead * LEAD
    row = lambda b, i: (b * nt + i, 0)
    const = lambda b, i: (0, 0)
    cast_specs = [_slice_spec(w.shape, axis, 16 if axis == 0 else 128, batch * nt, nt)
                  for w, axis in later_weights]
    outs = pl.pallas_call(
        _inproj_kernel,
        grid=(batch, nt),
        in_specs=[_x_tile_spec(seq, nsub, k) for k in range(nsub)] + [
            pl.BlockSpec((N_META, D_MODEL), const),
            pl.BlockSpec((1, D_MODEL), const),
            pl.BlockSpec((D_MODEL, IN_DIM), const),
        ] + cast_specs,
        out_specs=[
            pl.BlockSpec((tm, RWKV_IN), row),
            pl.BlockSpec((tm, ATTN_DIM), row),
            pl.BlockSpec((tm, 2 * KV_DIM), row),
        ] + cast_specs,
        out_shape=[
            jax.ShapeDtypeStruct((tp, RWKV_IN), BF16),
            jax.ShapeDtypeStruct((tp, ATTN_DIM), BF16),
            jax.ShapeDtypeStruct((tp, 2 * KV_DIM), BF16),
        ] + [jax.ShapeDtypeStruct(w.shape, BF16) for w, _ in later_weights],
        scratch_shapes=[pltpu.VMEM(w_in.shape, BF16)],
        compiler_params=pltpu.CompilerParams(
            dimension_semantics=("arbitrary", "arbitrary"), vmem_limit_bytes=VMEM_LIMIT),
    )(*([x2] * nsub), meta, norm_w, w_in, *[w for w, _ in later_weights])
    return outs[:3], outs[3:]


def _blockdiag(z, lane_lo):
    zero = jnp.zeros_like(z)
    return jnp.concatenate([jnp.where(lane_lo, z, zero), jnp.where(lane_lo, zero, z)], axis=0)


def _rwkv_kernel(p_ref, halo_ref, mu_ref, w0_ref, decup_ref, a0_ref, aup_ref, gup_ref,
                 kk_ref, ka_ref, lnw_ref, lnb_ref, rk_ref, o_ref, s_ref):
    step = pl.program_id(0)
    nb, rows, _ = p_ref.shape
    c = CHUNK

    @pl.when(step == 0)
    def _():
        s_ref[...] = jnp.zeros_like(s_ref)

    ti = lax.broadcasted_iota(jnp.int32, (c, 128), 0)
    lane = lax.broadcasted_iota(jnp.int32, (c, 128), 1)
    si = lane % HEAD_DIM
    lane_lo = lane < HEAD_DIM
    strict = si < ti
    incl = si <= ti
    eye = si == ti
    eye_f = eye.astype(F32)
    blk_masks = [(ti // 8) == (si // 8)]
    size = 8
    while size < c:
        blk_masks.append(((ti // (2 * size)) == (si // (2 * size))) & ((ti // size) != (si // size)))
        size *= 2
    ones = _head_ones(128, BF16)
    row = lax.broadcasted_iota(jnp.int32, (rows, 1), 0)
    bd = lambda z: _blockdiag(z, lane_lo)

    def pair_t(z):
        zt = bd(z).T
        return zt[0:c, :] + zt[c:2 * c, :]

    npair = HEADS // 2

    def head_sums(x, split=False):
        n = x.shape[0]
        xs = jnp.concatenate([x[:, 128 * j:128 * (j + 1)] for j in range(npair)], axis=0)
        ys = _dot_split(xs, ones) if split else _dot(_bf(xs), ones)
        return jnp.concatenate([ys[n * j:n * (j + 1)] for j in range(npair)], axis=1)

    prevs = []
    for b in range(nb):
        prev = pltpu.roll(p_ref[b].astype(F32), shift=1, axis=0)
        last = halo_ref.shape[1] - 1
        prev = jnp.where(row == 0, halo_ref[b, last:last + 1, :].astype(F32), prev)
        prevs.append(jnp.where((row == 0) & (step == 0), 0.0, prev))
    p = jnp.concatenate([p_ref[b].astype(F32) for b in range(nb)], axis=0)
    ps = p + (jnp.concatenate(prevs, axis=0) - p) * mu_ref[...]
    r = ps[:, 0:512]
    k = ps[:, 512:1024]
    v = ps[:, 1024:1536]
    wa = ps[:, 1536:1664]
    gd = ps[:, 1664:1792]
    th = jnp.tanh(wa)
    th_hi = _bf(th)
    th_lo = _bf(th - th_hi.astype(F32))
    z = w0_ref[...] + _dot(jnp.concatenate([th_hi, th_lo, th_hi], axis=1), decup_ref[...])
    lw = -math.exp(-0.5) * jax.nn.sigmoid(z)
    a = jax.nn.sigmoid(a0_ref[...] + _dot(_bf(wa), aup_ref[...]))
    g = _dot(_bf(jax.nn.sigmoid(gd)), gup_ref[...])
    kk = k * kk_ref[...]
    kn = kk * lax.rsqrt(jnp.maximum(head_sums(kk * kk), 1e-24))
    bb = kn * a
    k = k * (1.0 + (a - 1.0) * ka_ref[...])
    bonus = head_sums(r * k * rk_ref[...]) * v

    rowc = lax.broadcasted_iota(jnp.int32, (c, 1), 0)
    chains = []
    for b in range(nb):
        for ch in range(rows // c):
            rs = slice(c * ch, c * (ch + 1))
            ra = slice(rows * b + c * ch, rows * b + c * (ch + 1))
            lwc = lw[ra]
            cum = lwc
            for sh in (1, 2, 4, 8, 16, 32):
                cum = cum + jnp.where(rowc >= sh, pltpu.roll(cum, shift=sh, axis=0), 0.0)
            cum_last = cum[c - 1:c, :]
            e_neg = jnp.exp(-cum)
            e_end = jnp.exp(cum_last - cum)
            rt = r[ra] * jnp.exp(cum)
            at = -kn[ra] * jnp.exp(cum - lwc)
            kt, bt = k[ra] * e_neg, bb[ra] * e_neg
            kh, bh = k[ra] * e_end, bb[ra] * e_end
            wc = jnp.exp(cum_last)
            for j in range(npair):
                sl = slice(128 * j, 128 * (j + 1))
                chains.append(dict(
                    b=b, ch=ch, j=j, rs=rs, sl=sl, rt=rt[:, sl], at=_bf(at[:, sl]),
                    rhs=jnp.concatenate([bd(_bf(bt[:, sl])), bd(_bf(kt[:, sl]))], axis=0),
                    vbd=bd(_bf(v[ra, sl])),
                    bht=pair_t(_bf(bh[:, sl])), kht=pair_t(_bf(kh[:, sl])),
                    wc=wc[:, sl], bonus=bonus[ra, sl], g=g[ra, sl]))

    def s_products(cd):
        cd["pm"] = _dot_nt(jnp.concatenate([cd["at"], _bf(cd["rt"])], axis=0), cd.pop("rhs"))

    def s_masks(cd):
        pm = cd.pop("pm")
        cd["a_ab"] = jnp.where(strict, pm[0:c, 0:128], 0.0)
        cd["p_rb"] = _bf(jnp.where(incl, pm[c:2 * c, 0:128], 0.0))
        a_ak = _bf(jnp.where(strict, pm[0:c, 128:256], 0.0))
        p_rk = _bf(jnp.where(incl, pm[c:2 * c, 128:256], 0.0))
        cd["xv"] = _dot(jnp.concatenate([a_ak, p_rk, cd.pop("kht")], axis=0), cd.pop("vbd"))

    def s_d2(cd):
        cd["d"] = jnp.where(blk_masks[0], cd["a_ab"], 0.0)
        db = _bf(cd["d"])
        cd["bdd"] = bd(db)
        cd["d2"] = _dot(db, cd["bdd"])

    def s_d34(cd):
        d2b = _bf(cd["d2"])
        cd["d34"] = _dot(d2b, jnp.concatenate([cd.pop("bdd"), bd(d2b)], axis=1))

    def s_t0(cd):
        d34 = cd.pop("d34")
        s3 = eye_f + cd.pop("d") + cd.pop("d2") + d34[:, 0:128]
        cd["t"] = s3 + _dot(_bf(d34[:, 128:256]), bd(_bf(s3)))

    def s_inner(level):
        def run(cd):
            cd["tb"] = _bf(cd["t"])
            cd["inner"] = _bf(_dot(_bf(jnp.where(blk_masks[level], cd["a_ab"], 0.0)), bd(cd["tb"])))
        return run

    def s_merge(cd):
        cd["t"] = cd["t"] + _dot(cd.pop("tb"), bd(cd.pop("inner")))

    def s_gu(cd):
        cd.pop("a_ab")
        gu = _bf(_dot(_bf(cd.pop("t")),
                      jnp.concatenate([bd(cd.pop("at")), bd(_bf(cd["xv"][0:c]))], axis=1)))
        cd["gu"] = jnp.concatenate([bd(gu[:, 0:128]), bd(gu[:, 128:256])], axis=1)

    def s_maps(cd):
        r = _dot(jnp.concatenate([cd.pop("p_rb"), cd.pop("bht")], axis=0), cd.pop("gu"))
        xv = cd.pop("xv")
        cd["y0"] = r[0:c, 128:256] + xv[c:2 * c]
        cd["n_add"] = r[c:2 * c, 128:256] + xv[2 * c:3 * c]
        cd["mq_lhs"] = _bf(jnp.concatenate([r[c:2 * c, 0:128], cd.pop("rt") + r[0:c, 0:128]], axis=0))

    stages = [s_products, s_masks, s_d2, s_d34, s_t0]
    for level in range(1, len(blk_masks)):
        stages += [s_inner(level), s_merge]
    stages += [s_gu, s_maps]
    for stage in stages:
        for cd in chains:
            stage(cd)
    wcc = _dot_split(jnp.concatenate(
        [jnp.where(eye, jnp.broadcast_to(cd["wc"], (c, 128)), 0.0) for cd in chains], axis=0), ones)
    for i, cd in enumerate(chains):
        cd["wc_col"] = wcc[c * i:c * (i + 1)]

    states = {(b, j): s_ref[b, j] for b in range(nb) for j in range(HEADS // 2)}
    for ch in range(rows // c):
        for cd in chains:
            if cd["ch"] != ch:
                continue
            st = states[(cd["b"], cd["j"])]
            mq = _dot(cd["mq_lhs"], bd(_bf(st)))
            states[(cd["b"], cd["j"])] = cd["wc_col"] * st + mq[0:c] + cd["n_add"]
            cd["y"] = mq[c:2 * c] + cd["y0"]
    for (b, j), st in states.items():
        s_ref[b, j] = st

    ys = jnp.concatenate([cd["y"] for cd in chains], axis=0)
    dy = ys - _dot_split(ys, ones) * (1.0 / HEAD_DIM)
    var = _dot(_bf(dy * dy), ones) * (1.0 / HEAD_DIM)
    zn = dy * lax.rsqrt(var + GN_EPS)
    for i, cd in enumerate(chains):
        sl = cd["sl"]
        yn = zn[c * i:c * (i + 1)] * lnw_ref[:, sl] + lnb_ref[:, sl]
        o_ref[cd["b"], cd["rs"], sl] = ((yn + cd["bonus"]) * cd["g"]).astype(o_ref.dtype)


def _rwkv(pr3, mu, w0, decup_pad, a0, aup_pad, gup, k_k, k_a, lnx_w, lnx_b, r_k):
    batch, lp, _ = pr3.shape
    rows = CHUNK * CHUNKS_PER_STEP
    const = lambda i: (0, 0)
    vec = pl.BlockSpec((1, RWKV_DIM), const)
    lora = pl.BlockSpec((128, RWKV_DIM), const)
    return pl.pallas_call(
        _rwkv_kernel,
        grid=(lp // rows,),
        in_specs=[
            pl.BlockSpec((batch, rows, RWKV_IN), lambda i: (0, i, 0)),
            pl.BlockSpec((batch, 16, RWKV_IN),
                         lambda i: (0, jnp.maximum(i * (rows // 16) - 1, 0), 0)),
            pl.BlockSpec((1, RWKV_IN), const),
            vec, pl.BlockSpec((3 * 128, RWKV_DIM), const), vec, lora, lora, vec, vec, vec, vec, vec,
        ],
        out_specs=pl.BlockSpec((batch, rows, RWKV_DIM), lambda i: (0, i, 0)),
        out_shape=jax.ShapeDtypeStruct((batch, lp, RWKV_DIM), BF16),
        scratch_shapes=[pltpu.VMEM((batch, HEADS // 2, CHUNK, 128), F32)],
        compiler_params=pltpu.CompilerParams(
            dimension_semantics=("arbitrary",), vmem_limit_bytes=VMEM_LIMIT),
    )(pr3, pr3, mu, w0, decup_pad, a0, aup_pad, gup, k_k, k_a, lnx_w, lnx_b, r_k)


def _t5_thresholds():
    n_log = REL_BUCKETS - REL_MAX_EXACT
    out = []
    for k in range(1, n_log):
        x = REL_MAX_EXACT * (REL_MAX_DIST / REL_MAX_EXACT) ** (k / n_log)
        assert min(x - math.floor(x), math.ceil(x) - x) > 1e-3
        out.append(math.ceil(x))
    return out


def _t5_bucket(d):
    d = jnp.maximum(d, 0)
    large = jnp.full(d.shape, REL_MAX_EXACT, jnp.int32)
    for t in _t5_thresholds():
        large = large + (d >= t).astype(jnp.int32)
    return jnp.where(d < REL_MAX_EXACT, d, large)


def _attn_kernel(rb_ref, sink_ref, q_ref, kv0_ref, kvp_ref, kvc_ref, qw_ref, kw_ref, o_ref, tbl_ref):
    n = pl.program_id(1)

    @pl.when((pl.program_id(0) == 0) & (n == 0))
    def _():
        q = lax.broadcasted_iota(jnp.int32, (BLOCK, 3 * BLOCK), 0)
        col = lax.broadcasted_iota(jnp.int32, (BLOCK, 3 * BLOCK), 1)
        is_meta = col < BLOCK
        for nn in range(3):
            d_meta = nn * BLOCK + q - col
            d = jnp.where(is_meta, d_meta, q + 2 * BLOCK - col)
            ok = is_meta & (col >= BLOCK - N_META) & (d_meta >= 0)
            if nn >= 2:
                ok = ok | ((col >= BLOCK) & (col < 2 * BLOCK) & (col - BLOCK > q))
            if nn >= 1:
                ok = ok | ((col >= 2 * BLOCK) & (col - 2 * BLOCK <= q))
            bucket = _t5_bucket(d)

            def per_head(h, carry):
                acc = jnp.zeros((BLOCK, 3 * BLOCK), F32)
                for bk in range(REL_BUCKETS):
                    acc = jnp.where(bucket == bk, rb_ref[bk, h], acc)
                tbl_ref[nn, h] = jnp.where(ok, acc, MASK_VALUE)
                return carry

            lax.fori_loop(0, HEADS, per_head, 0)

    ones = _head_ones(128, BF16)
    lane_lo = lax.broadcasted_iota(jnp.int32, (BLOCK, 128), 1) < HEAD_DIM
    lane_hi = jnp.logical_not(lane_lo)
    rr = lax.broadcasted_iota(jnp.int32, (128, 128), 0)
    cc = lax.broadcasted_iota(jnp.int32, (128, 128), 1)
    dup = [((rr // HEAD_DIM == g) & (rr % HEAD_DIM == cc % HEAD_DIM)).astype(BF16) for g in range(2)]

    def qk_norm(x, w):
        ms = _dot(_bf(x * x), ones) * (1.0 / HEAD_DIM)
        return x * lax.rsqrt(ms + NORM_EPS) * w

    nbk = q_ref.shape[0] // BLOCK
    grp = HEADS // 2
    qw = qw_ref[...] * HEAD_DIM ** -0.5

    kvb = [kv0_ref[...], kvp_ref[...]] + [kvc_ref[BLOCK * jb:BLOCK * (jb + 1), :] for jb in range(nbk)]
    kvb = [x.astype(F32) for x in kvb]
    kn = [_bf(qk_norm(x[:, 0:128], kw_ref[...])) for x in kvb]
    kd = [[_bf(_dot(x, dup[g])) for x in kn] for g in range(2)]
    vd = [[_bf(_dot(_bf(x[:, 128:256]), dup[g])) for x in kvb] for g in range(2)]
    qn = [[qk_norm(q_ref[BLOCK * jb:BLOCK * (jb + 1), 128 * c4:128 * (c4 + 1)].astype(F32), qw)
           for c4 in range(4)] for jb in range(nbk)]
    sinks = [jnp.concatenate([jnp.full((BLOCK, 128), sink_ref[0, grp * g + i], F32)
                              for i in range(grp)], axis=0) for g in range(2)]
    ones_cols = jnp.ones((3 * BLOCK, 128), BF16)

    def scores(c):
        jb, g = c["jb"], c["g"]
        lhs = jnp.concatenate(
            [jnp.where(lane_lo if e == 0 else lane_hi, qn[jb][2 * g + jj], 0.0)
             for jj in range(2) for e in range(2)], axis=0).astype(BF16)
        keys = jnp.concatenate([kd[g][0], kd[g][1 + jb], kd[g][2 + jb]], axis=0)
        tsel = jnp.clip(n * nbk + jb - META_BLOCK, 0, 2)
        tb = tbl_ref[tsel, pl.ds(grp * g, grp)].reshape(grp * BLOCK, 3 * BLOCK)
        c["s"] = _dot_nt(lhs, keys) + tb

    def row_max(c):
        x = c["s"]
        m3 = jnp.maximum(jnp.maximum(x[:, 0:128], x[:, 128:256]), x[:, 256:384])
        m = jnp.maximum(jnp.max(m3, axis=-1, keepdims=True), sinks[c["g"]][:, 0:1])
        c["mb"] = jnp.broadcast_to(m, (grp * BLOCK, 128))

    def exps(c):
        x, mm = c.pop("s"), c["mb"]
        c["ex"] = jnp.concatenate(
            [_bf(jnp.exp(x[:, 128 * i:128 * (i + 1)] - mm)) for i in range(3)], axis=1)

    def values(c):
        jb, g = c["jb"], c["g"]
        vals = jnp.concatenate([vd[g][0], vd[g][1 + jb], vd[g][2 + jb]], axis=0)
        od = _dot(c.pop("ex"), jnp.concatenate([vals, ones_cols], axis=1))
        den = od[:, 128:256] + jnp.exp(sinks[g] - c.pop("mb"))
        o = od[:, 0:128] * (1.0 / den)
        for jj in range(2):
            col = 2 * g + jj
            o_ref[BLOCK * jb:BLOCK * (jb + 1), 128 * col:128 * (col + 1)] = jnp.where(
                lane_lo, o[256 * jj:256 * jj + 128], o[256 * jj + 128:256 * jj + 256]
            ).astype(o_ref.dtype)

    stages = (scores, row_max, exps, values)
    chains = [dict(jb=jb, g=g) for jb in range(nbk) for g in range(2)]
    for w in range(len(chains) + len(stages) - 1):
        for ci, c in enumerate(chains):
            if 0 <= w - ci < len(stages):
                stages[w - ci](c)


def _attention(q, kv, rel_bias, sinks, q_norm_w2, k_norm_w2, batch, lp, tm):
    nb = lp // BLOCK
    nt = lp // tm
    nbk = tm // BLOCK
    return pl.pallas_call(
        _attn_kernel,
        grid=(batch, nt),
        in_specs=[
            pl.BlockSpec(memory_space=pltpu.SMEM),
            pl.BlockSpec(memory_space=pltpu.SMEM),
            pl.BlockSpec((tm, ATTN_DIM), lambda b, n: (b * nt + n, 0)),
            pl.BlockSpec((BLOCK, 2 * KV_DIM), lambda b, n: (b * nb + META_BLOCK, 0)),
            pl.BlockSpec((BLOCK, 2 * KV_DIM),
                         lambda b, n: (b * nb + jnp.maximum(n * nbk - 1, 0), 0)),
            pl.BlockSpec((tm, 2 * KV_DIM), lambda b, n: (b * nt + n, 0)),
            pl.BlockSpec((1, 128), lambda b, n: (0, 0)),
            pl.BlockSpec((1, 128), lambda b, n: (0, 0)),
        ],
        out_specs=pl.BlockSpec((tm, ATTN_DIM), lambda b, n: (b * nt + n, 0)),
        out_shape=jax.ShapeDtypeStruct(q.shape, BF16),
        scratch_shapes=[pltpu.VMEM((3, HEADS, BLOCK, 3 * BLOCK), F32)],
        compiler_params=pltpu.CompilerParams(
            dimension_semantics=("arbitrary", "arbitrary"), vmem_limit_bytes=VMEM_LIMIT),
    )(rel_bias, sinks, q, kv, kv, kv, q_norm_w2, k_norm_w2)


def _mixed(h_in, yr, ya, wo_ref):
    return (h_in + _dot(_bf(yr), wo_ref[0:RWKV_DIM, :]) + _dot(_bf(ya), wo_ref[RWKV_DIM:, :]))


def _normed(h, nw_ref):
    ms = jnp.mean(h * h, axis=-1, keepdims=True)
    return _bf(h * lax.rsqrt(ms + NORM_EPS) * nw_ref[...])


FFN_SUB = 4


def _ffn_kernel(x_ref, *refs):
    tm = LEAD
    nsub = x_ref.shape[0] // tm
    yr_refs, ya_refs = refs[0:nsub], refs[nsub:2 * nsub]
    (meta_ref, yrm_ref, yam_ref, wo_ref, nw_ref, wup_ref, cw_ref, cb_ref, wdn_ref,
     o_ref, carry_ref, hbuf_ref) = refs[2 * nsub:]
    cw = 256
    nj = D_FF // cw
    nslot = hbuf_ref.shape[0]

    @pl.when(pl.program_id(1) == 0)
    def _():
        h = _mixed(meta_ref[...], yrm_ref[...], yam_ref[...], wo_ref)
        carry_ref[...] = _dot(_normed(h, nw_ref), wup_ref[...])[N_META - 8:N_META, :]

    subs = range(nsub)
    hres = [_mixed(x_ref[tm * s:tm * (s + 1), :], yr_refs[s][...], ya_refs[s][...], wo_ref)
            for s in subs]
    u = [_normed(h, nw_ref) for h in hres]
    acc = [None for _ in subs]

    def up(s, j):
        return [_dot(u[s], wup_ref[:, base + cw * j:base + cw * (j + 1)]) for base in (0, D_FF)]

    items = [(s, j) for s in subs for j in range(nj)]
    hids = up(*items[0])
    for n, (s, j) in enumerate(items):
        nxt = up(*items[n + 1]) if n + 1 < len(items) else None
        halves = []
        for half, (base, hid) in enumerate(zip((0, D_FF), hids)):
            lo = base + cw * j
            buf = hbuf_ref.at[(2 * n + half) % nslot]
            buf[0:8, :] = carry_ref[:, lo:lo + cw]
            buf[8:8 + tm, :] = hid
            carry_ref[:, lo:lo + cw] = hid[tm - 8:tm, :]
            halves.append(hid * cw_ref[0:1, lo:lo + cw]
                          + buf[7:7 + tm, :] * cw_ref[1:2, lo:lo + cw]
                          + buf[6:6 + tm, :] * cw_ref[2:3, lo:lo + cw]
                          + cb_ref[:, lo:lo + cw])
        gate, val = halves
        act = (gate * jax.nn.sigmoid(gate) * val).astype(BF16)
        down = _dot(act, wdn_ref[cw * j:cw * (j + 1), :])
        acc[s] = down if acc[s] is None else acc[s] + down
        hids = nxt
        if j == nj - 1:
            o_ref[tm * s:tm * (s + 1), :] = hres[s] + acc[s]


def _ffn(x2, meta, yr, ya, w_out, norm_w, w_up, conv_w, conv_b, w_down, batch, seq):
    nlead = (LEAD + seq) // LEAD
    nsub = max(m for m in range(1, FFN_SUB + 1) if (nlead - 1) % m == 0)
    meta_rows = lambda b, i: ((b * nlead + 1) * (LEAD // N_META) - 1, 0)
    tm = nsub * LEAD
    nt = seq // tm
    row = lambda b, i: (b * nt + i, 0)
    const = lambda b, i: (0, 0)
    sub = lambda width: [pl.BlockSpec((LEAD, width),
                                      lambda b, i, s=s: (b * nlead + 1 + nsub * i + s, 0))
                         for s in range(nsub)]
    return pl.pallas_call(
        _ffn_kernel,
        grid=(batch, nt),
        in_specs=[pl.BlockSpec((tm, D_MODEL), row)] + sub(RWKV_DIM) + sub(ATTN_DIM) + [
            pl.BlockSpec((N_META, D_MODEL), const),
            pl.BlockSpec((N_META, RWKV_DIM), meta_rows),
            pl.BlockSpec((N_META, ATTN_DIM), meta_rows),
            pl.BlockSpec((D_MODEL, D_MODEL), const),
            pl.BlockSpec((1, D_MODEL), const),
            pl.BlockSpec((D_MODEL, 2 * D_FF), const),
            pl.BlockSpec((3, 2 * D_FF), const),
            pl.BlockSpec((1, 2 * D_FF), const),
            pl.BlockSpec((D_FF, D_MODEL), const),
        ],
        out_specs=pl.BlockSpec((tm, D_MODEL), row),
        out_shape=jax.ShapeDtypeStruct(x2.shape, F32),
        scratch_shapes=[pltpu.VMEM((8, 2 * D_FF), F32),
                        pltpu.VMEM((8, LEAD + 8, 256), F32)],
        compiler_params=pltpu.CompilerParams(
            dimension_semantics=("arbitrary", "arbitrary"), vmem_limit_bytes=VMEM_LIMIT),
    )(x2, *([yr] * nsub), *([ya] * nsub), meta, yr, ya, w_out, norm_w, w_up, conv_w, conv_b,
      w_down)


def _row_tile(lp, cap, unit=BLOCK):
    n = lp // unit
    best = 1
    for f in range(1, n + 1):
        if n % f == 0 and f * unit <= cap:
            best = f
    return best * unit


def kernel(x, meta_tokens, rel_bias, norm1_w, w_in, shift_mu, decay_w0, decay_up, aaa_a0, aaa_up, gate_up, k_k, k_a, r_k, lnx_w, lnx_b, q_norm_w, k_norm_w, sinks, w_out, norm2_w, w_up, conv_w, conv_b, w_down):
    batch, seq, _ = x.shape
    assert norm1_w.shape[0] == 1 and seq % LEAD == 0
    lp = LEAD + seq
    layer = 0
    x2 = x.reshape(batch * seq, D_MODEL)
    meta = meta_tokens.astype(x.dtype)
    row2 = lambda t: t.reshape(1, -1)
    zeros64 = jnp.zeros((64, RWKV_DIM), F32)
    dec = jnp.concatenate([decay_up[layer], zeros64], axis=0)
    dec_hi = dec.astype(BF16)
    dec_lo = (dec - dec_hi.astype(F32)).astype(BF16)
    (pr, q, kv), (w_out_b, w_up_b, w_down_b) = _inproj(
        x2, meta, row2(norm1_w[layer]), w_in[layer],
        ((w_out[layer], 0), (w_up[layer], 1), (w_down[layer], 0)), batch, seq)
    y_rwkv = _rwkv(
        pr.reshape(batch, lp, RWKV_IN), row2(shift_mu[layer]), row2(decay_w0[layer]),
        jnp.concatenate([dec_hi, dec_hi, dec_lo], axis=0), row2(aaa_a0[layer]),
        jnp.concatenate([zeros64, aaa_up[layer]], axis=0).astype(BF16),
        gate_up[layer].astype(BF16), row2(k_k[layer]), row2(k_a[layer]),
        row2(lnx_w[layer]), row2(lnx_b[layer]), row2(r_k[layer]))
    y_attn = _attention(q, kv, rel_bias, sinks[layer].reshape(1, HEADS),
                        jnp.tile(q_norm_w[layer], 2).reshape(1, 128),
                        jnp.tile(k_norm_w[layer], 2).reshape(1, 128), batch, lp,
                        _row_tile(lp, 1408))
    y_rwkv = y_rwkv.reshape(batch * lp, RWKV_DIM)
    out = _ffn(x2, meta, y_rwkv, y_attn, w_out_b, row2(norm2_w[layer]), w_up_b, conv_w[layer],
               row2(conv_b[layer]), w_down_b, batch, seq)
    return out.reshape(batch, seq, D_MODEL)
```

```python
import math

import jax
import jax.numpy as jnp
from jax import lax
from jax.experimental import pallas as pl
from jax.experimental.pallas import tpu as pltpu

F32 = jnp.float32
BF16 = jnp.bfloat16

D_MODEL = 1024
N_META = 16
BLOCK = 128
LEAD = 2 * BLOCK
PAD = LEAD - N_META
META_BLOCK = 1
HEADS = 8
HEAD_DIM = 64
RWKV_DIM = 512
RWKV_IN = 1792
ATTN_DIM = 512
KV_DIM = 128
IN_DIM = 2560
D_FF = 2816
CHUNK = 64
CHUNKS_PER_STEP = 4
NORM_EPS = 1e-6
GN_EPS = HEAD_DIM * 1e-5
MASK_VALUE = -1e30
REL_BUCKETS = 32
REL_MAX_EXACT = 16
REL_MAX_DIST = 128
VMEM_LIMIT = 56 * 1024 * 1024


def _dot(a, b):
    return lax.dot_general(a, b, (((1,), (0,)), ((), ())), preferred_element_type=F32)


def _dot_nt(a, b):
    return lax.dot_general(a, b, (((1,), (1,)), ((), ())), preferred_element_type=F32)


def _bf(x):
    return x.astype(BF16)


def _dot_split(x, w_bf16):
    hi = _bf(x)
    lo = _bf(x - hi.astype(F32))
    return _dot(hi, w_bf16) + _dot(lo, w_bf16)


def _head_ones(n):
    r = lax.broadcasted_iota(jnp.int32, (n, n), 0) // HEAD_DIM
    c = lax.broadcasted_iota(jnp.int32, (n, n), 1) // HEAD_DIM
    return (r == c).astype(BF16)


def _lead_or_x(is_lead, x_ref, meta_ref):
    lead = jnp.concatenate([jnp.zeros((PAD, D_MODEL), F32), meta_ref[...]], axis=0)
    return jnp.where(is_lead, lead, x_ref[...])


def _inproj_kernel(*refs):
    nsub = len(refs) - 7
    x_refs = refs[:nsub]
    meta_ref, nw_ref, w32_ref, pr_ref, q_ref, kv_ref, w_ref = refs[nsub:]

    @pl.when((pl.program_id(0) == 0) & (pl.program_id(1) == 0))
    def _():
        w_ref[...] = w32_ref[...].astype(BF16)

    first = pl.program_id(1) == 0
    x = jnp.concatenate([_lead_or_x(first if k == 0 else False, x_refs[k], meta_ref)
                         for k in range(nsub)], axis=0)
    ms = jnp.mean(x * x, axis=-1, keepdims=True)
    u = (x * lax.rsqrt(ms + NORM_EPS) * nw_ref[...]).astype(BF16)
    for j in range(IN_DIM // 256):
        c = _dot(u, w_ref[:, 256 * j:256 * (j + 1)])
        lo = 256 * j
        if lo < RWKV_IN:
            pr_ref[:, lo:lo + 256] = c.astype(pr_ref.dtype)
        elif lo < RWKV_IN + ATTN_DIM:
            q_ref[:, lo - RWKV_IN:lo - RWKV_IN + 256] = c.astype(q_ref.dtype)
        else:
            kv_ref[...] = c.astype(kv_ref.dtype)


def _x_tile_spec(seq, nsub=1, k=0):
    nx = seq // LEAD
    return pl.BlockSpec((LEAD, D_MODEL),
                        lambda b, i: (b * nx + jnp.maximum(nsub * i + k - 1, 0), 0))


def _inproj(x2, meta, norm_w, w_in, batch, seq):
    nlead = (LEAD + seq) // LEAD
    nsub = max(m for m in (3, 2, 1) if nlead % m == 0)
    tm = nsub * LEAD
    nt = nlead // nsub
    tp = batch * nlead * LEAD
    row = lambda b, i: (b * nt + i, 0)
    const = lambda b, i: (0, 0)
    return pl.pallas_call(
        _inproj_kernel,
        grid=(batch, nt),
        in_specs=[_x_tile_spec(seq, nsub, k) for k in range(nsub)] + [
            pl.BlockSpec((N_META, D_MODEL), const),
            pl.BlockSpec((1, D_MODEL), const),
            pl.BlockSpec((D_MODEL, IN_DIM), const),
        ],
        out_specs=[
            pl.BlockSpec((tm, RWKV_IN), row),
            pl.BlockSpec((tm, ATTN_DIM), row),
            pl.BlockSpec((tm, 2 * KV_DIM), row),
        ],
        out_shape=[
            jax.ShapeDtypeStruct((tp, RWKV_IN), BF16),
            jax.ShapeDtypeStruct((tp, ATTN_DIM), BF16),
            jax.ShapeDtypeStruct((tp, 2 * KV_DIM), BF16),
        ],
        scratch_shapes=[pltpu.VMEM(w_in.shape, BF16)],
        compiler_params=pltpu.CompilerParams(
            dimension_semantics=("arbitrary", "arbitrary"), vmem_limit_bytes=VMEM_LIMIT),
    )(*([x2] * nsub), meta, norm_w, w_in)


def _blockdiag(z, lane_lo):
    zero = jnp.zeros_like(z)
    return jnp.concatenate([jnp.where(lane_lo, z, zero), jnp.where(lane_lo, zero, z)], axis=0)


def _rwkv_kernel(p_ref, halo_ref, mu_ref, w0_ref, decup_ref, a0_ref, aup_ref, gup_ref,
                 kk_ref, ka_ref, lnw_ref, lnb_ref, rk_ref, *refs):
    ncast = (len(refs) - 2) // 2
    cast_in, o_ref, cast_out, s_ref = refs[:ncast], refs[ncast], refs[ncast + 1:-1], refs[-1]
    for src, dst in zip(cast_in, cast_out):
        dst[...] = src[...].astype(BF16)
    step = pl.program_id(0)
    nb, rows, _ = p_ref.shape
    c = CHUNK

    @pl.when(step == 0)
    def _():
        s_ref[...] = jnp.zeros_like(s_ref)

    ti = lax.broadcasted_iota(jnp.int32, (c, 128), 0)
    lane = lax.broadcasted_iota(jnp.int32, (c, 128), 1)
    si = lane % HEAD_DIM
    lane_lo = lane < HEAD_DIM
    strict = si < ti
    incl = si <= ti
    eye = si == ti
    eye_f = eye.astype(F32)
    blk_masks = [(ti // 8) == (si // 8)]
    size = 8
    while size < c:
        blk_masks.append(((ti // (2 * size)) == (si // (2 * size))) & ((ti // size) != (si // size)))
        size *= 2
    ones = _head_ones(128)
    row = lax.broadcasted_iota(jnp.int32, (rows, 1), 0)
    bd = lambda z: _blockdiag(z, lane_lo)

    def pair_t(z):
        zt = bd(z).T
        return zt[0:c, :] + zt[c:2 * c, :]

    npair = HEADS // 2

    def head_sums(x, split=False):
        n = x.shape[0]
        xs = jnp.concatenate([x[:, 128 * j:128 * (j + 1)] for j in range(npair)], axis=0)
        ys = _dot_split(xs, ones) if split else _dot(_bf(xs), ones)
        return jnp.concatenate([ys[n * j:n * (j + 1)] for j in range(npair)], axis=1)

    prevs = []
    for b in range(nb):
        prev = pltpu.roll(p_ref[b].astype(F32), shift=1, axis=0)
        last = halo_ref.shape[1] - 1
        prev = jnp.where(row == 0, halo_ref[b, last:last + 1, :].astype(F32), prev)
        prevs.append(jnp.where((row == 0) & (step == 0), 0.0, prev))
    p = jnp.concatenate([p_ref[b].astype(F32) for b in range(nb)], axis=0)
    ps = p + (jnp.concatenate(prevs, axis=0) - p) * mu_ref[...]
    r = ps[:, 0:512]
    k = ps[:, 512:1024]
    v = ps[:, 1024:1536]
    wa = ps[:, 1536:1664]
    gd = ps[:, 1664:1792]
    th = jnp.tanh(wa)
    th_hi = _bf(th)
    th_lo = _bf(th - th_hi.astype(F32))
    z = w0_ref[...] + _dot(jnp.concatenate([th_hi, th_lo, th_hi], axis=1), decup_ref[...])
    lw = -math.exp(-0.5) * jax.nn.sigmoid(z)
    a = jax.nn.sigmoid(a0_ref[...] + _dot(_bf(wa), aup_ref[...]))
    g = _dot(_bf(jax.nn.sigmoid(gd)), gup_ref[...])
    kk = k * kk_ref[...]
    kn = kk * lax.rsqrt(jnp.maximum(head_sums(kk * kk), 1e-24))
    bb = kn * a
    k = k * (1.0 + (a - 1.0) * ka_ref[...])
    bonus = head_sums(r * k * rk_ref[...]) * v

    rowc = lax.broadcasted_iota(jnp.int32, (c, 1), 0)
    chains = []
    for b in range(nb):
        for ch in range(rows // c):
            rs = slice(c * ch, c * (ch + 1))
            ra = slice(rows * b + c * ch, rows * b + c * (ch + 1))
            lwc = lw[ra]
            cum = lwc
            for sh in (1, 2, 4, 8, 16, 32):
                cum = cum + jnp.where(rowc >= sh, pltpu.roll(cum, shift=sh, axis=0), 0.0)
            cum_last = cum[c - 1:c, :]
            e_neg = jnp.exp(-cum)
            e_end = jnp.exp(cum_last - cum)
            rt = r[ra] * jnp.exp(cum)
            at = -kn[ra] * jnp.exp(cum - lwc)
            kt, bt = k[ra] * e_neg, bb[ra] * e_neg
            kh, bh = k[ra] * e_end, bb[ra] * e_end
            wc = jnp.exp(cum_last)
            for j in range(npair):
                sl = slice(128 * j, 128 * (j + 1))
                chains.append(dict(
                    b=b, ch=ch, j=j, rs=rs, sl=sl, rt=rt[:, sl], at=_bf(at[:, sl]),
                    rhs=jnp.concatenate([bd(_bf(bt[:, sl])), bd(_bf(kt[:, sl]))], axis=0),
                    vbd=bd(_bf(v[ra, sl])),
                    bht=pair_t(_bf(bh[:, sl])), kht=pair_t(_bf(kh[:, sl])),
                    wc=wc[:, sl], bonus=bonus[ra, sl], g=g[ra, sl]))

    def s_products(cd):
        cd["pm"] = _dot_nt(jnp.concatenate([cd["at"], _bf(cd["rt"])], axis=0), cd.pop("rhs"))

    def s_masks(cd):
        pm = cd.pop("pm")
        cd["a_ab"] = jnp.where(strict, pm[0:c, 0:128], 0.0)
        cd["p_rb"] = _bf(jnp.where(incl, pm[c:2 * c, 0:128], 0.0))
        a_ak = _bf(jnp.where(strict, pm[0:c, 128:256], 0.0))
        p_rk = _bf(jnp.where(incl, pm[c:2 * c, 128:256], 0.0))
        cd["xv"] = _dot(jnp.concatenate([a_ak, p_rk, cd.pop("kht")], axis=0), cd.pop("vbd"))

    def s_d2(cd):
        cd["d"] = jnp.where(blk_masks[0], cd["a_ab"], 0.0)
        db = _bf(cd["d"])
        cd["bdd"] = bd(db)
        cd["d2"] = _dot(db, cd["bdd"])

    def s_d34(cd):
        d2b = _bf(cd["d2"])
        cd["d34"] = _dot(d2b, jnp.concatenate([cd.pop("bdd"), bd(d2b)], axis=1))

    def s_t0(cd):
        d34 = cd.pop("d34")
        s3 = eye_f + cd.pop("d") + cd.pop("d2") + d34[:, 0:128]
        cd["t"] = s3 + _dot(_bf(d34[:, 128:256]), bd(_bf(s3)))

    def s_inner(level):
        def run(cd):
            cd["tb"] = _bf(cd["t"])
            cd["inner"] = _bf(_dot(_bf(jnp.where(blk_masks[level], cd["a_ab"], 0.0)), bd(cd["tb"])))
        return run

    def s_merge(cd):
        cd["t"] = cd["t"] + _dot(cd.pop("tb"), bd(cd.pop("inner")))

    def s_gu(cd):
        cd.pop("a_ab")
        gu = _bf(_dot(_bf(cd.pop("t")),
                      jnp.concatenate([bd(cd.pop("at")), bd(_bf(cd["xv"][0:c]))], axis=1)))
        cd["gu"] = jnp.concatenate([bd(gu[:, 0:128]), bd(gu[:, 128:256])], axis=1)

    def s_maps(cd):
        r = _dot(jnp.concatenate([cd.pop("p_rb"), cd.pop("bht")], axis=0), cd.pop("gu"))
        xv = cd.pop("xv")
        cd["y0"] = r[0:c, 128:256] + xv[c:2 * c]
        cd["n_add"] = r[c:2 * c, 128:256] + xv[2 * c:3 * c]
        cd["mq_lhs"] = _bf(jnp.concatenate([r[c:2 * c, 0:128], cd.pop("rt") + r[0:c, 0:128]], axis=0))

    stages = [s_products, s_masks, s_d2, s_d34, s_t0]
    for level in range(1, len(blk_masks)):
        stages += [s_inner(level), s_merge]
    stages += [s_gu, s_maps]
    for stage in stages:
        for cd in chains:
            stage(cd)
    wcc = _dot_split(jnp.concatenate(
        [jnp.where(eye, jnp.broadcast_to(cd["wc"], (c, 128)), 0.0) for cd in chains], axis=0), ones)
    for i, cd in enumerate(chains):
        cd["wc_col"] = wcc[c * i:c * (i + 1)]

    states = {(b, j): s_ref[b, j] for b in range(nb) for j in range(HEADS // 2)}
    for ch in range(rows // c):
        for cd in chains:
            if cd["ch"] != ch:
                continue
            st = states[(cd["b"], cd["j"])]
            mq = _dot(cd["mq_lhs"], bd(_bf(st)))
            states[(cd["b"], cd["j"])] = cd["wc_col"] * st + mq[0:c] + cd["n_add"]
            cd["y"] = mq[c:2 * c] + cd["y0"]
    for (b, j), st in states.items():
        s_ref[b, j] = st

    ys = jnp.concatenate([cd["y"] for cd in chains], axis=0)
    dy = ys - _dot_split(ys, ones) * (1.0 / HEAD_DIM)
    var = _dot(_bf(dy * dy), ones) * (1.0 / HEAD_DIM)
    zn = dy * lax.rsqrt(var + GN_EPS)
    for i, cd in enumerate(chains):
        sl = cd["sl"]
        yn = zn[c * i:c * (i + 1)] * lnw_ref[:, sl] + lnb_ref[:, sl]
        o_ref[cd["b"], cd["rs"], sl] = ((yn + cd["bonus"]) * cd["g"]).astype(o_ref.dtype)


def _slice_spec(shape, axis, unit, nsteps):
    size = shape[axis]
    blk = min(b for b in range(unit, size + 1, unit) if size % b == 0 and size // b <= nsteps)
    last = size // blk - 1
    block = tuple(blk if a == axis else s for a, s in enumerate(shape))
    return pl.BlockSpec(block, lambda i: tuple(jnp.minimum(i, last) if a == axis else 0
                                               for a in range(len(shape))))


def _rwkv(pr3, mu, w0, decup_pad, a0, aup_pad, gup, k_k, k_a, lnx_w, lnx_b, r_k, later_weights):
    batch, lp, _ = pr3.shape
    rows = CHUNK * CHUNKS_PER_STEP
    const = lambda i: (0, 0)
    vec = pl.BlockSpec((1, RWKV_DIM), const)
    lora = pl.BlockSpec((128, RWKV_DIM), const)
    cast_specs = [_slice_spec(w.shape, axis, 16 if axis == 0 else 128, lp // rows)
                  for w, axis in later_weights]
    outs = pl.pallas_call(
        _rwkv_kernel,
        grid=(lp // rows,),
        in_specs=[
            pl.BlockSpec((batch, rows, RWKV_IN), lambda i: (0, i, 0)),
            pl.BlockSpec((batch, 16, RWKV_IN),
                         lambda i: (0, jnp.maximum(i * (rows // 16) - 1, 0), 0)),
            pl.BlockSpec((1, RWKV_IN), const),
            vec, pl.BlockSpec((3 * 128, RWKV_DIM), const), vec, lora, lora, vec, vec, vec, vec, vec,
        ] + cast_specs,
        out_specs=[pl.BlockSpec((batch, rows, RWKV_DIM), lambda i: (0, i, 0))] + cast_specs,
        out_shape=[jax.ShapeDtypeStruct((batch, lp, RWKV_DIM), BF16)]
        + [jax.ShapeDtypeStruct(w.shape, BF16) for w, _ in later_weights],
        scratch_shapes=[pltpu.VMEM((batch, HEADS // 2, CHUNK, 128), F32)],
        compiler_params=pltpu.CompilerParams(
            dimension_semantics=("arbitrary",), vmem_limit_bytes=VMEM_LIMIT),
    )(pr3, pr3, mu, w0, decup_pad, a0, aup_pad, gup, k_k, k_a, lnx_w, lnx_b, r_k,
      *[w for w, _ in later_weights])
    return outs[0], outs[1:]


def _t5_thresholds():
    n_log = REL_BUCKETS - REL_MAX_EXACT
    out = []
    for k in range(1, n_log):
        x = REL_MAX_EXACT * (REL_MAX_DIST / REL_MAX_EXACT) ** (k / n_log)
        assert min(x - math.floor(x), math.ceil(x) - x) > 1e-3
        out.append(math.ceil(x))
    return out


def _t5_bucket(d):
    d = jnp.maximum(d, 0)
    large = jnp.full(d.shape, REL_MAX_EXACT, jnp.int32)
    for t in _t5_thresholds():
        large = large + (d >= t).astype(jnp.int32)
    return jnp.where(d < REL_MAX_EXACT, d, large)


def _attn_kernel(rb_ref, sink_ref, q_ref, kv0_ref, kvp_ref, kvc_ref, qw_ref, kw_ref, o_ref, tbl_ref):
    n = pl.program_id(1)

    @pl.when((pl.program_id(0) == 0) & (n == 0))
    def _():
        q = lax.broadcasted_iota(jnp.int32, (BLOCK, 3 * BLOCK), 0)
        col = lax.broadcasted_iota(jnp.int32, (BLOCK, 3 * BLOCK), 1)
        is_meta = col < BLOCK
        for nn in range(3):
            d_meta = nn * BLOCK + q - col
            d = jnp.where(is_meta, d_meta, q + 2 * BLOCK - col)
            ok = is_meta & (col >= BLOCK - N_META) & (d_meta >= 0)
            if nn >= 2:
                ok = ok | ((col >= BLOCK) & (col < 2 * BLOCK) & (col - BLOCK > q))
            if nn >= 1:
                ok = ok | ((col >= 2 * BLOCK) & (col - 2 * BLOCK <= q))
            bucket = _t5_bucket(d)

            def per_head(h, carry):
                acc = jnp.zeros((BLOCK, 3 * BLOCK), F32)
                for bk in range(REL_BUCKETS):
                    acc = jnp.where(bucket == bk, rb_ref[bk, h], acc)
                tbl_ref[nn, h] = jnp.where(ok, acc, MASK_VALUE)
                return carry

            lax.fori_loop(0, HEADS, per_head, 0)

    ones = _head_ones(128)
    lane_lo = lax.broadcasted_iota(jnp.int32, (BLOCK, 128), 1) < HEAD_DIM
    lane_hi = jnp.logical_not(lane_lo)
    rr = lax.broadcasted_iota(jnp.int32, (128, 128), 0)
    cc = lax.broadcasted_iota(jnp.int32, (128, 128), 1)
    dup = [((rr // HEAD_DIM == g) & (rr % HEAD_DIM == cc % HEAD_DIM)).astype(BF16) for g in range(2)]

    def qk_norm(x, w):
        ms = _dot(_bf(x * x), ones) * (1.0 / HEAD_DIM)
        return x * lax.rsqrt(ms + NORM_EPS) * w

    nbk = q_ref.shape[0] // BLOCK
    grp = HEADS // 2
    qw = qw_ref[...] * HEAD_DIM ** -0.5

    kvb = [kv0_ref[...], kvp_ref[...]] + [kvc_ref[BLOCK * jb:BLOCK * (jb + 1), :] for jb in range(nbk)]
    kvb = [x.astype(F32) for x in kvb]
    kn = [_bf(qk_norm(x[:, 0:128], kw_ref[...])) for x in kvb]
    kd = [[_bf(_dot(x, dup[g])) for x in kn] for g in range(2)]
    vd = [[_bf(_dot(_bf(x[:, 128:256]), dup[g])) for x in kvb] for g in range(2)]
    qn = [[qk_norm(q_ref[BLOCK * jb:BLOCK * (jb + 1), 128 * c4:128 * (c4 + 1)].astype(F32), qw)
           for c4 in range(4)] for jb in range(nbk)]
    sinks = [jnp.concatenate([jnp.full((BLOCK, 128), sink_ref[0, grp * g + i], F32)
                              for i in range(grp)], axis=0) for g in range(2)]
    ones_cols = jnp.ones((3 * BLOCK, 128), BF16)

    def scores(c):
        jb, g = c["jb"], c["g"]
        lhs = jnp.concatenate(
            [jnp.where(lane_lo if e == 0 else lane_hi, qn[jb][2 * g + jj], 0.0)
             for jj in range(2) for e in range(2)], axis=0).astype(BF16)
        keys = jnp.concatenate([kd[g][0], kd[g][1 + jb], kd[g][2 + jb]], axis=0)
        tsel = jnp.clip(n * nbk + jb - META_BLOCK, 0, 2)
        tb = tbl_ref[tsel, pl.ds(grp * g, grp)].reshape(grp * BLOCK, 3 * BLOCK)
        c["s"] = _dot_nt(lhs, keys) + tb

    def row_max(c):
        x = c["s"]
        m3 = jnp.maximum(jnp.maximum(x[:, 0:128], x[:, 128:256]), x[:, 256:384])
        m = jnp.maximum(jnp.max(m3, axis=-1, keepdims=True), sinks[c["g"]][:, 0:1])
        c["mb"] = jnp.broadcast_to(m, (grp * BLOCK, 128))

    def exps(c):
        x, mm = c.pop("s"), c["mb"]
        c["ex"] = jnp.concatenate(
            [_bf(jnp.exp(x[:, 128 * i:128 * (i + 1)] - mm)) for i in range(3)], axis=1)

    def values(c):
        jb, g = c["jb"], c["g"]
        vals = jnp.concatenate([vd[g][0], vd[g][1 + jb], vd[g][2 + jb]], axis=0)
        od = _dot(c.pop("ex"), jnp.concatenate([vals, ones_cols], axis=1))
        den = od[:, 128:256] + jnp.exp(sinks[g] - c.pop("mb"))
        o = od[:, 0:128] * (1.0 / den)
        for jj in range(2):
            col = 2 * g + jj
            o_ref[BLOCK * jb:BLOCK * (jb + 1), 128 * col:128 * (col + 1)] = jnp.where(
                lane_lo, o[256 * jj:256 * jj + 128], o[256 * jj + 128:256 * jj + 256]
            ).astype(o_ref.dtype)

    stages = (scores, row_max, exps, values)
    chains = [dict(jb=jb, g=g) for jb in range(nbk) for g in range(2)]
    for w in range(len(chains) + len(stages) - 1):
        for ci, c in enumerate(chains):
            if 0 <= w - ci < len(stages):
                stages[w - ci](c)


def _attention(q, kv, rel_bias, sinks, q_norm_w2, k_norm_w2, batch, lp, tm):
    nb = lp // BLOCK
    nt = lp // tm
    nbk = tm // BLOCK
    return pl.pallas_call(
        _attn_kernel,
        grid=(batch, nt),
        in_specs=[
            pl.BlockSpec(memory_space=pltpu.SMEM),
            pl.BlockSpec(memory_space=pltpu.SMEM),
            pl.BlockSpec((tm, ATTN_DIM), lambda b, n: (b * nt + n, 0)),
            pl.BlockSpec((BLOCK, 2 * KV_DIM), lambda b, n: (b * nb + META_BLOCK, 0)),
            pl.BlockSpec((BLOCK, 2 * KV_DIM),
                         lambda b, n: (b * nb + jnp.maximum(n * nbk - 1, 0), 0)),
            pl.BlockSpec((tm, 2 * KV_DIM), lambda b, n: (b * nt + n, 0)),
            pl.BlockSpec((1, 128), lambda b, n: (0, 0)),
            pl.BlockSpec((1, 128), lambda b, n: (0, 0)),
        ],
        out_specs=pl.BlockSpec((tm, ATTN_DIM), lambda b, n: (b * nt + n, 0)),
        out_shape=jax.ShapeDtypeStruct(q.shape, BF16),
        scratch_shapes=[pltpu.VMEM((3, HEADS, BLOCK, 3 * BLOCK), F32)],
        compiler_params=pltpu.CompilerParams(
            dimension_semantics=("arbitrary", "arbitrary"), vmem_limit_bytes=VMEM_LIMIT),
    )(rel_bias, sinks, q, kv, kv, kv, q_norm_w2, k_norm_w2)


def _mixed(h_in, yr, ya, wo_ref):
    return (h_in + _dot(_bf(yr), wo_ref[0:RWKV_DIM, :]) + _dot(_bf(ya), wo_ref[RWKV_DIM:, :]))


def _normed(h, nw_ref):
    ms = jnp.mean(h * h, axis=-1, keepdims=True)
    return _bf(h * lax.rsqrt(ms + NORM_EPS) * nw_ref[...])


FFN_SUB = 4


def _ffn_kernel(x_ref, *refs):
    tm = LEAD
    nsub = x_ref.shape[0] // tm
    yr_refs, ya_refs = refs[0:nsub], refs[nsub:2 * nsub]
    (meta_ref, yrm_ref, yam_ref, wo_ref, nw_ref, wup_ref, cw_ref, cb_ref, wdn_ref,
     o_ref, carry_ref, hbuf_ref) = refs[2 * nsub:]
    cw = 256
    nj = D_FF // cw
    nslot = hbuf_ref.shape[0]

    @pl.when(pl.program_id(1) == 0)
    def _():
        h = _mixed(meta_ref[...], yrm_ref[...], yam_ref[...], wo_ref)
        carry_ref[...] = _dot(_normed(h, nw_ref), wup_ref[...])[N_META - 8:N_META, :]

    subs = range(nsub)
    hres = [_mixed(x_ref[tm * s:tm * (s + 1), :], yr_refs[s][...], ya_refs[s][...], wo_ref)
            for s in subs]
    u = [_normed(h, nw_ref) for h in hres]
    acc = [None for _ in subs]

    def up(s, j):
        return [_dot(u[s], wup_ref[:, base + cw * j:base + cw * (j + 1)]) for base in (0, D_FF)]

    items = [(s, j) for s in subs for j in range(nj)]
    hids = up(*items[0])
    for n, (s, j) in enumerate(items):
        nxt = up(*items[n + 1]) if n + 1 < len(items) else None
        halves = []
        for half, (base, hid) in enumerate(zip((0, D_FF), hids)):
            lo = base + cw * j
            buf = hbuf_ref.at[(2 * n + half) % nslot]
            buf[0:8, :] = carry_ref[:, lo:lo + cw]
            buf[8:8 + tm, :] = hid
            carry_ref[:, lo:lo + cw] = hid[tm - 8:tm, :]
            halves.append(hid * cw_ref[0:1, lo:lo + cw]
                          + buf[7:7 + tm, :] * cw_ref[1:2, lo:lo + cw]
                          + buf[6:6 + tm, :] * cw_ref[2:3, lo:lo + cw]
                          + cb_ref[:, lo:lo + cw])
        gate, val = halves
        act = (gate * jax.nn.sigmoid(gate) * val).astype(BF16)
        down = _dot(act, wdn_ref[cw * j:cw * (j + 1), :])
        acc[s] = down if acc[s] is None else acc[s] + down
        hids = nxt
        if j == nj - 1:
            o_ref[tm * s:tm * (s + 1), :] = hres[s] + acc[s]


def _ffn(x2, meta, yr, ya, w_out, norm_w, w_up, conv_w, conv_b, w_down, batch, seq):
    nlead = (LEAD + seq) // LEAD
    nsub = max(m for m in range(1, FFN_SUB + 1) if (nlead - 1) % m == 0)
    meta_rows = lambda b, i: ((b * nlead + 1) * (LEAD // N_META) - 1, 0)
    tm = nsub * LEAD
    nt = seq // tm
    row = lambda b, i: (b * nt + i, 0)
    const = lambda b, i: (0, 0)
    sub = lambda width: [pl.BlockSpec((LEAD, width),
                                      lambda b, i, s=s: (b * nlead + 1 + nsub * i + s, 0))
                         for s in range(nsub)]
    return pl.pallas_call(
        _ffn_kernel,
        grid=(batch, nt),
        in_specs=[pl.BlockSpec((tm, D_MODEL), row)] + sub(RWKV_DIM) + sub(ATTN_DIM) + [
            pl.BlockSpec((N_META, D_MODEL), const),
            pl.BlockSpec((N_META, RWKV_DIM), meta_rows),
            pl.BlockSpec((N_META, ATTN_DIM), meta_rows),
            pl.BlockSpec((D_MODEL, D_MODEL), const),
            pl.BlockSpec((1, D_MODEL), const),
            pl.BlockSpec((D_MODEL, 2 * D_FF), const),
            pl.BlockSpec((3, 2 * D_FF), const),
            pl.BlockSpec((1, 2 * D_FF), const),
            pl.BlockSpec((D_FF, D_MODEL), const),
        ],
        out_specs=pl.BlockSpec((tm, D_MODEL), row),
        out_shape=jax.ShapeDtypeStruct(x2.shape, F32),
        scratch_shapes=[pltpu.VMEM((8, 2 * D_FF), F32),
                        pltpu.VMEM((8, LEAD + 8, 256), F32)],
        compiler_params=pltpu.CompilerParams(
            dimension_semantics=("arbitrary", "arbitrary"), vmem_limit_bytes=VMEM_LIMIT),
    )(x2, *([yr] * nsub), *([ya] * nsub), meta, yr, ya, w_out, norm_w, w_up, conv_w, conv_b,
      w_down)


def _row_tile(lp, cap, unit=BLOCK):
    n = lp // unit
    best = 1
    for f in range(1, n + 1):
        if n % f == 0 and f * unit <= cap:
            best = f
    return best * unit


def kernel(x, meta_tokens, rel_bias, norm1_w, w_in, shift_mu, decay_w0, decay_up, aaa_a0, aaa_up, gate_up, k_k, k_a, r_k, lnx_w, lnx_b, q_norm_w, k_norm_w, sinks, w_out, norm2_w, w_up, conv_w, conv_b, w_down):
    batch, seq, _ = x.shape
    assert norm1_w.shape[0] == 1 and seq % LEAD == 0
    lp = LEAD + seq
    layer = 0
    x2 = x.reshape(batch * seq, D_MODEL)
    meta = meta_tokens.astype(x.dtype)
    row2 = lambda t: t.reshape(1, -1)
    zeros64 = jnp.zeros((64, RWKV_DIM), F32)
    dec = jnp.concatenate([decay_up[layer], zeros64], axis=0)
    dec_hi = dec.astype(BF16)
    dec_lo = (dec - dec_hi.astype(F32)).astype(BF16)
    pr, q, kv = _inproj(x2, meta, row2(norm1_w[layer]), w_in[layer], batch, seq)
    y_rwkv, (w_out_b, w_up_b, w_down_b) = _rwkv(
        pr.reshape(batch, lp, RWKV_IN), row2(shift_mu[layer]), row2(decay_w0[layer]),
        jnp.concatenate([dec_hi, dec_hi, dec_lo], axis=0), row2(aaa_a0[layer]),
        jnp.concatenate([zeros64, aaa_up[layer]], axis=0).astype(BF16),
        gate_up[layer].astype(BF16), row2(k_k[layer]), row2(k_a[layer]),
        row2(lnx_w[layer]), row2(lnx_b[layer]), row2(r_k[layer]),
        ((w_out[layer], 0), (w_up[layer], 1), (w_down[layer], 0)))
    y_attn = _attention(q, kv, rel_bias, sinks[layer].reshape(1, HEADS),
                        jnp.tile(q_norm_w[layer], 2).reshape(1, 128),
                        jnp.tile(k_norm_w[layer], 2).reshape(1, 128), batch, lp,
                        _row_tile(lp, 1408))
    y_rwkv = y_rwkv.reshape(batch * lp, RWKV_DIM)
    out = _ffn(x2, meta, y_rwkv, y_attn, w_out_b, row2(norm2_w[layer]), w_up_b, conv_w[layer],
               row2(conv_b[layer]), w_down_b, batch, seq)
    return out.reshape(batch, seq, D_MODEL)
```

```python
import math

import jax
import jax.numpy as jnp
from jax import lax
from jax.experimental import pallas as pl
from jax.experimental.pallas import tpu as pltpu

F32 = jnp.float32
BF16 = jnp.bfloat16

D_MODEL = 1024
N_META = 16
BLOCK = 128
LEAD = 2 * BLOCK
PAD = LEAD - N_META
META_BLOCK = 1
HEADS = 8
HEAD_DIM = 64
RWKV_DIM = 512
RWKV_IN = 1792
ATTN_DIM = 512
KV_DIM = 128
IN_DIM = 2560
D_FF = 2816
CHUNK = 64
CHUNKS_PER_STEP = 6
NORM_EPS = 1e-6
GN_EPS = HEAD_DIM * 1e-5
MASK_VALUE = -1e30
REL_BUCKETS = 32
REL_MAX_EXACT = 16
REL_MAX_DIST = 128
VMEM_LIMIT = 56 * 1024 * 1024


def _dot(a, b):
    return lax.dot_general(a, b, (((1,), (0,)), ((), ())), preferred_element_type=F32)


def _dot_nt(a, b):
    return lax.dot_general(a, b, (((1,), (1,)), ((), ())), preferred_element_type=F32)


def _bf(x):
    return x.astype(BF16)


def _dot_split(x, w_bf16):
    hi = _bf(x)
    lo = _bf(x - hi.astype(F32))
    return _dot(hi, w_bf16) + _dot(lo, w_bf16)


def _head_ones(n):
    r = lax.broadcasted_iota(jnp.int32, (n, n), 0) // HEAD_DIM
    c = lax.broadcasted_iota(jnp.int32, (n, n), 1) // HEAD_DIM
    return (r == c).astype(BF16)


def _lead_or_x(is_lead, x_ref, meta_ref):
    lead = jnp.concatenate([jnp.zeros((PAD, D_MODEL), F32), meta_ref[...]], axis=0)
    return jnp.where(is_lead, lead, x_ref[...])


def _inproj_kernel(*refs):
    nsub = len(refs) - 7
    x_refs = refs[:nsub]
    meta_ref, nw_ref, w32_ref, pr_ref, q_ref, kv_ref, w_ref = refs[nsub:]

    @pl.when((pl.program_id(0) == 0) & (pl.program_id(1) == 0))
    def _():
        w_ref[...] = w32_ref[...].astype(BF16)

    first = pl.program_id(1) == 0
    x = jnp.concatenate([_lead_or_x(first if k == 0 else False, x_refs[k], meta_ref)
                         for k in range(nsub)], axis=0)
    ms = jnp.mean(x * x, axis=-1, keepdims=True)
    u = (x * lax.rsqrt(ms + NORM_EPS) * nw_ref[...]).astype(BF16)
    for j in range(IN_DIM // 256):
        c = _dot(u, w_ref[:, 256 * j:256 * (j + 1)])
        lo = 256 * j
        if lo < RWKV_IN:
            pr_ref[:, lo:lo + 256] = c.astype(pr_ref.dtype)
        elif lo < RWKV_IN + ATTN_DIM:
            q_ref[:, lo - RWKV_IN:lo - RWKV_IN + 256] = c.astype(q_ref.dtype)
        else:
            kv_ref[...] = c.astype(kv_ref.dtype)


def _x_tile_spec(seq, nsub=1, k=0):
    nx = seq // LEAD
    return pl.BlockSpec((LEAD, D_MODEL),
                        lambda b, i: (b * nx + jnp.maximum(nsub * i + k - 1, 0), 0))


def _inproj(x2, meta, norm_w, w_in, batch, seq):
    nlead = (LEAD + seq) // LEAD
    nsub = max(m for m in (3, 2, 1) if nlead % m == 0)
    tm = nsub * LEAD
    nt = nlead // nsub
    tp = batch * nlead * LEAD
    row = lambda b, i: (b * nt + i, 0)
    const = lambda b, i: (0, 0)
    return pl.pallas_call(
        _inproj_kernel,
        grid=(batch, nt),
        in_specs=[_x_tile_spec(seq, nsub, k) for k in range(nsub)] + [
            pl.BlockSpec((N_META, D_MODEL), const),
            pl.BlockSpec((1, D_MODEL), const),
            pl.BlockSpec((D_MODEL, IN_DIM), const),
        ],
        out_specs=[
            pl.BlockSpec((tm, RWKV_IN), row),
            pl.BlockSpec((tm, ATTN_DIM), row),
            pl.BlockSpec((tm, 2 * KV_DIM), row),
        ],
        out_shape=[
            jax.ShapeDtypeStruct((tp, RWKV_IN), BF16),
            jax.ShapeDtypeStruct((tp, ATTN_DIM), BF16),
            jax.ShapeDtypeStruct((tp, 2 * KV_DIM), BF16),
        ],
        scratch_shapes=[pltpu.VMEM(w_in.shape, BF16)],
        compiler_params=pltpu.CompilerParams(
            dimension_semantics=("arbitrary", "arbitrary"), vmem_limit_bytes=VMEM_LIMIT),
    )(*([x2] * nsub), meta, norm_w, w_in)


def _blockdiag(z, lane_lo):
    zero = jnp.zeros_like(z)
    return jnp.concatenate([jnp.where(lane_lo, z, zero), jnp.where(lane_lo, zero, z)], axis=0)


def _rwkv_kernel(p_ref, halo_ref, mu_ref, w0_ref, decup_ref, a0_ref, aup_ref, gup_ref,
                 kk_ref, ka_ref, lnw_ref, lnb_ref, rk_ref, *refs):
    ncast = (len(refs) - 2) // 2
    cast_in, o_ref, cast_out, s_ref = refs[:ncast], refs[ncast], refs[ncast + 1:-1], refs[-1]
    for src, dst in zip(cast_in, cast_out):
        dst[...] = src[...].astype(BF16)
    step = pl.program_id(0)
    nb, rows, _ = p_ref.shape
    c = CHUNK

    @pl.when(step == 0)
    def _():
        s_ref[...] = jnp.zeros_like(s_ref)

    ti = lax.broadcasted_iota(jnp.int32, (c, 128), 0)
    lane = lax.broadcasted_iota(jnp.int32, (c, 128), 1)
    si = lane % HEAD_DIM
    lane_lo = lane < HEAD_DIM
    strict = si < ti
    incl = si <= ti
    eye = si == ti
    eye_f = eye.astype(F32)
    blk_masks = [(ti // 8) == (si // 8)]
    size = 8
    while size < c:
        blk_masks.append(((ti // (2 * size)) == (si // (2 * size))) & ((ti // size) != (si // size)))
        size *= 2
    ones = _head_ones(128)
    row = lax.broadcasted_iota(jnp.int32, (rows, 1), 0)
    bd = lambda z: _blockdiag(z, lane_lo)

    def pair_t(z):
        zt = bd(z).T
        return zt[0:c, :] + zt[c:2 * c, :]

    npair = HEADS // 2

    def head_sums(x, split=False):
        n = x.shape[0]
        xs = jnp.concatenate([x[:, 128 * j:128 * (j + 1)] for j in range(npair)], axis=0)
        ys = _dot_split(xs, ones) if split else _dot(_bf(xs), ones)
        return jnp.concatenate([ys[n * j:n * (j + 1)] for j in range(npair)], axis=1)

    prevs = []
    for b in range(nb):
        prev = pltpu.roll(p_ref[b].astype(F32), shift=1, axis=0)
        last = halo_ref.shape[1] - 1
        prev = jnp.where(row == 0, halo_ref[b, last:last + 1, :].astype(F32), prev)
        prevs.append(jnp.where((row == 0) & (step == 0), 0.0, prev))
    p = jnp.concatenate([p_ref[b].astype(F32) for b in range(nb)], axis=0)
    ps = p + (jnp.concatenate(prevs, axis=0) - p) * mu_ref[...]
    r = ps[:, 0:512]
    k = ps[:, 512:1024]
    v = ps[:, 1024:1536]
    wa = ps[:, 1536:1664]
    gd = ps[:, 1664:1792]
    th = jnp.tanh(wa)
    th_hi = _bf(th)
    th_lo = _bf(th - th_hi.astype(F32))
    z = w0_ref[...] + _dot(jnp.concatenate([th_hi, th_lo, th_hi], axis=1), decup_ref[...])
    lw = -math.exp(-0.5) * jax.nn.sigmoid(z)
    a = jax.nn.sigmoid(a0_ref[...] + _dot(_bf(wa), aup_ref[...]))
    g = _dot(_bf(jax.nn.sigmoid(gd)), gup_ref[...])
    kk = k * kk_ref[...]
    kn = kk * lax.rsqrt(jnp.maximum(head_sums(kk * kk), 1e-24))
    bb = kn * a
    k = k * (1.0 + (a - 1.0) * ka_ref[...])
    bonus = head_sums(r * k * rk_ref[...]) * v

    rowc = lax.broadcasted_iota(jnp.int32, (c, 1), 0)
    chains = []
    for b in range(nb):
        for ch in range(rows // c):
            rs = slice(c * ch, c * (ch + 1))
            ra = slice(rows * b + c * ch, rows * b + c * (ch + 1))
            lwc = lw[ra]
            cum = lwc
            for sh in (1, 2, 4, 8, 16, 32):
                cum = cum + jnp.where(rowc >= sh, pltpu.roll(cum, shift=sh, axis=0), 0.0)
            cum_last = cum[c - 1:c, :]
            e_neg = jnp.exp(-cum)
            e_end = jnp.exp(cum_last - cum)
            rt = r[ra] * jnp.exp(cum)
            at = -kn[ra] * jnp.exp(cum - lwc)
            kt, bt = k[ra] * e_neg, bb[ra] * e_neg
            kh, bh = k[ra] * e_end, bb[ra] * e_end
            wc = jnp.exp(cum_last)
            for j in range(npair):
                sl = slice(128 * j, 128 * (j + 1))
                chains.append(dict(
                    b=b, ch=ch, j=j, rs=rs, sl=sl, rt=rt[:, sl], at=_bf(at[:, sl]),
                    rhs=jnp.concatenate([bd(_bf(bt[:, sl])), bd(_bf(kt[:, sl]))], axis=0),
                    vbd=bd(_bf(v[ra, sl])),
                    bht=pair_t(_bf(bh[:, sl])), kht=pair_t(_bf(kh[:, sl])),
                    wc=wc[:, sl], bonus=bonus[ra, sl], g=g[ra, sl]))

    def s_products(cd):
        cd["pm"] = _dot_nt(jnp.concatenate([cd["at"], _bf(cd["rt"])], axis=0), cd.pop("rhs"))

    def s_masks(cd):
        pm = cd.pop("pm")
        cd["a_ab"] = jnp.where(strict, pm[0:c, 0:128], 0.0)
        cd["p_rb"] = _bf(jnp.where(incl, pm[c:2 * c, 0:128], 0.0))
        a_ak = _bf(jnp.where(strict, pm[0:c, 128:256], 0.0))
        p_rk = _bf(jnp.where(incl, pm[c:2 * c, 128:256], 0.0))
        cd["xv"] = _dot(jnp.concatenate([a_ak, p_rk, cd.pop("kht")], axis=0), cd.pop("vbd"))

    def s_d2(cd):
        cd["d"] = jnp.where(blk_masks[0], cd["a_ab"], 0.0)
        db = _bf(cd["d"])
        cd["bdd"] = bd(db)
        cd["d2"] = _dot(db, cd["bdd"])

    def s_d34(cd):
        d2b = _bf(cd["d2"])
        cd["d34"] = _dot(d2b, jnp.concatenate([cd.pop("bdd"), bd(d2b)], axis=1))

    def s_t0(cd):
        d34 = cd.pop("d34")
        s3 = eye_f + cd.pop("d") + cd.pop("d2") + d34[:, 0:128]
        cd["t"] = s3 + _dot(_bf(d34[:, 128:256]), bd(_bf(s3)))

    def s_inner(level):
        def run(cd):
            cd["tb"] = _bf(cd["t"])
            cd["inner"] = _bf(_dot(_bf(jnp.where(blk_masks[level], cd["a_ab"], 0.0)), bd(cd["tb"])))
        return run

    def s_merge(cd):
        cd["t"] = cd["t"] + _dot(cd.pop("tb"), bd(cd.pop("inner")))

    def s_gu(cd):
        cd.pop("a_ab")
        gu = _bf(_dot(_bf(cd.pop("t")),
                      jnp.concatenate([bd(cd.pop("at")), bd(_bf(cd["xv"][0:c]))], axis=1)))
        cd["gu"] = jnp.concatenate([bd(gu[:, 0:128]), bd(gu[:, 128:256])], axis=1)

    def s_maps(cd):
        r = _dot(jnp.concatenate([cd.pop("p_rb"), cd.pop("bht")], axis=0), cd.pop("gu"))
        xv = cd.pop("xv")
        cd["y0"] = r[0:c, 128:256] + xv[c:2 * c]
        cd["n_add"] = r[c:2 * c, 128:256] + xv[2 * c:3 * c]
        cd["mq_lhs"] = _bf(jnp.concatenate([r[c:2 * c, 0:128], cd.pop("rt") + r[0:c, 0:128]], axis=0))

    stages = [s_products, s_masks, s_d2, s_d34, s_t0]
    for level in range(1, len(blk_masks)):
        stages += [s_inner(level), s_merge]
    stages += [s_gu, s_maps]
    for stage in stages:
        for cd in chains:
            stage(cd)
    wcc = _dot_split(jnp.concatenate(
        [jnp.where(eye, jnp.broadcast_to(cd["wc"], (c, 128)), 0.0) for cd in chains], axis=0), ones)
    for i, cd in enumerate(chains):
        cd["wc_col"] = wcc[c * i:c * (i + 1)]

    states = {(b, j): s_ref[b, j] for b in range(nb) for j in range(HEADS // 2)}
    for ch in range(rows // c):
        for cd in chains:
            if cd["ch"] != ch:
                continue
            st = states[(cd["b"], cd["j"])]
            mq = _dot(cd["mq_lhs"], bd(_bf(st)))
            states[(cd["b"], cd["j"])] = cd["wc_col"] * st + mq[0:c] + cd["n_add"]
            cd["y"] = mq[c:2 * c] + cd["y0"]
    for (b, j), st in states.items():
        s_ref[b, j] = st

    ys = jnp.concatenate([cd["y"] for cd in chains], axis=0)
    dy = ys - _dot_split(ys, ones) * (1.0 / HEAD_DIM)
    var = _dot(_bf(dy * dy), ones) * (1.0 / HEAD_DIM)
    zn = dy * lax.rsqrt(var + GN_EPS)
    for i, cd in enumerate(chains):
        sl = cd["sl"]
        yn = zn[c * i:c * (i + 1)] * lnw_ref[:, sl] + lnb_ref[:, sl]
        o_ref[cd["b"], cd["rs"], sl] = ((yn + cd["bonus"]) * cd["g"]).astype(o_ref.dtype)


def _slice_spec(shape, axis, unit, nsteps):
    size = shape[axis]
    blk = min(b for b in range(unit, size + 1, unit) if size % b == 0 and size // b <= nsteps)
    last = size // blk - 1
    block = tuple(blk if a == axis else s for a, s in enumerate(shape))
    return pl.BlockSpec(block, lambda i: tuple(jnp.minimum(i, last) if a == axis else 0
                                               for a in range(len(shape))))


def _rwkv(pr3, mu, w0, decup_pad, a0, aup_pad, gup, k_k, k_a, lnx_w, lnx_b, r_k, later_weights):
    batch, lp, _ = pr3.shape
    rows = CHUNK * CHUNKS_PER_STEP
    const = lambda i: (0, 0)
    vec = pl.BlockSpec((1, RWKV_DIM), const)
    lora = pl.BlockSpec((128, RWKV_DIM), const)
    cast_specs = [_slice_spec(w.shape, axis, 16 if axis == 0 else 128, lp // rows)
                  for w, axis in later_weights]
    outs = pl.pallas_call(
        _rwkv_kernel,
        grid=(lp // rows,),
        in_specs=[
            pl.BlockSpec((batch, rows, RWKV_IN), lambda i: (0, i, 0)),
            pl.BlockSpec((batch, 16, RWKV_IN),
                         lambda i: (0, jnp.maximum(i * (rows // 16) - 1, 0), 0)),
            pl.BlockSpec((1, RWKV_IN), const),
            vec, pl.BlockSpec((3 * 128, RWKV_DIM), const), vec, lora, lora, vec, vec, vec, vec, vec,
        ] + cast_specs,
        out_specs=[pl.BlockSpec((batch, rows, RWKV_DIM), lambda i: (0, i, 0))] + cast_specs,
        out_shape=[jax.ShapeDtypeStruct((batch, lp, RWKV_DIM), BF16)]
        + [jax.ShapeDtypeStruct(w.shape, BF16) for w, _ in later_weights],
        scratch_shapes=[pltpu.VMEM((batch, HEADS // 2, CHUNK, 128), F32)],
        compiler_params=pltpu.CompilerParams(
            dimension_semantics=("arbitrary",), vmem_limit_bytes=VMEM_LIMIT),
    )(pr3, pr3, mu, w0, decup_pad, a0, aup_pad, gup, k_k, k_a, lnx_w, lnx_b, r_k,
      *[w for w, _ in later_weights])
    return outs[0], outs[1:]


def _t5_thresholds():
    n_log = REL_BUCKETS - REL_MAX_EXACT
    out = []
    for k in range(1, n_log):
        x = REL_MAX_EXACT * (REL_MAX_DIST / REL_MAX_EXACT) ** (k / n_log)
        assert min(x - math.floor(x), math.ceil(x) - x) > 1e-3
        out.append(math.ceil(x))
    return out


def _t5_bucket(d):
    d = jnp.maximum(d, 0)
    large = jnp.full(d.shape, REL_MAX_EXACT, jnp.int32)
    for t in _t5_thresholds():
        large = large + (d >= t).astype(jnp.int32)
    return jnp.where(d < REL_MAX_EXACT, d, large)


def _attn_kernel(rb_ref, sink_ref, q_ref, kv0_ref, kvp_ref, kvc_ref, qw_ref, kw_ref, o_ref, tbl_ref):
    n = pl.program_id(1)

    @pl.when((pl.program_id(0) == 0) & (n == 0))
    def _():
        q = lax.broadcasted_iota(jnp.int32, (BLOCK, 3 * BLOCK), 0)
        col = lax.broadcasted_iota(jnp.int32, (BLOCK, 3 * BLOCK), 1)
        is_meta = col < BLOCK
        for nn in range(3):
            d_meta = nn * BLOCK + q - col
            d = jnp.where(is_meta, d_meta, q + 2 * BLOCK - col)
            ok = is_meta & (col >= BLOCK - N_META) & (d_meta >= 0)
            if nn >= 2:
                ok = ok | ((col >= BLOCK) & (col < 2 * BLOCK) & (col - BLOCK > q))
            if nn >= 1:
                ok = ok | ((col >= 2 * BLOCK) & (col - 2 * BLOCK <= q))
            bucket = _t5_bucket(d)

            def per_head(h, carry):
                acc = jnp.zeros((BLOCK, 3 * BLOCK), F32)
                for bk in range(REL_BUCKETS):
                    acc = jnp.where(bucket == bk, rb_ref[bk, h], acc)
                tbl_ref[nn, h] = jnp.where(ok, acc, MASK_VALUE)
                return carry

            lax.fori_loop(0, HEADS, per_head, 0)

    ones = _head_ones(128)
    lane_lo = lax.broadcasted_iota(jnp.int32, (BLOCK, 128), 1) < HEAD_DIM
    lane_hi = jnp.logical_not(lane_lo)
    rr = lax.broadcasted_iota(jnp.int32, (128, 128), 0)
    cc = lax.broadcasted_iota(jnp.int32, (128, 128), 1)
    dup = [((rr // HEAD_DIM == g) & (rr % HEAD_DIM == cc % HEAD_DIM)).astype(BF16) for g in range(2)]

    def qk_norm(x, w):
        ms = _dot(_bf(x * x), ones) * (1.0 / HEAD_DIM)
        return x * lax.rsqrt(ms + NORM_EPS) * w

    nbk = q_ref.shape[0] // BLOCK
    grp = HEADS // 2
    qw = qw_ref[...] * HEAD_DIM ** -0.5

    kvb = [kv0_ref[...], kvp_ref[...]] + [kvc_ref[BLOCK * jb:BLOCK * (jb + 1), :] for jb in range(nbk)]
    kvb = [x.astype(F32) for x in kvb]
    kn = [_bf(qk_norm(x[:, 0:128], kw_ref[...])) for x in kvb]
    kd = [[_bf(_dot(x, dup[g])) for x in kn] for g in range(2)]
    vd = [[_bf(_dot(_bf(x[:, 128:256]), dup[g])) for x in kvb] for g in range(2)]
    qn = [[qk_norm(q_ref[BLOCK * jb:BLOCK * (jb + 1), 128 * c4:128 * (c4 + 1)].astype(F32), qw)
           for c4 in range(4)] for jb in range(nbk)]
    sinks = [jnp.concatenate([jnp.full((BLOCK, 128), sink_ref[0, grp * g + i], F32)
                              for i in range(grp)], axis=0) for g in range(2)]
    ones_cols = jnp.ones((3 * BLOCK, 128), BF16)

    def scores(c):
        jb, g = c["jb"], c["g"]
        lhs = jnp.concatenate(
            [jnp.where(lane_lo if e == 0 else lane_hi, qn[jb][2 * g + jj], 0.0)
             for jj in range(2) for e in range(2)], axis=0).astype(BF16)
        keys = jnp.concatenate([kd[g][0], kd[g][1 + jb], kd[g][2 + jb]], axis=0)
        tsel = jnp.clip(n * nbk + jb - META_BLOCK, 0, 2)
        tb = tbl_ref[tsel, pl.ds(grp * g, grp)].reshape(grp * BLOCK, 3 * BLOCK)
        c["s"] = _dot_nt(lhs, keys) + tb

    def row_max(c):
        x = c["s"]
        m3 = jnp.maximum(jnp.maximum(x[:, 0:128], x[:, 128:256]), x[:, 256:384])
        m = jnp.maximum(jnp.max(m3, axis=-1, keepdims=True), sinks[c["g"]][:, 0:1])
        c["mb"] = jnp.broadcast_to(m, (grp * BLOCK, 128))

    def exps(c):
        x, mm = c.pop("s"), c["mb"]
        c["ex"] = jnp.concatenate(
            [_bf(jnp.exp(x[:, 128 * i:128 * (i + 1)] - mm)) for i in range(3)], axis=1)

    def values(c):
        jb, g = c["jb"], c["g"]
        vals = jnp.concatenate([vd[g][0], vd[g][1 + jb], vd[g][2 + jb]], axis=0)
        od = _dot(c.pop("ex"), jnp.concatenate([vals, ones_cols], axis=1))
        den = od[:, 128:256] + jnp.exp(sinks[g] - c.pop("mb"))
        o = od[:, 0:128] * (1.0 / den)
        for jj in range(2):
            col = 2 * g + jj
            o_ref[BLOCK * jb:BLOCK * (jb + 1), 128 * col:128 * (col + 1)] = jnp.where(
                lane_lo, o[256 * jj:256 * jj + 128], o[256 * jj + 128:256 * jj + 256]
            ).astype(o_ref.dtype)

    stages = (scores, row_max, exps, values)
    chains = [dict(jb=jb, g=g) for jb in range(nbk) for g in range(2)]
    for w in range(len(chains) + len(stages) - 1):
        for ci, c in enumerate(chains):
            if 0 <= w - ci < len(stages):
                stages[w - ci](c)


def _attention(q, kv, rel_bias, sinks, q_norm_w2, k_norm_w2, batch, lp, tm):
    nb = lp // BLOCK
    nt = lp // tm
    nbk = tm // BLOCK
    return pl.pallas_call(
        _attn_kernel,
        grid=(batch, nt),
        in_specs=[
            pl.BlockSpec(memory_space=pltpu.SMEM),
            pl.BlockSpec(memory_space=pltpu.SMEM),
            pl.BlockSpec((tm, ATTN_DIM), lambda b, n: (b * nt + n, 0)),
            pl.BlockSpec((BLOCK, 2 * KV_DIM), lambda b, n: (b * nb + META_BLOCK, 0)),
            pl.BlockSpec((BLOCK, 2 * KV_DIM),
                         lambda b, n: (b * nb + jnp.maximum(n * nbk - 1, 0), 0)),
            pl.BlockSpec((tm, 2 * KV_DIM), lambda b, n: (b * nt + n, 0)),
            pl.BlockSpec((1, 128), lambda b, n: (0, 0)),
            pl.BlockSpec((1, 128), lambda b, n: (0, 0)),
        ],
        out_specs=pl.BlockSpec((tm, ATTN_DIM), lambda b, n: (b * nt + n, 0)),
        out_shape=jax.ShapeDtypeStruct(q.shape, BF16),
        scratch_shapes=[pltpu.VMEM((3, HEADS, BLOCK, 3 * BLOCK), F32)],
        compiler_params=pltpu.CompilerParams(
            dimension_semantics=("arbitrary", "arbitrary"), vmem_limit_bytes=VMEM_LIMIT),
    )(rel_bias, sinks, q, kv, kv, kv, q_norm_w2, k_norm_w2)


def _mixed(h_in, yr, ya, wo_ref):
    return (h_in + _dot(_bf(yr), wo_ref[0:RWKV_DIM, :]) + _dot(_bf(ya), wo_ref[RWKV_DIM:, :]))


def _normed(h, nw_ref):
    ms = jnp.mean(h * h, axis=-1, keepdims=True)
    return _bf(h * lax.rsqrt(ms + NORM_EPS) * nw_ref[...])


FFN_SUB = 4


def _ffn_kernel(x_ref, *refs):
    tm = LEAD
    nsub = x_ref.shape[0] // tm
    yr_refs, ya_refs = refs[0:nsub], refs[nsub:2 * nsub]
    (meta_ref, yrm_ref, yam_ref, wo_ref, nw_ref, wup_ref, cw_ref, cb_ref, wdn_ref,
     o_ref, carry_ref, hbuf_ref) = refs[2 * nsub:]
    cw = 256
    nj = D_FF // cw
    nslot = hbuf_ref.shape[0]

    @pl.when(pl.program_id(1) == 0)
    def _():
        h = _mixed(meta_ref[...], yrm_ref[...], yam_ref[...], wo_ref)
        carry_ref[...] = _dot(_normed(h, nw_ref), wup_ref[...])[N_META - 8:N_META, :]

    subs = range(nsub)
    hres = [_mixed(x_ref[tm * s:tm * (s + 1), :], yr_refs[s][...], ya_refs[s][...], wo_ref)
            for s in subs]
    u = [_normed(h, nw_ref) for h in hres]
    acc = [None for _ in subs]

    def up(s, j):
        return [_dot(u[s], wup_ref[:, base + cw * j:base + cw * (j + 1)]) for base in (0, D_FF)]

    items = [(s, j) for s in subs for j in range(nj)]
    hids = up(*items[0])
    for n, (s, j) in enumerate(items):
        nxt = up(*items[n + 1]) if n + 1 < len(items) else None
        halves = []
        for half, (base, hid) in enumerate(zip((0, D_FF), hids)):
            lo = base + cw * j
            buf = hbuf_ref.at[(2 * n + half) % nslot]
            buf[0:8, :] = carry_ref[:, lo:lo + cw]
            buf[8:8 + tm, :] = hid
            carry_ref[:, lo:lo + cw] = hid[tm - 8:tm, :]
            halves.append(hid * cw_ref[0:1, lo:lo + cw]
                          + buf[7:7 + tm, :] * cw_ref[1:2, lo:lo + cw]
                          + buf[6:6 + tm, :] * cw_ref[2:3, lo:lo + cw]
                          + cb_ref[:, lo:lo + cw])
        gate, val = halves
        act = (gate * jax.nn.sigmoid(gate) * val).astype(BF16)
        down = _dot(act, wdn_ref[cw * j:cw * (j + 1), :])
        acc[s] = down if acc[s] is None else acc[s] + down
        hids = nxt
        if j == nj - 1:
            o_ref[tm * s:tm * (s + 1), :] = hres[s] + acc[s]


def _ffn(x2, meta, yr, ya, w_out, norm_w, w_up, conv_w, conv_b, w_down, batch, seq):
    nlead = (LEAD + seq) // LEAD
    nsub = max(m for m in range(1, FFN_SUB + 1) if (nlead - 1) % m == 0)
    meta_rows = lambda b, i: ((b * nlead + 1) * (LEAD // N_META) - 1, 0)
    tm = nsub * LEAD
    nt = seq // tm
    row = lambda b, i: (b * nt + i, 0)
    const = lambda b, i: (0, 0)
    sub = lambda width: [pl.BlockSpec((LEAD, width),
                                      lambda b, i, s=s: (b * nlead + 1 + nsub * i + s, 0))
                         for s in range(nsub)]
    return pl.pallas_call(
        _ffn_kernel,
        grid=(batch, nt),
        in_specs=[pl.BlockSpec((tm, D_MODEL), row)] + sub(RWKV_DIM) + sub(ATTN_DIM) + [
            pl.BlockSpec((N_META, D_MODEL), const),
            pl.BlockSpec((N_META, RWKV_DIM), meta_rows),
            pl.BlockSpec((N_META, ATTN_DIM), meta_rows),
            pl.BlockSpec((D_MODEL, D_MODEL), const),
            pl.BlockSpec((1, D_MODEL), const),
            pl.BlockSpec((D_MODEL, 2 * D_FF), const),
            pl.BlockSpec((3, 2 * D_FF), const),
            pl.BlockSpec((1, 2 * D_FF), const),
            pl.BlockSpec((D_FF, D_MODEL), const),
        ],
        out_specs=pl.BlockSpec((tm, D_MODEL), row),
        out_shape=jax.ShapeDtypeStruct(x2.shape, F32),
        scratch_shapes=[pltpu.VMEM((8, 2 * D_FF), F32),
                        pltpu.VMEM((8, LEAD + 8, 256), F32)],
        compiler_params=pltpu.CompilerParams(
            dimension_semantics=("arbitrary", "arbitrary"), vmem_limit_bytes=VMEM_LIMIT),
    )(x2, *([yr] * nsub), *([ya] * nsub), meta, yr, ya, w_out, norm_w, w_up, conv_w, conv_b,
      w_down)


def _row_tile(lp, cap, unit=BLOCK):
    n = lp // unit
    best = 1
    for f in range(1, n + 1):
        if n % f == 0 and f * unit <= cap:
            best = f
    return best * unit


def kernel(x, meta_tokens, rel_bias, norm1_w, w_in, shift_mu, decay_w0, decay_up, aaa_a0, aaa_up, gate_up, k_k, k_a, r_k, lnx_w, lnx_b, q_norm_w, k_norm_w, sinks, w_out, norm2_w, w_up, conv_w, conv_b, w_down):
    batch, seq, _ = x.shape
    assert norm1_w.shape[0] == 1 and seq % LEAD == 0
    lp = LEAD + seq
    layer = 0
    x2 = x.reshape(batch * seq, D_MODEL)
    meta = meta_tokens.astype(x.dtype)
    row2 = lambda t: t.reshape(1, -1)
    zeros64 = jnp.zeros((64, RWKV_DIM), F32)
    dec = jnp.concatenate([decay_up[layer], zeros64], axis=0)
    dec_hi = dec.astype(BF16)
    dec_lo = (dec - dec_hi.astype(F32)).astype(BF16)
    pr, q, kv = _inproj(x2, meta, row2(norm1_w[layer]), w_in[layer], batch, seq)
    y_rwkv, (w_out_b, w_up_b, w_down_b) = _rwkv(
        pr.reshape(batch, lp, RWKV_IN), row2(shift_mu[layer]), row2(decay_w0[layer]),
        jnp.concatenate([dec_hi, dec_hi, dec_lo], axis=0), row2(aaa_a0[layer]),
        jnp.concatenate([zeros64, aaa_up[layer]], axis=0).astype(BF16),
        gate_up[layer].astype(BF16), row2(k_k[layer]), row2(k_a[layer]),
        row2(lnx_w[layer]), row2(lnx_b[layer]), row2(r_k[layer]),
        ((w_out[layer], 0), (w_up[layer], 1), (w_down[layer], 0)))
    y_attn = _attention(q, kv, rel_bias, sinks[layer].reshape(1, HEADS),
                        jnp.tile(q_norm_w[layer], 2).reshape(1, 128),
                        jnp.tile(k_norm_w[layer], 2).reshape(1, 128), batch, lp,
                        _row_tile(lp, 1408))
    y_rwkv = y_rwkv.reshape(batch * lp, RWKV_DIM)
    out = _ffn(x2, meta, y_rwkv, y_attn, w_out_b, row2(norm2_w[layer]), w_up_b, conv_w[layer],
               row2(conv_b[layer]), w_down_b, batch, seq)
    return out.reshape(batch, seq, D_MODEL)
```
